```python
import jax, jax.numpy as jnp
from jax import lax
import numpy as np

D_MODEL = 1024
BATCH = 16
SEQ = 4096
DEPTH = 4

GRID_W = 64
CTX_LEN = 256
HEAD_DIM = 64
RET_HEADS = 4
RET_DK = 64
RET_DV = 128
RET_CHUNK = 128
WIN_Q_HEADS = 4
WIN_KV_HEADS = 2
WINDOW = 128
GLB_Q_HEADS = 4
GLB_KV_HEADS = 2
BLOCK_Q = 128
ROPE_BASE = 10000.0
N_EXPERTS = 16
N_GROUPS = 4
EXPERTS_PER_GROUP = N_EXPERTS // N_GROUPS
TOP_K = 2
D_EXPERT = 1024
MOE_BLOCK = 256
EPS = 1e-6
ADA_INIT = 0.3
NEG_INF = -1e30
SPLIT_SIZES = (RET_HEADS * RET_DK, RET_HEADS * RET_DK, RET_HEADS * RET_DV, RET_HEADS * RET_DV,
               WIN_Q_HEADS * HEAD_DIM, WIN_KV_HEADS * HEAD_DIM, WIN_KV_HEADS * HEAD_DIM,
               GLB_Q_HEADS * HEAD_DIM, GLB_KV_HEADS * HEAD_DIM, GLB_KV_HEADS * HEAD_DIM)
D_IN = sum(SPLIT_SIZES)
D_MIX = RET_HEADS * RET_DV + WIN_Q_HEADS * HEAD_DIM + GLB_Q_HEADS * HEAD_DIM

kernel_name = 'hybrid_parallel_heads_retention_swa_axial_moe_dit'


def _rms(x, gain):
    x32 = x.astype(jnp.float32)
    y = x32 * lax.rsqrt(jnp.mean(x32 * x32, axis=-1, keepdims=True) + EPS)
    return (y * gain.astype(jnp.float32)).astype(x.dtype)


def _modulate(h, shift, scale):
    return h * (1 + scale) + shift


def _split_cols(p):
    outs, start = [], 0
    for s in SPLIT_SIZES:
        outs.append(p[..., start:start + s])
        start += s
    return outs


def _heads(t, n_heads):
    b, l, _ = t.shape
    return t.reshape(b, l, n_heads, -1).transpose(0, 2, 1, 3)


def _rope_tables(n_tok, dtype):
    rows = n_tok // GRID_W
    row = jnp.repeat(jnp.arange(rows), GRID_W).astype(jnp.float32)
    col = (jnp.arange(rows * GRID_W) % GRID_W).astype(jnp.float32)
    half = HEAD_DIM // 2
    inv = jnp.power(ROPE_BASE, -jnp.arange(0, half, 2, dtype=jnp.float32) / half)
    ang = jnp.concatenate([row[:, None] * inv, col[:, None] * inv], -1)
    return jnp.cos(ang).astype(dtype), jnp.sin(ang).astype(dtype)


def _rope_2d(x, cos, sin):
    q = HEAD_DIM // 4
    parts = []
    for a in range(2):
        xa = x[..., 2 * a * q:(2 * a + 2) * q]
        x1, x2 = xa[..., :q], xa[..., q:]
        c, s = cos[:, a * q:(a + 1) * q], sin[:, a * q:(a + 1) * q]
        parts += [x1 * c - x2 * s, x2 * c + x1 * s]
    return jnp.concatenate(parts, -1)


def _attn_prep(q, k, v, n_q, n_kv, gain, rope):
    q = _rms(_heads(q, n_q), gain[0])
    k = _rms(_heads(k, n_kv), gain[1])
    v = _heads(v, n_kv)
    if rope is not None:
        q = _rope_2d(q, *rope)
        k = _rope_2d(k, *rope)
    b, _, l, _ = q.shape
    q = (q * HEAD_DIM ** -0.5).reshape(b, n_kv, n_q // n_kv, l, HEAD_DIM)
    return q, k, v


def _attend(q, k, v, mask, sink):
    s = jnp.einsum('bhgqd,bhkd->bhgqk', q, k).astype(jnp.float32)
    if mask is not None:
        s = jnp.where(mask, s, NEG_INF)
    m = jnp.max(s, axis=-1, keepdims=True)
    if sink is not None:
        sk = sink.astype(jnp.float32)[None, :, :, None, None]
        m = jnp.maximum(m, sk)
    p = jnp.exp(s - m)
    den = jnp.sum(p, axis=-1, keepdims=True)
    if sink is not None:
        den = den + jnp.exp(sk - m)
    return jnp.einsum('bhgqk,bhkd->bhgqd', (p / den).astype(v.dtype), v)


def _to_blocks(q):
    b, hk, g, l, hd = q.shape
    return jnp.moveaxis(q.reshape(b, hk, g, l // BLOCK_Q, BLOCK_Q, hd), 3, 0)


def _from_blocks(o):
    nb, b, hk, g, bq, hd = o.shape
    return jnp.moveaxis(o, 0, 3).reshape(b, hk, g, nb * bq, hd)


def _merge_heads(o):
    b, hk, g, l, hd = o.shape
    return o.reshape(b, hk * g, l, hd).transpose(0, 2, 1, 3).reshape(b, l, hk * g * hd)


def _ret_scan(q, k, v, log_g, s0):
    b, h, l, _ = q.shape
    n = l // RET_CHUNK
    idx = jnp.arange(RET_CHUNK, dtype=jnp.float32)
    diff = idx[:, None] - idx[None, :]
    d_in = jnp.where(diff >= 0, jnp.exp(jnp.maximum(diff, 0.0) * log_g[:, None, None]), 0.0).astype(q.dtype)
    xi = jnp.exp((idx + 1.0) * log_g[:, None]).astype(q.dtype)[..., None]
    zeta = jnp.exp((RET_CHUNK - 1.0 - idx) * log_g[:, None]).astype(q.dtype)[..., None]
    g_c = jnp.exp(RET_CHUNK * log_g).astype(q.dtype)[:, None, None]

    def chunks(t):
        return jnp.moveaxis(t.reshape(b, h, n, RET_CHUNK, t.shape[-1]), 2, 0)

    def step(s, inp):
        qi, ki, vi = inp
        att = jnp.einsum('bhqd,bhkd->bhqk', qi, ki) * d_in
        o = jnp.einsum('bhqk,bhkv->bhqv', att, vi) + jnp.einsum('bhqd,bhdv->bhqv', qi, s) * xi
        s = g_c * s + jnp.einsum('bhkd,bhkv->bhdv', ki * zeta, vi)
        return s, o

    s_fin, o = lax.scan(step, s0, (chunks(q), chunks(k), chunks(v)))
    return jnp.moveaxis(o, 0, 2).reshape(b, h, l, -1), s_fin


def _ret_out(o, g):
    o32 = o.astype(jnp.float32)
    mu = jnp.mean(o32, axis=-1, keepdims=True)
    var = jnp.mean(jnp.square(o32 - mu), axis=-1, keepdims=True)
    o = ((o32 - mu) * lax.rsqrt(var + EPS)).astype(g.dtype)
    b, h, l, dv = o.shape
    return jax.nn.silu(g) * o.transpose(0, 2, 1, 3).reshape(b, l, h * dv)


def _retention(q, k, v, g, qc, kc, vc, gc, decay_logit, need_ctx):
    log_g = jax.nn.log_sigmoid(decay_logit.astype(jnp.float32))

    def prep(q, k, v):
        return _heads(q, RET_HEADS), _heads(k, RET_HEADS) * RET_DK ** -0.5, _heads(v, RET_HEADS)

    q, k, v = prep(q, k, v)
    qc, kc, vc = prep(qc, kc, vc)
    flip = lambda t: jnp.flip(t, axis=2)
    s0 = jnp.zeros((q.shape[0], RET_HEADS, RET_DK, RET_DV), q.dtype)
    oc_f, s_f = _ret_scan(qc, kc, vc, log_g[0], s0)
    oc_b, s_b = _ret_scan(flip(qc), flip(kc), flip(vc), log_g[1], s0)
    o_f, _ = _ret_scan(q, k, v, log_g[0], s_f)
    o_b, _ = _ret_scan(flip(q), flip(k), flip(v), log_g[1], s_b)
    y = _ret_out(o_f + flip(o_b), g)
    yc = _ret_out(oc_f + flip(oc_b), gc) if need_ctx else None
    return y, yc


def _window_mixer(q, k, v, qc, kc, vc, gain, sink, rope, need_ctx):
    q, k, v = _attn_prep(q, k, v, WIN_Q_HEADS, WIN_KV_HEADS, gain, rope)
    qc, kc, vc = _attn_prep(qc, kc, vc, WIN_Q_HEADS, WIN_KV_HEADS, gain, None)
    sink = sink.reshape(WIN_KV_HEADS, WIN_Q_HEADS // WIN_KV_HEADS)
    l = q.shape[3]
    nb = l // BLOCK_Q
    span = BLOCK_Q + 2 * WINDOW
    pad = ((0, 0), (0, 0), (WINDOW, WINDOW), (0, 0))
    kp, vp = jnp.pad(k, pad), jnp.pad(v, pad)
    off = jnp.arange(span) - WINDOW
    band = jnp.abs(off[None, :] - jnp.arange(BLOCK_Q)[:, None]) <= WINDOW
    ctx_ok = jnp.ones((BLOCK_Q, kc.shape[2]), bool)

    def block(args):
        qb, i = args
        start = i * BLOCK_Q
        kw = lax.dynamic_slice_in_dim(kp, start, span, axis=2)
        vw = lax.dynamic_slice_in_dim(vp, start, span, axis=2)
        kpos = start + off
        mask = band & ((kpos >= 0) & (kpos < l))[None, :]
        return _attend(qb, jnp.concatenate([kw, kc], 2), jnp.concatenate([vw, vc], 2),
                       jnp.concatenate([mask, ctx_ok], 1), sink)

    o = lax.map(block, (_to_blocks(q), jnp.arange(nb)))
    y = _merge_heads(_from_blocks(o))
    yc = _merge_heads(_attend(qc, kc, vc, None, sink)) if need_ctx else None
    return y, yc


def _global_mixer(q, k, v, qc, kc, vc, gain, rope, need_ctx):
    q, k, v = _attn_prep(q, k, v, GLB_Q_HEADS, GLB_KV_HEADS, gain, rope)
    qc, kc, vc = _attn_prep(qc, kc, vc, GLB_Q_HEADS, GLB_KV_HEADS, gain, None)
    k_all = jnp.concatenate([k, kc], 2)
    v_all = jnp.concatenate([v, vc], 2)
    o = lax.map(lambda qb: _attend(qb, k_all, v_all, None, None), _to_blocks(q))
    y = _merge_heads(_from_blocks(o))
    yc = _merge_heads(_attend(qc, kc, vc, None, None)) if need_ctx else None
    return y, yc


def _mixers(px, pc, ret_decay, win_gain, win_sink, glb_gain, rope, need_ctx):
    rq, rk, rv, rg, wq, wk, wv, gq, gk, gv = _split_cols(px)
    rqc, rkc, rvc, rgc, wqc, wkc, wvc, gqc, gkc, gvc = _split_cols(pc)
    yr, yrc = _retention(rq, rk, rv, rg, rqc, rkc, rvc, rgc, ret_decay, need_ctx)
    yw, ywc = _window_mixer(wq, wk, wv, wqc, wkc, wvc, win_gain, win_sink, rope, need_ctx)
    yg, ygc = _global_mixer(gq, gk, gv, gqc, gkc, gvc, glb_gain, rope, need_ctx)
    y = jnp.concatenate([yr, yw, yg], -1)
    yc = jnp.concatenate([yrc, ywc, ygc], -1) if need_ctx else None
    return y, yc


def _route(h, w_router, b_router):
    s = jax.nn.sigmoid(jnp.dot(h, w_router).astype(jnp.float32))
    grouped = (s + b_router.astype(jnp.float32)).reshape(-1, N_GROUPS, EXPERTS_PER_GROUP)
    gscore = lax.top_k(grouped, TOP_K)[0].sum(-1)
    grp = jnp.argmax(gscore, axis=-1)
    cand = jnp.take_along_axis(grouped, grp[:, None, None], axis=1)[:, 0]
    _, loc = lax.top_k(cand, TOP_K)
    e_idx = grp[:, None] * EXPERTS_PER_GROUP + loc
    w = jnp.take_along_axis(s, e_idx, axis=1)
    return e_idx, w / jnp.sum(w, axis=-1, keepdims=True)


def _moe(h, w_router, b_router, w_gu, w_down):
    t, d = h.shape
    e_idx, gate = _route(h, w_router, b_router)
    flat_e = e_idx.reshape(-1)
    order = jnp.argsort(flat_e)
    e_sorted = flat_e[order]
    tok = order // TOP_K
    counts = jnp.bincount(flat_e, length=N_EXPERTS)
    padded = (counts + MOE_BLOCK - 1) // MOE_BLOCK * MOE_BLOCK
    pad_end = jnp.cumsum(padded)
    dest = (pad_end - padded)[e_sorted] + jnp.arange(t * TOP_K) - (jnp.cumsum(counts) - counts)[e_sorted]
    n_blocks = (t * TOP_K + N_EXPERTS * (MOE_BLOCK - 1) + MOE_BLOCK - 1) // MOE_BLOCK
    xp = jnp.zeros((n_blocks * MOE_BLOCK, d), h.dtype).at[dest].set(h[tok])
    blk_e = jnp.minimum(jnp.searchsorted(pad_end, jnp.arange(n_blocks) * MOE_BLOCK, side='right'), N_EXPERTS - 1)

    def expert_block(args):
        xb, e = args
        a, u = jnp.split(xb @ w_gu[e], 2, axis=-1)
        return (jax.nn.silu(a) * u) @ w_down[e]

    yp = lax.map(expert_block, (xp.reshape(n_blocks, MOE_BLOCK, d), blk_e)).reshape(-1, d)
    y = yp[dest] * gate.reshape(-1)[order][:, None].astype(h.dtype)
    return jnp.zeros_like(h).at[tok].add(y)


def setup_inputs(seed: int = 0) -> dict:
    key = jax.random.key(seed)
    ks = jax.random.split(key, 20)
    f32 = jnp.float32
    nrm = lambda k, shape, scale: jax.random.normal(k, shape, f32) * scale
    base_gamma = 1.0 - jnp.power(2.0, -5.0 - jnp.arange(RET_HEADS, dtype=f32))
    base_logit = jnp.log(base_gamma) - jnp.log1p(-base_gamma)
    return {
        'x': nrm(ks[0], (BATCH, SEQ, D_MODEL), 1.0),
        'c': nrm(ks[1], (BATCH, D_MODEL), 1.0),
        'ctx': nrm(ks[2], (BATCH, CTX_LEN, D_MODEL), 1.0),
        'c_ctx': nrm(ks[3], (D_MODEL,), 1.0),
        'ada_w': nrm(ks[4], (DEPTH, D_MODEL, 6 * D_MODEL), ADA_INIT * D_MODEL ** -0.5),
        'ada_b': nrm(ks[5], (DEPTH, 6 * D_MODEL), 0.02),
        'norm1': 1.0 + nrm(ks[6], (DEPTH, D_MODEL), 0.02),
        'norm2': 1.0 + nrm(ks[7], (DEPTH, D_MODEL), 0.02),
        'w_in': nrm(ks[8], (DEPTH, D_MODEL, D_IN), D_MODEL ** -0.5),
        'w_out': nrm(ks[9], (DEPTH, D_MIX, D_MODEL), D_MIX ** -0.5),
        'ret_decay': base_logit[None, None, :] + nrm(ks[10], (DEPTH, 2, RET_HEADS), 0.1),
        'win_qk_gain': 1.0 + nrm(ks[11], (DEPTH, 2, HEAD_DIM), 0.02),
        'win_sink': nrm(ks[12], (DEPTH, WIN_Q_HEADS), 0.5),
        'glb_qk_gain': 1.0 + nrm(ks[13], (DEPTH, 2, HEAD_DIM), 0.02),
        'w_router': nrm(ks[14], (D_MODEL, N_EXPERTS), D_MODEL ** -0.5),
        'b_router': nrm(ks[15], (N_EXPERTS,), 0.01),
        'w_gate_up': nrm(ks[16], (DEPTH, N_EXPERTS, D_MODEL, 2 * D_EXPERT), D_MODEL ** -0.5),
        'w_down': nrm(ks[17], (DEPTH, N_EXPERTS, D_EXPERT, D_MODEL), D_EXPERT ** -0.5),
    }


def reference(x, c, ctx, c_ctx, ada_w, ada_b, norm1, norm2, w_in, w_out, ret_decay, win_qk_gain, win_sink,
              glb_qk_gain, w_router, b_router, w_gate_up, w_down):
    b, l, d = x.shape
    rope = _rope_tables(l, x.dtype)
    sc, sc_ctx = jax.nn.silu(c), jax.nn.silu(c_ctx)
    for i in range(DEPTH):
        last = i == DEPTH - 1
        mod = (sc @ ada_w[i] + ada_b[i])[:, None, :]
        mod_c = sc_ctx @ ada_w[i] + ada_b[i]
        sh1, s1, g1, sh2, s2, g2 = jnp.split(mod, 6, axis=-1)
        sh1c, s1c, g1c, sh2c, s2c, g2c = jnp.split(mod_c, 6, axis=-1)
        px = _modulate(_rms(x, norm1[i]), sh1, s1) @ w_in[i]
        pc = _modulate(_rms(ctx, norm1[i]), sh1c, s1c) @ w_in[i]
        yx, yc = _mixers(px, pc, ret_decay[i], win_qk_gain[i], win_sink[i], glb_qk_gain[i], rope, not last)
        x = x + g1 * (yx @ w_out[i])
        hx = _modulate(_rms(x, norm2[i]), sh2, s2).reshape(b * l, d)
        if last:
            x = x + g2 * _moe(hx, w_router, b_router, w_gate_up[i], w_down[i]).reshape(b, l, d)
        else:
            ctx = ctx + g1c * (yc @ w_out[i])
            hc = _modulate(_rms(ctx, norm2[i]), sh2c, s2c).reshape(-1, d)
            y = _moe(jnp.concatenate([hx, hc], 0), w_router, b_router, w_gate_up[i], w_down[i])
            x = x + g2 * y[:b * l].reshape(b, l, d)
            ctx = ctx + g2c * y[b * l:].reshape(ctx.shape)
    return x
```

```python
import functools

import jax
import jax.numpy as jnp
from jax import lax
from jax.experimental import pallas as pl
from jax.experimental.pallas import tpu as pltpu

GRID_W = 64
HEAD_DIM = 64
RET_HEADS = 4
RET_DK = 64
RET_DV = 128
WIN_Q_HEADS = 4
WIN_KV_HEADS = 2
WINDOW = 128
GLB_Q_HEADS = 4
GLB_KV_HEADS = 2
ROPE_BASE = 10000.0
N_EXPERTS = 16
N_GROUPS = 4
EXPERTS_PER_GROUP = N_EXPERTS // N_GROUPS
TOP_K = 2
EPS = 1e-6
NEG_INF = -1e30

RQ, RK, RV, RG = 0, 256, 512, 1024
WQ, WK, WV = 1536, 1792, 1920
GQ, GK, GV = 2048, 2304, 2432
D_IN = 2560
D_RET = RET_HEADS * RET_DV
D_ATT = WIN_Q_HEADS * HEAD_DIM

TM = 512
RET_C = 256
TQ_W = 256
TQ_G = 128
TM_E = 512
VMEM_LIMIT = 56 * 1024 * 1024

F32 = jnp.float32
BF16 = jnp.bfloat16


def _dot(a, b):
    return jnp.dot(a, b, preferred_element_type=F32)


def _dot_nt(a, b):
    return lax.dot_general(a, b, (((1,), (1,)), ((), ())), preferred_element_type=F32)


def _dot_tn(a, b):
    return lax.dot_general(a, b, (((0,), (0,)), ((), ())), preferred_element_type=F32)


def _silu(x):
    return x * jax.nn.sigmoid(x)


def _params(sem):
    return pltpu.CompilerParams(dimension_semantics=sem, vmem_limit_bytes=VMEM_LIMIT)


def _ada_kernel(c_ref, w_ref, b_ref, o_ref):
    s = _silu(c_ref[...]).astype(BF16)
    o_ref[0] = _dot(s, w_ref[0].astype(BF16)) + b_ref[0]


def _ada(cc, ada_w, ada_b):
    depth, d, n = ada_w.shape
    tn = 1536
    rows = cc.shape[0]
    return pl.pallas_call(
        _ada_kernel,
        grid=(depth, n // tn),
        in_specs=[pl.BlockSpec((rows, d), lambda l, j: (0, 0)),
                  pl.BlockSpec((1, d, tn), lambda l, j: (l, 0, j)),
                  pl.BlockSpec((1, 1, tn), lambda l, j: (l, 0, j))],
        out_specs=pl.BlockSpec((1, rows, tn), lambda l, j: (l, 0, j)),
        out_shape=jax.ShapeDtypeStruct((depth, rows, n), F32),
        compiler_params=_params(("arbitrary", "arbitrary")),
        name="ada_mod",
    )(cc, ada_w, ada_b.reshape(depth, 1, n))


def _inproj_kernel(x_ref, mod_ref, n1_ref, w_ref, cos_ref, sin_ref, gains_ref, hm_ref, o_ref):
    x = x_ref[...]
    ms = jnp.mean(x * x, axis=-1, keepdims=True)
    h = x * lax.rsqrt(ms + EPS) * n1_ref[...]
    h = h * (1.0 + mod_ref[0, 1:2, :]) + mod_ref[0, 0:1, :]
    hb = h.astype(BF16)

    def proj(lo, width):
        return _dot(hb, w_ref[:, lo:lo + width])

    def qk(lo, width, gain_row, scale):
        y = proj(lo, width)
        sq = y * y
        sq_hi = sq.astype(BF16)
        sq_lo = (sq - sq_hi.astype(F32)).astype(BF16)
        hm = hm_ref[0:width, 0:width]
        msq = _dot(sq_hi, hm) + _dot(sq_lo, hm)
        yn = y * lax.rsqrt(msq + EPS) * gains_ref[gain_row:gain_row + 1, 0:width]
        nxt = pltpu.roll(yn, width - HEAD_DIM // 4, 1)
        prv = pltpu.roll(yn, HEAD_DIM // 4, 1)
        lane = lax.broadcasted_iota(jnp.int32, yn.shape, 1)
        partner = jnp.where((lane % (HEAD_DIM // 2)) < HEAD_DIM // 4, nxt, prv)
        yr = yn * cos_ref[:, 0:width] + partner * sin_ref[:, 0:width]
        return (yr * scale).astype(BF16)

    o_ref[:, RQ:RK] = proj(RQ, RK - RQ).astype(BF16)
    o_ref[:, RK:RV] = (proj(RK, RV - RK) * RET_DK ** -0.5).astype(BF16)
    o_ref[:, RV:RG] = proj(RV, RG - RV).astype(BF16)
    o_ref[:, RG:WQ] = proj(RG, WQ - RG).astype(BF16)
    o_ref[:, WQ:WK] = qk(WQ, WK - WQ, 0, HEAD_DIM ** -0.5)
    o_ref[:, WK:WV] = qk(WK, WV - WK, 1, 1.0)
    o_ref[:, WV:GQ] = proj(WV, GQ - WV).astype(BF16)
    o_ref[:, GQ:GK] = qk(GQ, GK - GQ, 2, HEAD_DIM ** -0.5)
    o_ref[:, GK:GV] = qk(GK, GV - GK, 3, 1.0)
    o_ref[:, GV:D_IN] = proj(GV, D_IN - GV).astype(BF16)


def _inproj(xa, mod_l, n1, w_in, cos_t, sin_t, gains, hm, nb, n_x_tiles, tiles_per_seq):
    ta, d = xa.shape
    n_tiles = ta // TM

    def mod_idx(i):
        return (jnp.where(i < n_x_tiles, i // tiles_per_seq, nb), 0, 0)

    def rope_idx(i):
        return (jnp.where(i < n_x_tiles, i % tiles_per_seq, tiles_per_seq), 0)

    return pl.pallas_call(
        _inproj_kernel,
        grid=(n_tiles,),
        in_specs=[pl.BlockSpec((TM, d), lambda i: (i, 0)),
                  pl.BlockSpec((1, 6, d), mod_idx),
                  pl.BlockSpec((1, d), lambda i: (0, 0)),
                  pl.BlockSpec((d, D_IN), lambda i: (0, 0)),
                  pl.BlockSpec((TM, D_ATT), rope_idx),
                  pl.BlockSpec((TM, D_ATT), rope_idx),
                  pl.BlockSpec((8, D_ATT), lambda i: (0, 0)),
                  pl.BlockSpec((D_ATT, D_ATT), lambda i: (0, 0))],
        out_specs=pl.BlockSpec((TM, D_IN), lambda i: (i, 0)),
        out_shape=jax.ShapeDtypeStruct((ta, D_IN), BF16),
        compiler_params=_params(("parallel",)),
        name="in_proj",
    )(xa, mod_l, n1, w_in, cos_t, sin_t, gains, hm)


def _ret_kernel(gc_ref, qf_ref, kf_ref, vf_ref, qb_ref, kb_ref, vb_ref, dmat_ref, xi_ref, zeta_ref,
                of_ref, ob_ref, s_ref):
    @pl.when(pl.program_id(1) == 0)
    def _():
        s_ref[...] = jnp.zeros_like(s_ref)

    dirs = ((qf_ref, kf_ref, vf_ref, of_ref), (qb_ref, kb_ref, vb_ref, ob_ref))
    for d, (q_ref, k_ref, v_ref, o_ref) in enumerate(dirs):
        q = q_ref[...]
        k = k_ref[...]
        v = v_ref[...]
        kz = (k.astype(F32) * zeta_ref[d]).astype(BF16)
        outs = []
        for h in range(RET_HEADS):
            i = d * RET_HEADS + h
            qh = q[:, h * RET_DK:(h + 1) * RET_DK]
            kh = k[:, h * RET_DK:(h + 1) * RET_DK]
            vh = v[:, h * RET_DV:(h + 1) * RET_DV]
            att = _dot_nt(qh, kh) * dmat_ref[i]
            state = s_ref[i]
            outs.append(_dot(att.astype(BF16), vh) + _dot(qh, state.astype(BF16)) * xi_ref[i])
            s_ref[i] = gc_ref[i] * state + _dot_tn(kz[:, h * RET_DK:(h + 1) * RET_DK], vh)
        o_ref[...] = jnp.concatenate(outs, axis=-1).astype(BF16)


def _retention(px, gc, dmat, xi, zeta, nb, seq, n_ctx):
    ta = px.shape[0]
    n_x = seq // RET_C
    n_c = n_ctx // RET_C
    ctx_base = nb * n_x
    steps = n_c + n_x

    def row_f(b, c):
        return jnp.where(c < n_c, ctx_base + b * n_c + c, b * n_x + (c - n_c))

    def row_b(b, c):
        return jnp.where(c < n_c, ctx_base + b * n_c + (n_c - 1 - c), b * n_x + (steps - 1 - c))

    def spec(width, col, row):
        return pl.BlockSpec((RET_C, width), lambda b, c: (row(b, c), col))

    const3 = lambda b, c: (0, 0, 0)
    return pl.pallas_call(
        _ret_kernel,
        grid=(nb, steps),
        in_specs=[pl.BlockSpec(memory_space=pltpu.SMEM),
                  spec(256, RQ // 256, row_f), spec(256, RK // 256, row_f), spec(D_RET, RV // D_RET, row_f),
                  spec(256, RQ // 256, row_b), spec(256, RK // 256, row_b), spec(D_RET, RV // D_RET, row_b),
                  pl.BlockSpec((2 * RET_HEADS, RET_C, RET_C), const3),
                  pl.BlockSpec((2 * RET_HEADS, RET_C, RET_DV), const3),
                  pl.BlockSpec((2, RET_C, RET_HEADS * RET_DK), const3)],
        out_specs=[spec(D_RET, 0, row_f), spec(D_RET, 0, row_b)],
        out_shape=[jax.ShapeDtypeStruct((ta, D_RET), BF16)] * 2,
        scratch_shapes=[pltpu.VMEM((2 * RET_HEADS, RET_DK, RET_DV), F32)],
        compiler_params=_params(("parallel", "arbitrary")),
        name="retention",
    )(gc, px, px, px, px, px, px, dmat, xi, zeta)


def _ret_tables(decay_logit):
    log_g = jax.nn.log_sigmoid(decay_logit.astype(F32)).reshape(2 * RET_HEADS)
    idx = jnp.arange(RET_C, dtype=F32)
    diff = idx[:, None] - idx[None, :]
    lg = log_g[:, None, None]
    d_fwd = jnp.where(diff >= 0, jnp.exp(jnp.maximum(diff, 0.0) * lg), 0.0)
    d_bwd = jnp.where(diff <= 0, jnp.exp(jnp.maximum(-diff, 0.0) * lg), 0.0)
    is_bwd = (jnp.arange(2 * RET_HEADS) >= RET_HEADS)[:, None, None]
    dmat = jnp.where(is_bwd, d_bwd, d_fwd)
    pos = jnp.where(is_bwd[:, :, 0], RET_C - 1.0 - idx[None, :], idx[None, :])
    xi = jnp.exp((pos + 1.0) * log_g[:, None])
    zeta = jnp.exp((RET_C - 1.0 - pos) * log_g[:, None])
    gc = jnp.exp(RET_C * log_g)
    xi = jnp.broadcast_to(xi[:, :, None], (2 * RET_HEADS, RET_C, RET_DV))
    zeta = jnp.repeat(zeta.reshape(2, RET_HEADS, RET_C).transpose(0, 2, 1), RET_DK, axis=-1)
    return gc, dmat, xi, zeta


def _softmax_av(s_list, v_list, sink):
    m = s_list[0].max(axis=-1, keepdims=True)
    for s in s_list[1:]:
        m = jnp.maximum(m, s.max(axis=-1, keepdims=True))
    if sink is not None:
        m = jnp.maximum(m, sink)
    den = None
    acc = None
    for s, v in zip(s_list, v_list):
        p = jnp.exp(s - m)
        d = p.sum(axis=-1, keepdims=True)
        a = _dot(p.astype(BF16), v)
        den = d if den is None else den + d
        acc = a if acc is None else acc + a
    if sink is not None:
        den = den + jnp.exp(sink - m)
    return acc / den


def _stack_group(q, h, group):
    return jnp.concatenate([q[:, (h * group + g) * HEAD_DIM:(h * group + g + 1) * HEAD_DIM]
                            for g in range(group)], axis=0)


def _unstack_heads(outs, rows, group):
    cols = []
    for o in outs:
        for g in range(group):
            cols.append(o[g * rows:(g + 1) * rows])
    return jnp.concatenate(cols, axis=-1)


def _win_kernel(q_ref, k_ref, v_ref, kc_ref, vc_ref, sink_ref, o_ref, *, n_q, seq):
    i = pl.program_id(1)
    group = WIN_Q_HEADS // WIN_KV_HEADS
    span = TQ_W + 2 * WINDOW
    q = q_ref[...]
    kc = kc_ref[...]
    vc = vc_ref[...]
    row = lax.broadcasted_iota(jnp.int32, (group * TQ_W, 1), 0)

    def sink_col(h):
        s = sink_ref[0:1, h * group * HEAD_DIM:h * group * HEAD_DIM + 1]
        for g in range(1, group):
            sg = sink_ref[0:1, (h * group + g) * HEAD_DIM:(h * group + g) * HEAD_DIM + 1]
            s = jnp.where(row >= g * TQ_W, sg, s)
        return s

    @pl.when(i < n_q)
    def _():
        start = i * TQ_W
        lo = pl.multiple_of(jnp.clip(start - WINDOW, 0, seq - span), WINDOW)
        kw = k_ref[pl.ds(lo, span), :]
        vw = v_ref[pl.ds(lo, span), :]
        qpos = start + lax.broadcasted_iota(jnp.int32, (group * TQ_W, span), 0) % TQ_W
        kpos = lo + lax.broadcasted_iota(jnp.int32, (group * TQ_W, span), 1)
        band = jnp.abs(kpos - qpos) <= WINDOW
        outs = []
        for h in range(WIN_KV_HEADS):
            sl = slice(h * HEAD_DIM, (h + 1) * HEAD_DIM)
            q2 = _stack_group(q, h, group)
            s = jnp.where(band, _dot_nt(q2, kw[:, sl]), NEG_INF)
            sc = _dot_nt(q2, kc[:, sl])
            outs.append(_softmax_av([s, sc], [vw[:, sl], vc[:, sl]], sink_col(h)))
        o_ref[...] = _unstack_heads(outs, TQ_W, group).astype(BF16)

    @pl.when(i >= n_q)
    def _():
        outs = []
        for h in range(WIN_KV_HEADS):
            sl = slice(h * HEAD_DIM, (h + 1) * HEAD_DIM)
            q2 = _stack_group(q, h, group)
            outs.append(_softmax_av([_dot_nt(q2, kc[:, sl])], [vc[:, sl]], sink_col(h)))
        o_ref[...] = _unstack_heads(outs, TQ_W, group).astype(BF16)


def _window(px, sink_l, nb, seq, n_ctx, need_ctx):
    ta = px.shape[0]
    assert n_ctx == TQ_W
    n_q = seq // TQ_W
    ctx_q = nb * n_q
    ctx_rows = (nb * seq) // n_ctx
    steps = n_q + (1 if need_ctx else 0)

    def q_idx(col):
        return lambda b, i: (jnp.where(i < n_q, b * n_q + i, ctx_q + b), col)

    return pl.pallas_call(
        functools.partial(_win_kernel, n_q=n_q, seq=seq),
        grid=(nb, steps),
        in_specs=[pl.BlockSpec((TQ_W, D_ATT), q_idx(WQ // D_ATT)),
                  pl.BlockSpec((seq, 128), lambda b, i: (b, WK // 128)),
                  pl.BlockSpec((seq, 128), lambda b, i: (b, WV // 128)),
                  pl.BlockSpec((n_ctx, 128), lambda b, i: (ctx_rows + b, WK // 128)),
                  pl.BlockSpec((n_ctx, 128), lambda b, i: (ctx_rows + b, WV // 128)),
                  pl.BlockSpec((1, D_ATT), lambda b, i: (0, 0))],
        out_specs=pl.BlockSpec((TQ_W, D_ATT), q_idx(0)),
        out_shape=jax.ShapeDtypeStruct((ta, D_ATT), BF16),
        compiler_params=_params(("parallel", "arbitrary")),
        name="window_attn",
    )(px, px, px, px, px, sink_l)


def _glb_kernel(q_ref, k_ref, v_ref, kc_ref, vc_ref, o_ref, *, n_q):
    i = pl.program_id(1)
    group = GLB_Q_HEADS // GLB_KV_HEADS
    q = q_ref[...]
    kc = kc_ref[...]
    vc = vc_ref[...]

    @pl.when(i < n_q)
    def _():
        outs = []
        for h in range(GLB_KV_HEADS):
            sl = slice(h * HEAD_DIM, (h + 1) * HEAD_DIM)
            q2 = _stack_group(q, h, group)
            s = _dot_nt(q2, k_ref[:, sl])
            sc = _dot_nt(q2, kc[:, sl])
            outs.append(_softmax_av([s, sc], [v_ref[:, sl], vc[:, sl]], None))
        o_ref[...] = _unstack_heads(outs, TQ_G, group).astype(BF16)

    @pl.when(i >= n_q)
    def _():
        outs = []
        for h in range(GLB_KV_HEADS):
            sl = slice(h * HEAD_DIM, (h + 1) * HEAD_DIM)
            q2 = _stack_group(q, h, group)
            outs.append(_softmax_av([_dot_nt(q2, kc[:, sl])], [vc[:, sl]], None))
        o_ref[...] = _unstack_heads(outs, TQ_G, group).astype(BF16)


def _global(px, nb, seq, n_ctx, need_ctx):
    ta = px.shape[0]
    n_q = seq // TQ_G
    n_cq = n_ctx // TQ_G
    ctx_q = nb * n_q
    ctx_rows = (nb * seq) // n_ctx
    steps = n_q + (n_cq if need_ctx else 0)

    def q_idx(col):
        return lambda b, i: (jnp.where(i < n_q, b * n_q + i, ctx_q + b * n_cq + (i - n_q)), col)

    return pl.pallas_call(
        functools.partial(_glb_kernel, n_q=n_q),
        grid=(nb, steps),
        in_specs=[pl.BlockSpec((TQ_G, D_ATT), q_idx(GQ // D_ATT)),
                  pl.BlockSpec((seq, 128), lambda b, i: (b, GK // 128)),
                  pl.BlockSpec((seq, 128), lambda b, i: (b, GV // 128)),
                  pl.BlockSpec((n_ctx, 128), lambda b, i: (ctx_rows + b, GK // 128)),
                  pl.BlockSpec((n_ctx, 128), lambda b, i: (ctx_rows + b, GV // 128))],
        out_specs=pl.BlockSpec((TQ_G, D_ATT), q_idx(0)),
        out_shape=jax.ShapeDtypeStruct((ta, D_ATT), BF16),
        compiler_params=_params(("parallel", "arbitrary")),
        name="global_attn",
    )(px, px, px, px, px)


def _outproj_kernel(of_ref, ob_ref, g_ref, yw_ref, yg_ref, x_ref, mod_ref, n2_ref, w_ref, wr_ref,
                    xo_ref, h_ref, lg_ref):
    o = of_ref[...].astype(F32) + ob_ref[...].astype(F32)
    normed = []
    for h in range(RET_HEADS):
        oh = o[:, h * RET_DV:(h + 1) * RET_DV]
        mu = jnp.mean(oh, axis=-1, keepdims=True)
        var = jnp.mean(jnp.square(oh - mu), axis=-1, keepdims=True)
        normed.append((oh - mu) * lax.rsqrt(var + EPS))
    yr = _silu(g_ref[...].astype(F32)) * jnp.concatenate(normed, axis=-1)
    acc = _dot(yr.astype(BF16), w_ref[0:D_RET, :])
    acc += _dot(yw_ref[...], w_ref[D_RET:D_RET + D_ATT, :])
    acc += _dot(yg_ref[...], w_ref[D_RET + D_ATT:D_RET + 2 * D_ATT, :])
    x = x_ref[...] + mod_ref[0, 2:3, :] * acc
    xo_ref[...] = x
    ms = jnp.mean(x * x, axis=-1, keepdims=True)
    h2 = x * lax.rsqrt(ms + EPS) * n2_ref[...]
    h2 = (h2 * (1.0 + mod_ref[0, 4:5, :]) + mod_ref[0, 3:4, :]).astype(BF16)
    h_ref[...] = h2
    lg_ref[...] = _dot_nt(wr_ref[...], h2)


def _outproj(o_f, o_b, px, yw, yg, xa, mod_l, n2, w_out, wr_t, nb, n_x_tiles, tiles_per_seq, n_tiles):
    ta, d = xa.shape

    def mod_idx(i):
        return (jnp.where(i < n_x_tiles, i // tiles_per_seq, nb), 0, 0)

    row = lambda i: (i, 0)
    return pl.pallas_call(
        _outproj_kernel,
        grid=(n_tiles,),
        in_specs=[pl.BlockSpec((TM, D_RET), row),
                  pl.BlockSpec((TM, D_RET), row),
                  pl.BlockSpec((TM, D_RET), lambda i: (i, RG // D_RET)),
                  pl.BlockSpec((TM, D_ATT), row),
                  pl.BlockSpec((TM, D_ATT), row),
                  pl.BlockSpec((TM, d), row),
                  pl.BlockSpec((1, 6, d), mod_idx),
                  pl.BlockSpec((1, d), lambda i: (0, 0)),
                  pl.BlockSpec((d, d), lambda i: (0, 0)),
                  pl.BlockSpec((N_EXPERTS, d), lambda i: (0, 0))],
        out_specs=[pl.BlockSpec((TM, d), row),
                   pl.BlockSpec((TM, d), row),
                   pl.BlockSpec((N_EXPERTS, TM), lambda i: (0, i))],
        out_shape=[jax.ShapeDtypeStruct((ta, d), F32),
                   jax.ShapeDtypeStruct((ta, d), BF16),
                   jax.ShapeDtypeStruct((N_EXPERTS, ta), F32)],
        input_output_aliases={5: 0},
        compiler_params=_params(("parallel",)),
        name="out_proj",
    )(o_f, o_b, px, yw, yg, xa, mod_l, n2, w_out, wr_t)


def _moe_kernel(be_ref, nu_ref, x_ref, wgu_ref, wd_ref, o_ref):
    @pl.when(pl.program_id(0) < nu_ref[0])
    def _():
        f = wd_ref.shape[1]
        au = _dot(x_ref[...], wgu_ref[0])
        mid = (_silu(au[:, :f]) * au[:, f:]).astype(BF16)
        o_ref[...] = _dot(mid, wd_ref[0]).astype(BF16)


def _moe_ffn(blk_e, n_used, xs, w_gu, w_down):
    rows, d = xs.shape
    f2 = w_gu.shape[2]
    f = w_down.shape[1]
    grid_spec = pltpu.PrefetchScalarGridSpec(
        num_scalar_prefetch=2,
        grid=(rows // TM_E,),
        in_specs=[pl.BlockSpec((TM_E, d), lambda i, be, nu: (i, 0)),
                  pl.BlockSpec((1, d, f2), lambda i, be, nu: (be[i], 0, 0)),
                  pl.BlockSpec((1, f, d), lambda i, be, nu: (be[i], 0, 0))],
        out_specs=pl.BlockSpec((TM_E, d), lambda i, be, nu: (i, 0)),
    )
    return pl.pallas_call(
        _moe_kernel,
        grid_spec=grid_spec,
        out_shape=jax.ShapeDtypeStruct((rows, d), BF16),
        compiler_params=_params(("arbitrary",)),
        name="moe_ffn",
    )(blk_e, n_used, xs, w_gu, w_down)


def _route(logits_t, b_router):
    s = jax.nn.sigmoid(logits_t.T)
    grouped = (s + b_router.astype(F32)).reshape(-1, N_GROUPS, EXPERTS_PER_GROUP)
    gscore = lax.top_k(grouped, TOP_K)[0].sum(-1)
    grp = jnp.argmax(gscore, axis=-1)
    cand = jnp.take_along_axis(grouped, grp[:, None, None], axis=1)[:, 0]
    _, loc = lax.top_k(cand, TOP_K)
    e_idx = grp[:, None] * EXPERTS_PER_GROUP + loc
    w = jnp.take_along_axis(s, e_idx, axis=1)
    return e_idx.astype(jnp.int32), w / jnp.sum(w, axis=-1, keepdims=True)


def _moe(h2, logits_t, b_router, w_gu, w_down, n_tok):
    e_idx, gate = _route(logits_t[:, :n_tok], b_router)
    n_asg = n_tok * TOP_K
    flat_e = e_idx.reshape(-1)
    onehot = (flat_e[:, None] == jnp.arange(N_EXPERTS, dtype=jnp.int32)[None, :]).astype(jnp.int32)
    csum = jnp.cumsum(onehot, axis=0)
    counts = csum[-1]
    rank = jnp.take_along_axis(csum, flat_e[:, None], axis=1)[:, 0] - 1
    padded = (counts + TM_E - 1) // TM_E * TM_E
    pad_end = jnp.cumsum(padded)
    pad_start = pad_end - padded
    cnt_start = jnp.cumsum(counts) - counts
    pos = (pad_start[flat_e] + rank).reshape(n_tok, TOP_K)
    n_blocks = (n_asg + N_EXPERTS * (TM_E - 1) + TM_E - 1) // TM_E
    blk_start = jnp.arange(n_blocks, dtype=jnp.int32) * TM_E
    blk_e = jnp.minimum(jnp.searchsorted(pad_end, blk_start, side='right'), N_EXPERTS - 1).astype(jnp.int32)
    n_used = (pad_end[-1] // TM_E).astype(jnp.int32).reshape(1)
    order = jnp.argsort(flat_e)
    p = jnp.arange(n_blocks * TM_E, dtype=jnp.int32)
    e_row = jnp.repeat(blk_e, TM_E)
    r = p - pad_start[e_row]
    src = order[jnp.clip(cnt_start[e_row] + jnp.minimum(r, counts[e_row] - 1), 0, n_asg - 1)] // TOP_K
    xs = h2[src]
    yp = _moe_ffn(blk_e, n_used, xs, w_gu, w_down)
    y = yp[pos[:, 0]].astype(F32) * gate[:, 0:1] + yp[pos[:, 1]].astype(F32) * gate[:, 1:2]
    return y


def _rope_tables(seq, dtype):
    rows = seq // GRID_W
    row = jnp.repeat(jnp.arange(rows), GRID_W).astype(jnp.float32)
    col = (jnp.arange(rows * GRID_W) % GRID_W).astype(jnp.float32)
    half = HEAD_DIM // 2
    inv = jnp.power(ROPE_BASE, -jnp.arange(0, half, 2, dtype=jnp.float32) / half)
    ang_r, ang_c = row[:, None] * inv, col[:, None] * inv
    cos_r, cos_c = jnp.cos(ang_r).astype(dtype), jnp.cos(ang_c).astype(dtype)
    sin_r, sin_c = jnp.sin(ang_r).astype(dtype), jnp.sin(ang_c).astype(dtype)
    cos_h = jnp.concatenate([cos_r, cos_r, cos_c, cos_c], -1)
    sin_h = jnp.concatenate([-sin_r, sin_r, -sin_c, sin_c], -1)
    cos_t = jnp.concatenate([jnp.tile(cos_h, (1, D_ATT // HEAD_DIM)), jnp.ones((TM, D_ATT), dtype)], 0)
    sin_t = jnp.concatenate([jnp.tile(sin_h, (1, D_ATT // HEAD_DIM)), jnp.zeros((TM, D_ATT), dtype)], 0)
    return cos_t, sin_t


def kernel(x, c, ctx, c_ctx, ada_w, ada_b, norm1, norm2, w_in, w_out, ret_decay, win_qk_gain, win_sink,
           glb_qk_gain, w_router, b_router, w_gate_up, w_down):
    nb, seq, d = x.shape
    n_ctx = ctx.shape[1]
    depth = ada_w.shape[0]
    n_x = nb * seq
    ta = n_x + nb * n_ctx
    assert seq % TM == 0 and (nb * n_ctx) % TM == 0 and seq % GRID_W == 0
    tiles_per_seq = seq // TM
    n_x_tiles = n_x // TM
    n_tiles = ta // TM

    xa = jnp.concatenate([x.reshape(n_x, d), ctx.reshape(nb * n_ctx, d)], 0)
    mod_rows = (nb + 1 + 7) // 8 * 8
    cc = jnp.zeros((mod_rows, d), F32).at[:nb].set(c).at[nb].set(c_ctx)
    mod = _ada(cc, ada_w, ada_b).reshape(depth, mod_rows, 6, d)

    cos_t, sin_t = _rope_tables(seq, F32)
    head_of = jnp.arange(D_ATT) // HEAD_DIM
    hm = jnp.where(head_of[:, None] == head_of[None, :], 1.0 / HEAD_DIM, 0.0).astype(BF16)
    wr_t = w_router.T.astype(BF16)

    for l in range(depth):
        last = l == depth - 1
        gains = jnp.zeros((8, D_ATT), F32)
        gains = gains.at[0].set(jnp.tile(win_qk_gain[l, 0], WIN_Q_HEADS))
        gains = gains.at[1, :WIN_KV_HEADS * HEAD_DIM].set(jnp.tile(win_qk_gain[l, 1], WIN_KV_HEADS))
        gains = gains.at[2].set(jnp.tile(glb_qk_gain[l, 0], GLB_Q_HEADS))
        gains = gains.at[3, :GLB_KV_HEADS * HEAD_DIM].set(jnp.tile(glb_qk_gain[l, 1], GLB_KV_HEADS))
        sink_l = jnp.repeat(win_sink[l].astype(F32), HEAD_DIM)[None, :]

        px = _inproj(xa, mod[l], norm1[l][None, :], w_in[l].astype(BF16), cos_t, sin_t, gains, hm,
                     nb, n_x_tiles, tiles_per_seq)
        gc, dmat, xi, zeta = _ret_tables(ret_decay[l])
        o_f, o_b = _retention(px, gc, dmat, xi, zeta, nb, seq, n_ctx)
        yw = _window(px, sink_l, nb, seq, n_ctx, not last)
        yg = _global(px, nb, seq, n_ctx, not last)
        tiles = n_x_tiles if last else n_tiles
        xa, h2, logits_t = _outproj(o_f, o_b, px, yw, yg, xa, mod[l], norm2[l][None, :],
                                    w_out[l].astype(BF16), wr_t, nb, n_x_tiles, tiles_per_seq, tiles)
        n_tok = n_x if last else ta
        y = _moe(h2, logits_t, b_router, w_gate_up[l].astype(BF16), w_down[l].astype(BF16), n_tok)
        g2 = mod[l, :nb, 5]
        xn = xa[:n_x].reshape(nb, seq, d) + g2[:, None, :] * y[:n_x].reshape(nb, seq, d)
        if last:
            return xn
        cn = xa[n_x:] + mod[l, nb, 5][None, :] * y[n_x:]
        xa = jnp.concatenate([xn.reshape(n_x, d), cn], 0)
```

```python
import functools

import jax
import jax.numpy as jnp
from jax import lax
from jax.experimental import pallas as pl
from jax.experimental.pallas import tpu as pltpu

GRID_W = 64
HEAD_DIM = 64
RET_HEADS = 4
RET_DK = 64
RET_DV = 128
WIN_Q_HEADS = 4
WIN_KV_HEADS = 2
WINDOW = 128
GLB_Q_HEADS = 4
GLB_KV_HEADS = 2
ROPE_BASE = 10000.0
N_EXPERTS = 16
N_GROUPS = 4
EXPERTS_PER_GROUP = N_EXPERTS // N_GROUPS
TOP_K = 2
EPS = 1e-6
NEG_INF = -1e30

RQ, RK, RV, RG = 0, 256, 512, 1024
WQ, WK, WV = 1536, 1792, 1920
GQ, GK, GV = 2048, 2304, 2432
D_IN = 2560
D_RET = RET_HEADS * RET_DV
D_ATT = WIN_Q_HEADS * HEAD_DIM

TM = 512
RET_C = 256
TQ_W = 256
TQ_G = 128
KC_G = 512
LOG2_E = 1.4426950408889634
assert GLB_KV_HEADS == 2
TM_E = 512
VMEM_LIMIT = 56 * 1024 * 1024

F32 = jnp.float32
BF16 = jnp.bfloat16


def _dot(a, b):
    return jnp.dot(a, b, preferred_element_type=F32)


def _dot_nt(a, b):
    return lax.dot_general(a, b, (((1,), (1,)), ((), ())), preferred_element_type=F32)


def _dot_tn(a, b):
    return lax.dot_general(a, b, (((0,), (0,)), ((), ())), preferred_element_type=F32)


def _silu(x):
    return x * jax.nn.sigmoid(x)


def _params(sem):
    return pltpu.CompilerParams(dimension_semantics=sem, vmem_limit_bytes=VMEM_LIMIT)


def _ada_kernel(c_ref, w_ref, b_ref, o_ref):
    s = _silu(c_ref[...]).astype(BF16)
    o_ref[0] = _dot(s, w_ref[0].astype(BF16)) + b_ref[0]


def _ada(cc, ada_w, ada_b):
    depth, d, n = ada_w.shape
    tn = 1536
    rows = cc.shape[0]
    return pl.pallas_call(
        _ada_kernel,
        grid=(depth, n // tn),
        in_specs=[pl.BlockSpec((rows, d), lambda l, j: (0, 0)),
                  pl.BlockSpec((1, d, tn), lambda l, j: (l, 0, j)),
                  pl.BlockSpec((1, 1, tn), lambda l, j: (l, 0, j))],
        out_specs=pl.BlockSpec((1, rows, tn), lambda l, j: (l, 0, j)),
        out_shape=jax.ShapeDtypeStruct((depth, rows, n), F32),
        compiler_params=_params(("arbitrary", "arbitrary")),
        name="ada_mod",
    )(cc, ada_w, ada_b.reshape(depth, 1, n))


def _inproj_kernel(x_ref, mod_ref, n1_ref, wf_ref, wvt_ref, cos_ref, sin_ref, gains_ref, hm_ref, o_ref, vt_ref,
                   w_ref):
    @pl.when(pl.program_id(0) == 0)
    def _():
        w_ref[...] = wf_ref[...].astype(BF16)

    x = x_ref[...]
    ms = jnp.mean(x * x, axis=-1, keepdims=True)
    h = x * lax.rsqrt(ms + EPS) * n1_ref[...]
    h = h * (1.0 + mod_ref[0, 1:2, :]) + mod_ref[0, 0:1, :]
    hb = h.astype(BF16)

    def proj(lo, width):
        return _dot(hb, w_ref[:, lo:lo + width])

    def qk(lo, width, gain_row, scale):
        y = proj(lo, width)
        sq = y * y
        sq_hi = sq.astype(BF16)
        sq_lo = (sq - sq_hi.astype(F32)).astype(BF16)
        hm = hm_ref[0:width, 0:width]
        msq = _dot(sq_hi, hm) + _dot(sq_lo, hm)
        yn = y * lax.rsqrt(msq + EPS) * gains_ref[gain_row:gain_row + 1, 0:width]
        nxt = pltpu.roll(yn, width - HEAD_DIM // 4, 1)
        prv = pltpu.roll(yn, HEAD_DIM // 4, 1)
        lane = lax.broadcasted_iota(jnp.int32, yn.shape, 1)
        partner = jnp.where((lane % (HEAD_DIM // 2)) < HEAD_DIM // 4, nxt, prv)
        yr = yn * cos_ref[:, 0:width] + partner * sin_ref[:, 0:width]
        return (yr * scale).astype(BF16)

    o_ref[:, RQ:RK] = proj(RQ, RK - RQ).astype(BF16)
    o_ref[:, RK:RV] = (proj(RK, RV - RK) * RET_DK ** -0.5).astype(BF16)
    o_ref[:, RV:RG] = proj(RV, RG - RV).astype(BF16)
    o_ref[:, RG:WQ] = proj(RG, WQ - RG).astype(BF16)
    o_ref[:, WQ:WK] = qk(WQ, WK - WQ, 0, HEAD_DIM ** -0.5)
    o_ref[:, WK:WV] = qk(WK, WV - WK, 1, 1.0)
    o_ref[:, WV:GQ] = proj(WV, GQ - WV).astype(BF16)
    o_ref[:, GQ:GK] = qk(GQ, GK - GQ, 2, HEAD_DIM ** -0.5 * LOG2_E)
    o_ref[:, GK:GV] = qk(GK, GV - GK, 3, 1.0)
    vt_ref[...] = _dot_nt(wvt_ref[...], hb).astype(BF16)


def _inproj(xa, mod_l, n1, w_in, layer, cos_t, sin_t, gains, hm, nb, n_x_tiles, tiles_per_seq):
    ta, d = xa.shape
    n_tiles = ta // TM
    w_gvt = w_in[layer, :, GV:].T.astype(BF16)
    n_gv = D_IN - GV
    once = pl.Buffered(1)

    def mod_idx(i):
        return (jnp.where(i < n_x_tiles, i // tiles_per_seq, nb), 0, 0)

    def rope_idx(i):
        return (jnp.where(i < n_x_tiles, i % tiles_per_seq, tiles_per_seq), 0)

    return pl.pallas_call(
        _inproj_kernel,
        grid=(n_tiles,),
        in_specs=[pl.BlockSpec((TM, d), lambda i: (i, 0)),
                  pl.BlockSpec((1, 6, d), mod_idx),
                  pl.BlockSpec((1, d), lambda i: (0, 0)),
                  pl.BlockSpec((None, d, GV), lambda i: (layer, 0, 0), pipeline_mode=once),
                  pl.BlockSpec((n_gv, d), lambda i: (0, 0), pipeline_mode=once),
                  pl.BlockSpec((TM, D_ATT), rope_idx),
                  pl.BlockSpec((TM, D_ATT), rope_idx),
                  pl.BlockSpec((8, D_ATT), lambda i: (0, 0)),
                  pl.BlockSpec((D_ATT, D_ATT), lambda i: (0, 0))],
        out_specs=[pl.BlockSpec((TM, GV), lambda i: (i, 0)),
                   pl.BlockSpec((n_gv, TM), lambda i: (0, i))],
        out_shape=[jax.ShapeDtypeStruct((ta, GV), BF16),
                   jax.ShapeDtypeStruct((n_gv, ta), BF16)],
        scratch_shapes=[pltpu.VMEM((d, GV), BF16)],
        compiler_params=_params(("arbitrary",)),
        name="in_proj",
    )(xa, mod_l, n1, w_in, w_gvt, cos_t, sin_t, gains, hm)


def _ret_kernel(gc_ref, qf_ref, kf_ref, vf_ref, qb_ref, kb_ref, vb_ref, dmat_ref, xi_ref, zeta_ref,
                of_ref, ob_ref, s_ref):
    @pl.when(pl.program_id(1) == 0)
    def _():
        s_ref[...] = jnp.zeros_like(s_ref)

    dirs = ((qf_ref, kf_ref, vf_ref, of_ref), (qb_ref, kb_ref, vb_ref, ob_ref))
    for d, (q_ref, k_ref, v_ref, o_ref) in enumerate(dirs):
        q = q_ref[...]
        k = k_ref[...]
        v = v_ref[...]
        kz = (k.astype(F32) * zeta_ref[d]).astype(BF16)
        outs = []
        for h in range(RET_HEADS):
            i = d * RET_HEADS + h
            qh = q[:, h * RET_DK:(h + 1) * RET_DK]
            kh = k[:, h * RET_DK:(h + 1) * RET_DK]
            vh = v[:, h * RET_DV:(h + 1) * RET_DV]
            att = _dot_nt(qh, kh) * dmat_ref[i]
            state = s_ref[i]
            outs.append(_dot(att.astype(BF16), vh) + _dot(qh, state.astype(BF16)) * xi_ref[i])
            s_ref[i] = gc_ref[i] * state + _dot_tn(kz[:, h * RET_DK:(h + 1) * RET_DK], vh)
        o_ref[...] = jnp.concatenate(outs, axis=-1).astype(BF16)


def _retention(px, gc, dmat, xi, zeta, nb, seq, n_ctx):
    ta = px.shape[0]
    n_x = seq // RET_C
    n_c = n_ctx // RET_C
    ctx_base = nb * n_x
    steps = n_c + n_x

    def row_f(b, c):
        return jnp.where(c < n_c, ctx_base + b * n_c + c, b * n_x + (c - n_c))

    def row_b(b, c):
        return jnp.where(c < n_c, ctx_base + b * n_c + (n_c - 1 - c), b * n_x + (steps - 1 - c))

    def spec(width, col, row):
        return pl.BlockSpec((RET_C, width), lambda b, c: (row(b, c), col))

    const3 = lambda b, c: (0, 0, 0)
    return pl.pallas_call(
        _ret_kernel,
        grid=(nb, steps),
        in_specs=[pl.BlockSpec(memory_space=pltpu.SMEM),
                  spec(256, RQ // 256, row_f), spec(256, RK // 256, row_f), spec(D_RET, RV // D_RET, row_f),
                  spec(256, RQ // 256, row_b), spec(256, RK // 256, row_b), spec(D_RET, RV // D_RET, row_b),
                  pl.BlockSpec((2 * RET_HEADS, RET_C, RET_C), const3),
                  pl.BlockSpec((2 * RET_HEADS, RET_C, RET_DV), const3),
                  pl.BlockSpec((2, RET_C, RET_HEADS * RET_DK), const3)],
        out_specs=[spec(D_RET, 0, row_f), spec(D_RET, 0, row_b)],
        out_shape=[jax.ShapeDtypeStruct((ta, D_RET), BF16)] * 2,
        scratch_shapes=[pltpu.VMEM((2 * RET_HEADS, RET_DK, RET_DV), F32)],
        compiler_params=_params(("parallel", "arbitrary")),
        name="retention",
    )(gc, px, px, px, px, px, px, dmat, xi, zeta)


def _ret_tables(decay_logit):
    log_g = jax.nn.log_sigmoid(decay_logit.astype(F32)).reshape(2 * RET_HEADS)
    idx = jnp.arange(RET_C, dtype=F32)
    diff = idx[:, None] - idx[None, :]
    lg = log_g[:, None, None]
    d_fwd = jnp.where(diff >= 0, jnp.exp(jnp.maximum(diff, 0.0) * lg), 0.0)
    d_bwd = jnp.where(diff <= 0, jnp.exp(jnp.maximum(-diff, 0.0) * lg), 0.0)
    is_bwd = (jnp.arange(2 * RET_HEADS) >= RET_HEADS)[:, None, None]
    dmat = jnp.where(is_bwd, d_bwd, d_fwd)
    pos = jnp.where(is_bwd[:, :, 0], RET_C - 1.0 - idx[None, :], idx[None, :])
    xi = jnp.exp((pos + 1.0) * log_g[:, None])
    zeta = jnp.exp((RET_C - 1.0 - pos) * log_g[:, None])
    gc = jnp.exp(RET_C * log_g)
    xi = jnp.broadcast_to(xi[:, :, None], (2 * RET_HEADS, RET_C, RET_DV))
    zeta = jnp.repeat(zeta.reshape(2, RET_HEADS, RET_C).transpose(0, 2, 1), RET_DK, axis=-1)
    return gc, dmat, xi, zeta


def _softmax_av(s_list, v_list, sink):
    m = s_list[0].max(axis=-1, keepdims=True)
    for s in s_list[1:]:
        m = jnp.maximum(m, s.max(axis=-1, keepdims=True))
    if sink is not None:
        m = jnp.maximum(m, sink)
    den = None
    acc = None
    for s, v in zip(s_list, v_list):
        p = jnp.exp(s - m)
        d = p.sum(axis=-1, keepdims=True)
        a = _dot(p.astype(BF16), v)
        den = d if den is None else den + d
        acc = a if acc is None else acc + a
    if sink is not None:
        den = den + jnp.exp(sink - m)
    return acc / den


def _stack_group(q, h, group):
    return jnp.concatenate([q[:, (h * group + g) * HEAD_DIM:(h * group + g + 1) * HEAD_DIM]
                            for g in range(group)], axis=0)


def _unstack_heads(outs, rows, group):
    cols = []
    for o in outs:
        for g in range(group):
            cols.append(o[g * rows:(g + 1) * rows])
    return jnp.concatenate(cols, axis=-1)


def _win_kernel(q_ref, k_ref, v_ref, kc_ref, vc_ref, sink_ref, o_ref, *, n_q, seq):
    i = pl.program_id(1)
    group = WIN_Q_HEADS // WIN_KV_HEADS
    span = TQ_W + 2 * WINDOW
    q = q_ref[...]
    kc = kc_ref[...]
    vc = vc_ref[...]
    row = lax.broadcasted_iota(jnp.int32, (group * TQ_W, 1), 0)

    def sink_col(h):
        s = sink_ref[0:1, h * group * HEAD_DIM:h * group * HEAD_DIM + 1]
        for g in range(1, group):
            sg = sink_ref[0:1, (h * group + g) * HEAD_DIM:(h * group + g) * HEAD_DIM + 1]
            s = jnp.where(row >= g * TQ_W, sg, s)
        return s

    @pl.when(i < n_q)
    def _():
        start = i * TQ_W
        lo = pl.multiple_of(jnp.clip(start - WINDOW, 0, seq - span), WINDOW)
        kw = k_ref[pl.ds(lo, span), :]
        vw = v_ref[pl.ds(lo, span), :]
        qpos = start + lax.broadcasted_iota(jnp.int32, (group * TQ_W, span), 0) % TQ_W
        kpos = lo + lax.broadcasted_iota(jnp.int32, (group * TQ_W, span), 1)
        band = jnp.abs(kpos - qpos) <= WINDOW
        outs = []
        for h in range(WIN_KV_HEADS):
            sl = slice(h * HEAD_DIM, (h + 1) * HEAD_DIM)
            q2 = _stack_group(q, h, group)
            s = jnp.where(band, _dot_nt(q2, kw[:, sl]), NEG_INF)
            sc = _dot_nt(q2, kc[:, sl])
            outs.append(_softmax_av([s, sc], [vw[:, sl], vc[:, sl]], sink_col(h)))
        o_ref[...] = _unstack_heads(outs, TQ_W, group).astype(BF16)

    @pl.when(i >= n_q)
    def _():
        outs = []
        for h in range(WIN_KV_HEADS):
            sl = slice(h * HEAD_DIM, (h + 1) * HEAD_DIM)
            q2 = _stack_group(q, h, group)
            outs.append(_softmax_av([_dot_nt(q2, kc[:, sl])], [vc[:, sl]], sink_col(h)))
        o_ref[...] = _unstack_heads(outs, TQ_W, group).astype(BF16)


def _window(px, sink_l, nb, seq, n_ctx, need_ctx):
    ta = px.shape[0]
    assert n_ctx == TQ_W
    n_q = seq // TQ_W
    ctx_q = nb * n_q
    ctx_rows = (nb * seq) // n_ctx
    steps = n_q + (1 if need_ctx else 0)

    def q_idx(col):
        return lambda b, i: (jnp.where(i < n_q, b * n_q + i, ctx_q + b), col)

    return pl.pallas_call(
        functools.partial(_win_kernel, n_q=n_q, seq=seq),
        grid=(nb, steps),
        in_specs=[pl.BlockSpec((TQ_W, D_ATT), q_idx(WQ // D_ATT)),
                  pl.BlockSpec((seq, 128), lambda b, i: (b, WK // 128)),
                  pl.BlockSpec((seq, 128), lambda b, i: (b, WV // 128)),
                  pl.BlockSpec((n_ctx, 128), lambda b, i: (ctx_rows + b, WK // 128)),
                  pl.BlockSpec((n_ctx, 128), lambda b, i: (ctx_rows + b, WV // 128)),
                  pl.BlockSpec((1, D_ATT), lambda b, i: (0, 0))],
        out_specs=pl.BlockSpec((TQ_W, D_ATT), q_idx(0)),
        out_shape=jax.ShapeDtypeStruct((ta, D_ATT), BF16),
        compiler_params=_params(("parallel", "arbitrary")),
        name="window_attn",
    )(px, px, px, px, px, sink_l)


def _glb_kernel(q_ref, k_ref, kc_ref, vt_ref, vtc_ref, o_ref, k_all, v_ext, *, n_q, seq, n_ctx):
    i = pl.program_id(1)
    group = GLB_Q_HEADS // GLB_KV_HEADS
    tq = q_ref.shape[0]
    w = group * tq

    @pl.when(i == 0)
    def _():
        k_all[0:seq, :] = k_ref[...]
        k_all[seq:seq + n_ctx, :] = kc_ref[...]
        for h in range(GLB_KV_HEADS):
            v_ext[h, 0:HEAD_DIM, 0:seq] = vt_ref[h * HEAD_DIM:(h + 1) * HEAD_DIM, :]
            v_ext[h, 0:HEAD_DIM, seq:seq + n_ctx] = vtc_ref[h * HEAD_DIM:(h + 1) * HEAD_DIM, :]
            v_ext[h, HEAD_DIM:2 * HEAD_DIM, :] = jnp.ones((HEAD_DIM, seq + n_ctx), BF16)

    def attend(chunks):
        q = q_ref[...]
        zeros = jnp.zeros((tq, HEAD_DIM), BF16)
        rows = []
        for h in range(GLB_KV_HEADS):
            for g in range(group):
                qh = q[:, (h * group + g) * HEAD_DIM:(h * group + g + 1) * HEAD_DIM]
                rows.append(jnp.concatenate([qh, zeros] if h == 0 else [zeros, qh], axis=1))
        q4 = jnp.concatenate(rows, axis=0)

        def scores(chunk):
            lo, nk = chunk
            return _dot_nt(k_all[lo:lo + nk, :], q4)

        m = None
        acc = [None] * GLB_KV_HEADS
        st_next = scores(chunks[0])
        for ci, (lo, nk) in enumerate(chunks):
            st = st_next
            if ci + 1 < len(chunks):
                st_next = scores(chunks[ci + 1])
            cm = jnp.max(st, axis=0, keepdims=True)
            m_new = cm if m is None else jnp.maximum(m, cm)
            pt = jnp.exp2(st - m_new).astype(BF16)
            if m is not None:
                alpha = jnp.exp2(m - m_new)
            for h in range(GLB_KV_HEADS):
                pv = _dot(v_ext[h, :, lo:lo + nk], pt[:, h * w:(h + 1) * w])
                acc[h] = pv if m is None else acc[h] * alpha[:, h * w:(h + 1) * w] + pv
            m = m_new
        for h in range(GLB_KV_HEADS):
            o = acc[h][0:HEAD_DIM] / acc[h][HEAD_DIM:HEAD_DIM + 1]
            for g in range(group):
                r0 = (h * group + g) * HEAD_DIM
                o_ref[r0:r0 + HEAD_DIM, :] = o[:, g * tq:(g + 1) * tq].astype(BF16)

    @pl.when(i < n_q)
    def _():
        attend([(c * KC_G, KC_G) for c in range(seq // KC_G)] + [(seq, n_ctx)])

    @pl.when(i >= n_q)
    def _():
        attend([(seq, n_ctx)])


def _global(px, gvt, nb, seq, n_ctx, need_ctx):
    ta = px.shape[0]
    assert seq % KC_G == 0
    n_q = seq // TQ_G
    n_cq = n_ctx // TQ_G
    ctx_q = nb * n_q
    ctx_rows = (nb * seq) // n_ctx
    steps = n_q + (n_cq if need_ctx else 0)
    n_kv = GLB_KV_HEADS * HEAD_DIM

    def q_row(b, i):
        return jnp.where(i < n_q, b * n_q + i, ctx_q + b * n_cq + (i - n_q))

    return pl.pallas_call(
        functools.partial(_glb_kernel, n_q=n_q, seq=seq, n_ctx=n_ctx),
        grid=(nb, steps),
        in_specs=[pl.BlockSpec((TQ_G, D_ATT), lambda b, i: (q_row(b, i), GQ // D_ATT)),
                  pl.BlockSpec((seq, n_kv), lambda b, i: (b, GK // n_kv)),
                  pl.BlockSpec((n_ctx, n_kv), lambda b, i: (ctx_rows + b, GK // n_kv)),
                  pl.BlockSpec((n_kv, seq), lambda b, i: (0, b)),
                  pl.BlockSpec((n_kv, n_ctx), lambda b, i: (0, ctx_rows + b))],
        out_specs=pl.BlockSpec((D_ATT, TQ_G), lambda b, i: (0, q_row(b, i))),
        out_shape=jax.ShapeDtypeStruct((D_ATT, ta), BF16),
        scratch_shapes=[pltpu.VMEM((seq + n_ctx, n_kv), BF16),
                        pltpu.VMEM((GLB_KV_HEADS, 2 * HEAD_DIM, seq + n_ctx), BF16)],
        compiler_params=_params(("parallel", "arbitrary")),
        name="global_attn",
    )(px, px, px, gvt, gvt)


def _route_rows(logits, bias):
    sig = jax.nn.sigmoid(logits)
    biased = sig + bias
    b_rows = [biased[e:e + 1, :] for e in range(N_EXPERTS)]
    s_rows = [sig[e:e + 1, :] for e in range(N_EXPERTS)]
    n_loc = EXPERTS_PER_GROUP

    best_score, grp = None, None
    for g in range(N_GROUPS):
        a = b_rows[g * n_loc:(g + 1) * n_loc]
        top2 = None
        for i in range(n_loc):
            for j in range(i + 1, n_loc):
                pair = a[i] + a[j]
                top2 = pair if top2 is None else jnp.maximum(top2, pair)
        if g == 0:
            best_score, grp = top2, jnp.zeros(top2.shape, jnp.int32)
        else:
            upd = top2 > best_score
            grp = jnp.where(upd, g, grp)
            best_score = jnp.where(upd, top2, best_score)

    def pick(rows, i):
        out = rows[i]
        for g in range(1, N_GROUPS):
            out = jnp.where(grp == g, rows[g * n_loc + i], out)
        return out

    cand = [pick(b_rows, i) for i in range(n_loc)]
    cand_s = [pick(s_rows, i) for i in range(n_loc)]
    m1, l1, w1 = cand[0], jnp.zeros(grp.shape, jnp.int32), cand_s[0]
    for i in range(1, n_loc):
        upd = cand[i] > m1
        m1 = jnp.where(upd, cand[i], m1)
        l1 = jnp.where(upd, i, l1)
        w1 = jnp.where(upd, cand_s[i], w1)
    m2, l2, w2 = None, None, None
    for i in range(n_loc):
        rest = jnp.where(l1 == i, -jnp.inf, cand[i])
        if i == 0:
            m2, l2, w2 = rest, jnp.zeros(grp.shape, jnp.int32), cand_s[0]
        else:
            upd = rest > m2
            m2 = jnp.where(upd, rest, m2)
            l2 = jnp.where(upd, i, l2)
            w2 = jnp.where(upd, cand_s[i], w2)
    tot = w1 + w2
    e_idx = jnp.concatenate([grp * n_loc + l1, grp * n_loc + l2], axis=0)
    gate = jnp.concatenate([w1 / tot, w2 / tot], axis=0)
    return e_idx, gate


def _outproj_kernel(of_ref, ob_ref, g_ref, yw_ref, ygt_ref, x_ref, mod_ref, n2_ref, wf_ref, wr_ref, br_ref,
                    xo_ref, h_ref, e_ref, gate_ref, w_ref):
    @pl.when(pl.program_id(0) == 0)
    def _():
        w_ref[...] = wf_ref[...].astype(BF16)

    o = of_ref[...].astype(F32) + ob_ref[...].astype(F32)
    normed = []
    for h in range(RET_HEADS):
        oh = o[:, h * RET_DV:(h + 1) * RET_DV]
        mu = jnp.mean(oh, axis=-1, keepdims=True)
        var = jnp.mean(jnp.square(oh - mu), axis=-1, keepdims=True)
        normed.append((oh - mu) * lax.rsqrt(var + EPS))
    yr = _silu(g_ref[...].astype(F32)) * jnp.concatenate(normed, axis=-1)
    acc = _dot(yr.astype(BF16), w_ref[0:D_RET, :])
    acc += _dot(yw_ref[...], w_ref[D_RET:D_RET + D_ATT, :])
    acc += _dot_tn(ygt_ref[...], w_ref[D_RET + D_ATT:D_RET + 2 * D_ATT, :])
    x = x_ref[...] + mod_ref[0, 2:3, :] * acc
    xo_ref[...] = x
    ms = jnp.mean(x * x, axis=-1, keepdims=True)
    h2 = x * lax.rsqrt(ms + EPS) * n2_ref[...]
    h2 = (h2 * (1.0 + mod_ref[0, 4:5, :]) + mod_ref[0, 3:4, :]).astype(BF16)
    h_ref[...] = h2
    e_idx, gate = _route_rows(_dot_nt(wr_ref[...], h2), br_ref[...])
    e_ref[...] = e_idx
    gate_ref[...] = gate


def _outproj(o_f, o_b, px, yw, yg, xa, mod_l, n2, w_out, layer, wr_t, b_r, nb, n_x_tiles, tiles_per_seq, n_tiles):
    ta, d = xa.shape
    once = pl.Buffered(1)

    def mod_idx(i):
        return (jnp.where(i < n_x_tiles, i // tiles_per_seq, nb), 0, 0)

    row = lambda i: (i, 0)
    return pl.pallas_call(
        _outproj_kernel,
        grid=(n_tiles,),
        in_specs=[pl.BlockSpec((TM, D_RET), row),
                  pl.BlockSpec((TM, D_RET), row),
                  pl.BlockSpec((TM, D_RET), lambda i: (i, RG // D_RET)),
                  pl.BlockSpec((TM, D_ATT), row),
                  pl.BlockSpec((D_ATT, TM), lambda i: (0, i)),
                  pl.BlockSpec((TM, d), row),
                  pl.BlockSpec((1, 6, d), mod_idx),
                  pl.BlockSpec((1, d), lambda i: (0, 0)),
                  pl.BlockSpec((None, d, d), lambda i: (layer, 0, 0), pipeline_mode=once),
                  pl.BlockSpec((N_EXPERTS, d), lambda i: (0, 0)),
                  pl.BlockSpec((N_EXPERTS, 1), lambda i: (0, 0))],
        out_specs=[pl.BlockSpec((TM, d), row),
                   pl.BlockSpec((TM, d), row),
                   pl.BlockSpec((TOP_K, TM), lambda i: (0, i)),
                   pl.BlockSpec((TOP_K, TM), lambda i: (0, i))],
        out_shape=[jax.ShapeDtypeStruct((ta, d), F32),
                   jax.ShapeDtypeStruct((ta, d), BF16),
                   jax.ShapeDtypeStruct((TOP_K, ta), jnp.int32),
                   jax.ShapeDtypeStruct((TOP_K, ta), F32)],
        scratch_shapes=[pltpu.VMEM((d, d), BF16)],
        input_output_aliases={5: 0},
        compiler_params=_params(("arbitrary",)),
        name="out_proj",
    )(o_f, o_b, px, yw, yg, xa, mod_l, n2, w_out, wr_t, b_r)


def _moe_kernel(be_ref, first_ref, nu_ref, x_ref, wgu_f_ref, wd_f_ref, o_ref, wgu_ref, wd_ref):
    i = pl.program_id(0)

    @pl.when(jnp.logical_and(i < nu_ref[0], first_ref[i] == 1))
    def _():
        wgu_ref[...] = wgu_f_ref[...].astype(BF16)
        wd_ref[...] = wd_f_ref[...].astype(BF16)

    @pl.when(i < nu_ref[0])
    def _():
        f = wd_ref.shape[0]
        au = _dot(x_ref[...], wgu_ref[...])
        mid = (_silu(au[:, :f]) * au[:, f:]).astype(BF16)
        o_ref[...] = _dot(mid, wd_ref[...]).astype(BF16)


def _moe_ffn(blk_e, first, n_used, xs, w_gu, w_down, layer):
    rows, d = xs.shape
    f2 = w_gu.shape[3]
    f = w_down.shape[2]
    grid_spec = pltpu.PrefetchScalarGridSpec(
        num_scalar_prefetch=3,
        grid=(rows // TM_E,),
        in_specs=[pl.BlockSpec((TM_E, d), lambda i, be, fi, nu: (i, 0)),
                  pl.BlockSpec((None, None, d, f2), lambda i, be, fi, nu: (layer, be[i], 0, 0)),
                  pl.BlockSpec((None, None, f, d), lambda i, be, fi, nu: (layer, be[i], 0, 0))],
        out_specs=pl.BlockSpec((TM_E, d), lambda i, be, fi, nu: (i, 0)),
        scratch_shapes=[pltpu.VMEM((d, f2), BF16), pltpu.VMEM((f, d), BF16)],
    )
    return pl.pallas_call(
        _moe_kernel,
        grid_spec=grid_spec,
        out_shape=jax.ShapeDtypeStruct((rows, d), BF16),
        compiler_params=_params(("arbitrary",)),
        name="moe_ffn",
    )(blk_e, first, n_used, xs, w_gu, w_down)


def _moe(h2, e_idx_t, w_gu, w_down, layer, n_tok):
    n_asg = n_tok * TOP_K
    flat_e = e_idx_t[:, :n_tok].reshape(-1)
    onehot = (flat_e[:, None] == jnp.arange(N_EXPERTS, dtype=jnp.int32)[None, :]).astype(jnp.int32)
    csum = jnp.cumsum(onehot, axis=0)
    counts = csum[-1]
    rank = jnp.sum(csum * onehot, axis=1) - 1
    padded = (counts + TM_E - 1) // TM_E * TM_E
    pad_end = jnp.cumsum(padded)
    pad_start = pad_end - padded
    cnt_start = jnp.cumsum(counts) - counts
    pos = (jnp.sum(pad_start[None, :] * onehot, axis=1) + rank).reshape(TOP_K, n_tok)
    n_blocks = (n_asg + N_EXPERTS * (TM_E - 1) + TM_E - 1) // TM_E
    blk_start = jnp.arange(n_blocks, dtype=jnp.int32) * TM_E
    blk_e = jnp.minimum(jnp.sum(blk_start[:, None] >= pad_end[None, :], axis=1), N_EXPERTS - 1).astype(jnp.int32)
    first = jnp.concatenate([jnp.ones((1,), jnp.int32), (blk_e[1:] != blk_e[:-1]).astype(jnp.int32)])
    n_used = (pad_end[-1] // TM_E).astype(jnp.int32).reshape(1)
    order = jnp.argsort(flat_e)
    e_row = jnp.repeat(blk_e, TM_E)
    r = jnp.arange(n_blocks * TM_E, dtype=jnp.int32) - pad_start[e_row]
    src = order[jnp.clip(cnt_start[e_row] + jnp.minimum(r, counts[e_row] - 1), 0, n_asg - 1)] % n_tok
    yp = _moe_ffn(blk_e, first, n_used, h2[src], w_gu, w_down, layer)
    return yp[pos[0]], yp[pos[1]]


def _combine_kernel(x_ref, y0_ref, y1_ref, gate_ref, mod_ref, o_ref):
    g = gate_ref[...]
    y = y0_ref[...].astype(F32) * g[:, 0:1] + y1_ref[...].astype(F32) * g[:, 1:2]
    o_ref[...] = x_ref[...] + mod_ref[0, 5:6, :] * y


def _combine(xa, y0, y1, gate, mod_l, nb, n_x_tiles, tiles_per_seq, n_tiles, in_place):
    d = xa.shape[1]

    def mod_idx(i):
        return (jnp.where(i < n_x_tiles, i // tiles_per_seq, nb), 0, 0)

    row = lambda i: (i, 0)
    return pl.pallas_call(
        _combine_kernel,
        grid=(n_tiles,),
        in_specs=[pl.BlockSpec((TM, d), row),
                  pl.BlockSpec((TM, d), row),
                  pl.BlockSpec((TM, d), row),
                  pl.BlockSpec((TM, TOP_K), row),
                  pl.BlockSpec((1, 6, d), mod_idx)],
        out_specs=pl.BlockSpec((TM, d), row),
        out_shape=jax.ShapeDtypeStruct((xa.shape[0] if in_place else n_tiles * TM, d), F32),
        input_output_aliases={0: 0} if in_place else {},
        compiler_params=_params(("parallel",)),
        name="moe_combine",
    )(xa, y0, y1, gate, mod_l)


def _rope_tables(seq, dtype):
    rows = seq // GRID_W
    row = jnp.repeat(jnp.arange(rows), GRID_W).astype(jnp.float32)
    col = (jnp.arange(rows * GRID_W) % GRID_W).astype(jnp.float32)
    half = HEAD_DIM // 2
    inv = jnp.power(ROPE_BASE, -jnp.arange(0, half, 2, dtype=jnp.float32) / half)
    ang_r, ang_c = row[:, None] * inv, col[:, None] * inv
    cos_r, cos_c = jnp.cos(ang_r).astype(dtype), jnp.cos(ang_c).astype(dtype)
    sin_r, sin_c = jnp.sin(ang_r).astype(dtype), jnp.sin(ang_c).astype(dtype)
    cos_h = jnp.concatenate([cos_r, cos_r, cos_c, cos_c], -1)
    sin_h = jnp.concatenate([-sin_r, sin_r, -sin_c, sin_c], -1)
    cos_t = jnp.concatenate([jnp.tile(cos_h, (1, D_ATT // HEAD_DIM)), jnp.ones((TM, D_ATT), dtype)], 0)
    sin_t = jnp.concatenate([jnp.tile(sin_h, (1, D_ATT // HEAD_DIM)), jnp.zeros((TM, D_ATT), dtype)], 0)
    return cos_t, sin_t


def kernel(x, c, ctx, c_ctx, ada_w, ada_b, norm1, norm2, w_in, w_out, ret_decay, win_qk_gain, win_sink,
           glb_qk_gain, w_router, b_router, w_gate_up, w_down):
    nb, seq, d = x.shape
    n_ctx = ctx.shape[1]
    depth = ada_w.shape[0]
    n_x = nb * seq
    ta = n_x + nb * n_ctx
    assert seq % TM == 0 and (nb * n_ctx) % TM == 0 and seq % GRID_W == 0
    tiles_per_seq = seq // TM
    n_x_tiles = n_x // TM
    n_tiles = ta // TM

    xa = jnp.concatenate([x.reshape(n_x, d), ctx.reshape(nb * n_ctx, d)], 0)
    mod_rows = (nb + 1 + 7) // 8 * 8
    cc = jnp.zeros((mod_rows, d), F32).at[:nb].set(c).at[nb].set(c_ctx)
    mod = _ada(cc, ada_w, ada_b).reshape(depth, mod_rows, 6, d)

    cos_t, sin_t = _rope_tables(seq, F32)
    head_of = jnp.arange(D_ATT) // HEAD_DIM
    hm = jnp.where(head_of[:, None] == head_of[None, :], 1.0 / HEAD_DIM, 0.0).astype(BF16)
    wr_t = w_router.T.astype(BF16)
    b_r = b_router.astype(F32)[:, None]

    for l in range(depth):
        last = l == depth - 1
        gains = jnp.zeros((8, D_ATT), F32)
        gains = gains.at[0].set(jnp.tile(win_qk_gain[l, 0], WIN_Q_HEADS))
        gains = gains.at[1, :WIN_KV_HEADS * HEAD_DIM].set(jnp.tile(win_qk_gain[l, 1], WIN_KV_HEADS))
        gains = gains.at[2].set(jnp.tile(glb_qk_gain[l, 0], GLB_Q_HEADS))
        gains = gains.at[3, :GLB_KV_HEADS * HEAD_DIM].set(jnp.tile(glb_qk_gain[l, 1], GLB_KV_HEADS))
        sink_l = jnp.repeat(win_sink[l].astype(F32), HEAD_DIM)[None, :]

        px, gvt = _inproj(xa, mod[l], norm1[l][None, :], w_in, l, cos_t, sin_t, gains, hm,
                          nb, n_x_tiles, tiles_per_seq)
        gc, dmat, xi, zeta = _ret_tables(ret_decay[l])
        o_f, o_b = _retention(px, gc, dmat, xi, zeta, nb, seq, n_ctx)
        yw = _window(px, sink_l, nb, seq, n_ctx, not last)
        yg = _global(px, gvt, nb, seq, n_ctx, not last)
        tiles = n_x_tiles if last else n_tiles
        xa, h2, e_idx_t, gate_t = _outproj(o_f, o_b, px, yw, yg, xa, mod[l], norm2[l][None, :], w_out, l,
                                           wr_t, b_r, nb, n_x_tiles, tiles_per_seq, tiles)
        n_tok = n_x if last else ta
        y0, y1 = _moe(h2, e_idx_t, w_gate_up, w_down, l, n_tok)
        xa = _combine(xa, y0, y1, gate_t.T, mod[l], nb, n_x_tiles, tiles_per_seq, tiles, not last)
    return xa.reshape(nb, seq, d)
```

```python
import functools

import jax
import jax.numpy as jnp
from jax import lax
from jax.experimental import pallas as pl
from jax.experimental.pallas import tpu as pltpu

GRID_W = 64
HEAD_DIM = 64
RET_HEADS = 4
RET_DK = 64
RET_DV = 128
WIN_Q_HEADS = 4
WIN_KV_HEADS = 2
WINDOW = 128
GLB_Q_HEADS = 4
GLB_KV_HEADS = 2
ROPE_BASE = 10000.0
N_EXPERTS = 16
N_GROUPS = 4
EXPERTS_PER_GROUP = N_EXPERTS // N_GROUPS
TOP_K = 2
EPS = 1e-6
NEG_INF = -1e30

RQ, RK, RV, RG = 0, 256, 512, 1024
WQ, WK, WV = 1536, 1792, 1920
GQ, GK, GV = 2048, 2304, 2432
D_IN = 2560
D_RET = RET_HEADS * RET_DV
D_ATT = WIN_Q_HEADS * HEAD_DIM
N_KV = WIN_KV_HEADS * HEAD_DIM
P_RQ, P_RK, P_RV, P_RG = 0, 256, 512, 1024
P_WQ, P_GQ, P_WK, P_GK = 1536, 1792, 2048, 2176
PX_W = 2304

TM = 512
RET_C = 256
TQ_W = 512
TQ_G = 256
KC_G = 256
LOG2_E = 1.4426950408889634
assert GLB_KV_HEADS == 2
TM_E = 512
VMEM_LIMIT = 56 * 1024 * 1024

F32 = jnp.float32
BF16 = jnp.bfloat16


def _dot(a, b):
    return jnp.dot(a, b, preferred_element_type=F32)


def _dot_nt(a, b):
    return lax.dot_general(a, b, (((1,), (1,)), ((), ())), preferred_element_type=F32)


def _dot_tn(a, b):
    return lax.dot_general(a, b, (((0,), (0,)), ((), ())), preferred_element_type=F32)


def _silu(x):
    return x * jax.nn.sigmoid(x)


def _params(sem):
    return pltpu.CompilerParams(dimension_semantics=sem, vmem_limit_bytes=VMEM_LIMIT)


def _ada_kernel(c_ref, w_ref, b_ref, o_ref):
    s = _silu(c_ref[...]).astype(BF16)
    o_ref[0] = _dot(s, w_ref[0].astype(BF16)) + b_ref[0]


def _ada(cc, ada_w, ada_b):
    depth, d, n = ada_w.shape
    tn = 1536
    rows = cc.shape[0]
    return pl.pallas_call(
        _ada_kernel,
        grid=(depth, n // tn),
        in_specs=[pl.BlockSpec((rows, d), lambda l, j: (0, 0)),
                  pl.BlockSpec((1, d, tn), lambda l, j: (l, 0, j)),
                  pl.BlockSpec((1, 1, tn), lambda l, j: (l, 0, j))],
        out_specs=pl.BlockSpec((1, rows, tn), lambda l, j: (l, 0, j)),
        out_shape=jax.ShapeDtypeStruct((depth, rows, n), F32),
        compiler_params=_params(("arbitrary", "arbitrary")),
        name="ada_mod",
    )(cc, ada_w, ada_b.reshape(depth, 1, n))


def _inproj_kernel(x_ref, mod_ref, n1_ref, wf_ref, wvt_ref, cos_ref, sin_ref, gains_ref, hm_ref, o_ref, vt_ref,
                   w_ref):
    @pl.when(pl.program_id(0) == 0)
    def _():
        w_ref[...] = wf_ref[...].astype(BF16)

    x = x_ref[...]
    ms = jnp.mean(x * x, axis=-1, keepdims=True)
    h = x * lax.rsqrt(ms + EPS) * n1_ref[...]
    h = h * (1.0 + mod_ref[0, 1:2, :]) + mod_ref[0, 0:1, :]
    hb = h.astype(BF16)

    def proj(lo, width):
        return _dot(hb, w_ref[:, lo:lo + width])

    def qk(lo, width, gain_row, scale):
        y = proj(lo, width)
        sq = y * y
        sq_hi = sq.astype(BF16)
        sq_lo = (sq - sq_hi.astype(F32)).astype(BF16)
        hm = hm_ref[0:width, 0:width]
        msq = _dot(sq_hi, hm) + _dot(sq_lo, hm)
        yn = y * lax.rsqrt(msq + EPS) * gains_ref[gain_row:gain_row + 1, 0:width]
        nxt = pltpu.roll(yn, width - HEAD_DIM // 4, 1)
        prv = pltpu.roll(yn, HEAD_DIM // 4, 1)
        lane = lax.broadcasted_iota(jnp.int32, yn.shape, 1)
        partner = jnp.where((lane % (HEAD_DIM // 2)) < HEAD_DIM // 4, nxt, prv)
        yr = yn * cos_ref[:, 0:width] + partner * sin_ref[:, 0:width]
        return (yr * scale).astype(BF16)

    o_ref[:, P_RQ:P_RK] = proj(RQ, RK - RQ).astype(BF16)
    o_ref[:, P_RK:P_RV] = (proj(RK, RV - RK) * RET_DK ** -0.5).astype(BF16)
    o_ref[:, P_RV:P_RG] = proj(RV, RG - RV).astype(BF16)
    o_ref[:, P_RG:P_WQ] = proj(RG, WQ - RG).astype(BF16)
    o_ref[:, P_WQ:P_GQ] = qk(WQ, D_ATT, 0, HEAD_DIM ** -0.5 * LOG2_E)
    o_ref[:, P_GQ:P_WK] = qk(GQ, D_ATT, 2, HEAD_DIM ** -0.5 * LOG2_E)
    o_ref[:, P_WK:P_GK] = qk(WK, N_KV, 1, 1.0)
    o_ref[:, P_GK:PX_W] = qk(GK, N_KV, 3, 1.0)
    vt_ref[...] = _dot_nt(wvt_ref[...], hb).astype(BF16)


def _inproj(xa, mod_l, n1, w_in, layer, cos_t, sin_t, gains, hm, nb, n_x_tiles, tiles_per_seq):
    ta, d = xa.shape
    n_tiles = ta // TM
    w_vt = jnp.concatenate([w_in[layer, :, WV:WV + N_KV], w_in[layer, :, GV:GV + N_KV]], axis=1).T.astype(BF16)
    n_gv = 2 * N_KV
    once = pl.Buffered(1)

    def mod_idx(i):
        return (jnp.where(i < n_x_tiles, i // tiles_per_seq, nb), 0, 0)

    def rope_idx(i):
        return (jnp.where(i < n_x_tiles, i % tiles_per_seq, tiles_per_seq), 0)

    return pl.pallas_call(
        _inproj_kernel,
        grid=(n_tiles,),
        in_specs=[pl.BlockSpec((TM, d), lambda i: (i, 0)),
                  pl.BlockSpec((1, 6, d), mod_idx),
                  pl.BlockSpec((1, d), lambda i: (0, 0)),
                  pl.BlockSpec((None, d, D_IN), lambda i: (layer, 0, 0), pipeline_mode=once),
                  pl.BlockSpec((n_gv, d), lambda i: (0, 0), pipeline_mode=once),
                  pl.BlockSpec((TM, D_ATT), rope_idx),
                  pl.BlockSpec((TM, D_ATT), rope_idx),
                  pl.BlockSpec((8, D_ATT), lambda i: (0, 0)),
                  pl.BlockSpec((D_ATT, D_ATT), lambda i: (0, 0))],
        out_specs=[pl.BlockSpec((TM, PX_W), lambda i: (i, 0)),
                   pl.BlockSpec((n_gv, TM), lambda i: (0, i))],
        out_shape=[jax.ShapeDtypeStruct((ta, PX_W), BF16),
                   jax.ShapeDtypeStruct((n_gv, ta), BF16)],
        scratch_shapes=[pltpu.VMEM((d, D_IN), BF16)],
        compiler_params=_params(("arbitrary",)),
        name="in_proj",
    )(xa, mod_l, n1, w_in, w_vt, cos_t, sin_t, gains, hm)


def _ret_kernel(gc_ref, qf_ref, kf_ref, vf_ref, qb_ref, kb_ref, vb_ref, dmat_ref, xi_ref, zeta_ref,
                of_ref, ob_ref, s_ref):
    @pl.when(pl.program_id(1) == 0)
    def _():
        s_ref[...] = jnp.zeros_like(s_ref)

    dirs = ((qf_ref, kf_ref, vf_ref, of_ref), (qb_ref, kb_ref, vb_ref, ob_ref))
    for d, (q_ref, k_ref, v_ref, o_ref) in enumerate(dirs):
        q = q_ref[...]
        k = k_ref[...]
        v = v_ref[...]
        kz = (k.astype(F32) * zeta_ref[d]).astype(BF16)
        outs = []
        for h in range(RET_HEADS):
            i = d * RET_HEADS + h
            qh = q[:, h * RET_DK:(h + 1) * RET_DK]
            kh = k[:, h * RET_DK:(h + 1) * RET_DK]
            vh = v[:, h * RET_DV:(h + 1) * RET_DV]
            att = _dot_nt(qh, kh) * dmat_ref[i]
            state = s_ref[i]
            outs.append(_dot(att.astype(BF16), vh) + _dot(qh, state.astype(BF16)) * xi_ref[i])
            s_ref[i] = gc_ref[i] * state + _dot_tn(kz[:, h * RET_DK:(h + 1) * RET_DK], vh)
        o_ref[...] = jnp.concatenate(outs, axis=-1).astype(BF16)


def _retention(px, gc, dmat, xi, zeta, nb, seq, n_ctx):
    ta = px.shape[0]
    n_x = seq // RET_C
    n_c = n_ctx // RET_C
    ctx_base = nb * n_x
    steps = n_c + n_x

    def row_f(b, c):
        return jnp.where(c < n_c, ctx_base + b * n_c + c, b * n_x + (c - n_c))

    def row_b(b, c):
        return jnp.where(c < n_c, ctx_base + b * n_c + (n_c - 1 - c), b * n_x + (steps - 1 - c))

    def spec(width, col, row):
        return pl.BlockSpec((RET_C, width), lambda b, c: (row(b, c), col))

    const3 = lambda b, c: (0, 0, 0)
    return pl.pallas_call(
        _ret_kernel,
        grid=(nb, steps),
        in_specs=[pl.BlockSpec(memory_space=pltpu.SMEM),
                  spec(256, P_RQ // 256, row_f), spec(256, P_RK // 256, row_f), spec(D_RET, P_RV // D_RET, row_f),
                  spec(256, P_RQ // 256, row_b), spec(256, P_RK // 256, row_b), spec(D_RET, P_RV // D_RET, row_b),
                  pl.BlockSpec((2 * RET_HEADS, RET_C, RET_C), const3),
                  pl.BlockSpec((2 * RET_HEADS, RET_C, RET_DV), const3),
                  pl.BlockSpec((2, RET_C, RET_HEADS * RET_DK), const3)],
        out_specs=[spec(D_RET, 0, row_f), spec(D_RET, 0, row_b)],
        out_shape=[jax.ShapeDtypeStruct((ta, D_RET), BF16)] * 2,
        scratch_shapes=[pltpu.VMEM((2 * RET_HEADS, RET_DK, RET_DV), F32)],
        compiler_params=_params(("parallel", "arbitrary")),
        name="retention",
    )(gc, px, px, px, px, px, px, dmat, xi, zeta)


def _ret_tables(decay_logit):
    log_g = jax.nn.log_sigmoid(decay_logit.astype(F32)).reshape(2 * RET_HEADS)
    idx = jnp.arange(RET_C, dtype=F32)
    diff = idx[:, None] - idx[None, :]
    lg = log_g[:, None, None]
    d_fwd = jnp.where(diff >= 0, jnp.exp(jnp.maximum(diff, 0.0) * lg), 0.0)
    d_bwd = jnp.where(diff <= 0, jnp.exp(jnp.maximum(-diff, 0.0) * lg), 0.0)
    is_bwd = (jnp.arange(2 * RET_HEADS) >= RET_HEADS)[:, None, None]
    dmat = jnp.where(is_bwd, d_bwd, d_fwd)
    pos = jnp.where(is_bwd[:, :, 0], RET_C - 1.0 - idx[None, :], idx[None, :])
    xi = jnp.exp((pos + 1.0) * log_g[:, None])
    zeta = jnp.exp((RET_C - 1.0 - pos) * log_g[:, None])
    gc = jnp.exp(RET_C * log_g)
    xi = jnp.broadcast_to(xi[:, :, None], (2 * RET_HEADS, RET_C, RET_DV))
    zeta = jnp.repeat(zeta.reshape(2, RET_HEADS, RET_C).transpose(0, 2, 1), RET_DK, axis=-1)
    return gc, dmat, xi, zeta


def _pad_heads(q, n_kv):
    rows, width = q.shape
    group = width // HEAD_DIM // n_kv
    zeros = jnp.zeros((rows, HEAD_DIM), BF16)
    out = []
    for h in range(n_kv):
        for g in range(group):
            qh = q[:, (h * group + g) * HEAD_DIM:(h * group + g + 1) * HEAD_DIM]
            out.append(jnp.concatenate([qh if hh == h else zeros for hh in range(n_kv)], axis=1))
    return jnp.concatenate(out, axis=0)


def _fill_values(v_ext, lo, vt):
    n = vt.shape[1]
    for h in range(v_ext.shape[0]):
        v_ext[h, 0:HEAD_DIM, lo:lo + n] = vt[h * HEAD_DIM:(h + 1) * HEAD_DIM, :]


def _store_heads(o_ref, col, h, group, oe, den):
    n = oe.shape[1] // group
    o = oe[0:HEAD_DIM] / den
    for g in range(group):
        r0 = (h * group + g) * HEAD_DIM
        o_ref[r0:r0 + HEAD_DIM, col:col + n] = o[:, g * n:(g + 1) * n].astype(BF16)


def _win_kernel(q_ref, k_ref, kc_ref, vt_ref, vtc_ref, sink_ref, tri_ref, o_ref, k_pad, v_ext, *, seq, n_ctx):
    i = pl.program_id(1)
    group = WIN_Q_HEADS // WIN_KV_HEADS
    sub = WINDOW
    w = group * sub
    span = 3 * WINDOW
    c0 = seq + 2 * WINDOW
    n_sub = q_ref.shape[0] // sub

    @pl.when(i == 0)
    def _():
        zk = jnp.zeros((WINDOW, N_KV), BF16)
        k_pad[0:WINDOW, :] = zk
        k_pad[WINDOW:WINDOW + seq, :] = k_ref[...]
        k_pad[WINDOW + seq:c0, :] = zk
        k_pad[c0:c0 + n_ctx, :] = kc_ref[...]
        v_ext[:, 0:HEAD_DIM, 0:WINDOW] = jnp.zeros((WIN_KV_HEADS, HEAD_DIM, WINDOW), BF16)
        v_ext[:, 0:HEAD_DIM, WINDOW + seq:c0] = jnp.zeros((WIN_KV_HEADS, HEAD_DIM, WINDOW), BF16)
        _fill_values(v_ext, WINDOW, vt_ref[...])
        _fill_values(v_ext, c0, vtc_ref[...])
        v_ext[:, HEAD_DIM:2 * HEAD_DIM, :] = jnp.ones((WIN_KV_HEADS, HEAD_DIM, c0 + n_ctx), BF16)

    sink = sink_ref[...]
    kc = k_pad[c0:c0 + n_ctx, :]
    start = i * q_ref.shape[0]

    def scores(a):
        qs = pl.multiple_of(start + a * sub, sub)
        q4 = _pad_heads(q_ref[a * sub:(a + 1) * sub, :], WIN_KV_HEADS)
        return a, qs, _dot_nt(k_pad[pl.ds(qs, span), :], q4), _dot_nt(kc, q4)

    def softmax(a, qs, st, sc):
        lo_edge = jnp.where(qs == 0, NEG_INF, 0.0)
        hi_edge = jnp.where(qs + sub == seq, NEG_INF, 0.0)
        s0 = st[0:WINDOW] + (tri_ref[0] + lo_edge)
        s1 = st[WINDOW:2 * WINDOW]
        s2 = st[2 * WINDOW:span] + (tri_ref[1] + hi_edge)
        m = jnp.maximum(jnp.maximum(jnp.max(s0, axis=0, keepdims=True), jnp.max(s1, axis=0, keepdims=True)),
                        jnp.maximum(jnp.max(s2, axis=0, keepdims=True), jnp.max(sc, axis=0, keepdims=True)))
        m = jnp.maximum(m, sink)
        pw = jnp.concatenate([jnp.exp2(s0 - m), jnp.exp2(s1 - m), jnp.exp2(s2 - m)], axis=0).astype(BF16)
        return a, qs, m, pw, jnp.exp2(sc - m).astype(BF16)

    def values(a, qs, m, pw, pc):
        for h in range(WIN_KV_HEADS):
            cs = slice(h * w, (h + 1) * w)
            oe = _dot(v_ext[h, :, pl.ds(qs, span)], pw[:, cs]) + _dot(v_ext[h, :, c0:c0 + n_ctx], pc[:, cs])
            den = oe[HEAD_DIM:HEAD_DIM + 1] + jnp.exp2(sink[:, cs] - m[:, cs])
            _store_heads(o_ref, a * sub, h, group, oe, den)

    nxt = scores(0)
    pend = None
    for a in range(n_sub):
        cur = nxt
        if a + 1 < n_sub:
            nxt = scores(a + 1)
        if pend is not None:
            values(*pend)
        pend = softmax(*cur)
    values(*pend)


def _window(px, vt, sink_row, tri, nb, seq, n_ctx, with_ctx_cols):
    ta = px.shape[0] if with_ctx_cols else nb * seq
    assert seq % TQ_W == 0 and TQ_W % WINDOW == 0
    n_q = seq // TQ_W
    ctx_rows = (nb * seq) // n_ctx
    n_keys = seq + 2 * WINDOW + n_ctx
    return pl.pallas_call(
        functools.partial(_win_kernel, seq=seq, n_ctx=n_ctx),
        grid=(nb, n_q),
        in_specs=[pl.BlockSpec((TQ_W, D_ATT), lambda b, i: (b * n_q + i, P_WQ // D_ATT)),
                  pl.BlockSpec((seq, N_KV), lambda b, i: (b, P_WK // N_KV)),
                  pl.BlockSpec((n_ctx, N_KV), lambda b, i: (ctx_rows + b, P_WK // N_KV)),
                  pl.BlockSpec((N_KV, seq), lambda b, i: (0, b)),
                  pl.BlockSpec((N_KV, n_ctx), lambda b, i: (0, ctx_rows + b)),
                  pl.BlockSpec((1, WIN_Q_HEADS * WINDOW), lambda b, i: (0, 0)),
                  pl.BlockSpec((2, WINDOW, WIN_Q_HEADS * WINDOW), lambda b, i: (0, 0, 0))],
        out_specs=pl.BlockSpec((D_ATT, TQ_W), lambda b, i: (0, b * n_q + i)),
        out_shape=jax.ShapeDtypeStruct((D_ATT, ta), BF16),
        scratch_shapes=[pltpu.VMEM((n_keys, N_KV), BF16),
                        pltpu.VMEM((WIN_KV_HEADS, 2 * HEAD_DIM, n_keys), BF16)],
        compiler_params=_params(("parallel", "arbitrary")),
        name="window_attn",
    )(px, px, px, vt, vt, sink_row, tri)


def _ctx_kernel(qw_ref, kw_ref, qg_ref, kg_ref, vt_ref, sink_ref, yw_in, yg_in, yw_ref, yg_ref):
    del yw_in, yg_in
    n_ctx = qw_ref.shape[0]
    sub = WINDOW
    ones = jnp.ones((HEAD_DIM, n_ctx), BF16)
    for q_ref, k_ref, row0, o_ref, sink in ((qw_ref, kw_ref, 0, yw_ref, sink_ref[...]),
                                            (qg_ref, kg_ref, N_KV, yg_ref, None)):
        n_kv = k_ref.shape[1] // HEAD_DIM
        group = q_ref.shape[1] // HEAD_DIM // n_kv
        w = group * sub
        k = k_ref[...]
        for a in range(n_ctx // sub):
            sc = _dot_nt(k, _pad_heads(q_ref[a * sub:(a + 1) * sub, :], n_kv))
            m = jnp.max(sc, axis=0, keepdims=True)
            if sink is not None:
                m = jnp.maximum(m, sink)
            p = jnp.exp2(sc - m).astype(BF16)
            for h in range(n_kv):
                cs = slice(h * w, (h + 1) * w)
                ve = jnp.concatenate([vt_ref[row0 + h * HEAD_DIM:row0 + (h + 1) * HEAD_DIM, :], ones], axis=0)
                oe = _dot(ve, p[:, cs])
                den = oe[HEAD_DIM:HEAD_DIM + 1]
                if sink is not None:
                    den = den + jnp.exp2(sink[:, cs] - m[:, cs])
                _store_heads(o_ref, a * sub, h, group, oe, den)


def _ctx_attention(px, vt, sink_row, ywt, ygt, nb, seq, n_ctx):
    ctx_rows = (nb * seq) // n_ctx
    row = lambda col: (lambda b: (ctx_rows + b, col))
    any_spec = pl.BlockSpec(memory_space=pl.ANY)
    out_spec = pl.BlockSpec((D_ATT, n_ctx), lambda b: (0, ctx_rows + b))
    return pl.pallas_call(
        _ctx_kernel,
        grid=(nb,),
        in_specs=[pl.BlockSpec((n_ctx, D_ATT), row(P_WQ // D_ATT)),
                  pl.BlockSpec((n_ctx, N_KV), row(P_WK // N_KV)),
                  pl.BlockSpec((n_ctx, D_ATT), row(P_GQ // D_ATT)),
                  pl.BlockSpec((n_ctx, N_KV), row(P_GK // N_KV)),
                  pl.BlockSpec((2 * N_KV, n_ctx), lambda b: (0, ctx_rows + b)),
                  pl.BlockSpec((1, WIN_Q_HEADS * WINDOW), lambda b: (0, 0)),
                  any_spec, any_spec],
        out_specs=[out_spec, out_spec],
        out_shape=[jax.ShapeDtypeStruct(ywt.shape, BF16), jax.ShapeDtypeStruct(ygt.shape, BF16)],
        input_output_aliases={6: 0, 7: 1},
        compiler_params=_params(("parallel",)),
        name="ctx_attn",
    )(px, px, px, px, vt, sink_row, ywt, ygt)


def _glb_kernel(q_ref, k_ref, kc_ref, vt_ref, vtc_ref, o_ref, k_all, v_ext, *, seq, n_ctx):
    i = pl.program_id(1)
    group = GLB_Q_HEADS // GLB_KV_HEADS
    tq = q_ref.shape[0]
    w = group * tq

    @pl.when(i == 0)
    def _():
        k_all[0:seq, :] = k_ref[...]
        k_all[seq:seq + n_ctx, :] = kc_ref[...]
        _fill_values(v_ext, 0, vt_ref[...])
        _fill_values(v_ext, seq, vtc_ref[...])
        v_ext[:, HEAD_DIM:2 * HEAD_DIM, :] = jnp.ones((GLB_KV_HEADS, HEAD_DIM, seq + n_ctx), BF16)

    def attend(chunks):
        q4 = _pad_heads(q_ref[...], GLB_KV_HEADS)

        def scores(chunk):
            lo, nk = chunk
            return _dot_nt(k_all[lo:lo + nk, :], q4)

        m = None
        acc = [None] * GLB_KV_HEADS

        def accumulate(pend):
            pt, alpha, (lo, nk) = pend
            for h in range(GLB_KV_HEADS):
                pv = _dot(v_ext[h, :, lo:lo + nk], pt[:, h * w:(h + 1) * w])
                acc[h] = pv if alpha is None else acc[h] * alpha[:, h * w:(h + 1) * w] + pv

        st_next = scores(chunks[0])
        pend = None
        for ci, chunk in enumerate(chunks):
            st = st_next
            if ci + 1 < len(chunks):
                st_next = scores(chunks[ci + 1])
            if pend is not None:
                accumulate(pend)
            cm = jnp.max(st, axis=0, keepdims=True)
            m_new = cm if m is None else jnp.maximum(m, cm)
            pt = jnp.exp2(st - m_new).astype(BF16)
            alpha = None if m is None else jnp.exp2(m - m_new)
            pend = (pt, alpha, chunk)
            m = m_new
        accumulate(pend)
        for h in range(GLB_KV_HEADS):
            _store_heads(o_ref, 0, h, group, acc[h], acc[h][HEAD_DIM:HEAD_DIM + 1])

    attend([(c * KC_G, KC_G) for c in range(seq // KC_G)] + [(seq, n_ctx)])


def _global(px, vt, nb, seq, n_ctx, with_ctx_cols):
    ta = px.shape[0] if with_ctx_cols else nb * seq
    assert seq % KC_G == 0 and seq % TQ_G == 0
    n_q = seq // TQ_G
    ctx_rows = (nb * seq) // n_ctx
    return pl.pallas_call(
        functools.partial(_glb_kernel, seq=seq, n_ctx=n_ctx),
        grid=(nb, n_q),
        in_specs=[pl.BlockSpec((TQ_G, D_ATT), lambda b, i: (b * n_q + i, P_GQ // D_ATT)),
                  pl.BlockSpec((seq, N_KV), lambda b, i: (b, P_GK // N_KV)),
                  pl.BlockSpec((n_ctx, N_KV), lambda b, i: (ctx_rows + b, P_GK // N_KV)),
                  pl.BlockSpec((N_KV, seq), lambda b, i: (1, b)),
                  pl.BlockSpec((N_KV, n_ctx), lambda b, i: (1, ctx_rows + b))],
        out_specs=pl.BlockSpec((D_ATT, TQ_G), lambda b, i: (0, b * n_q + i)),
        out_shape=jax.ShapeDtypeStruct((D_ATT, ta), BF16),
        scratch_shapes=[pltpu.VMEM((seq + n_ctx, N_KV), BF16),
                        pltpu.VMEM((GLB_KV_HEADS, 2 * HEAD_DIM, seq + n_ctx), BF16)],
        compiler_params=_params(("parallel", "arbitrary")),
        name="global_attn",
    )(px, px, px, vt, vt)


def _route_rows(logits, bias):
    sig = jax.nn.sigmoid(logits)
    biased = sig + bias
    b_rows = [biased[e:e + 1, :] for e in range(N_EXPERTS)]
    s_rows = [sig[e:e + 1, :] for e in range(N_EXPERTS)]
    n_loc = EXPERTS_PER_GROUP

    best_score, grp = None, None
    for g in range(N_GROUPS):
        a = b_rows[g * n_loc:(g + 1) * n_loc]
        top2 = None
        for i in range(n_loc):
            for j in range(i + 1, n_loc):
                pair = a[i] + a[j]
                top2 = pair if top2 is None else jnp.maximum(top2, pair)
        if g == 0:
            best_score, grp = top2, jnp.zeros(top2.shape, jnp.int32)
        else:
            upd = top2 > best_score
            grp = jnp.where(upd, g, grp)
            best_score = jnp.where(upd, top2, best_score)

    def pick(rows, i):
        out = rows[i]
        for g in range(1, N_GROUPS):
            out = jnp.where(grp == g, rows[g * n_loc + i], out)
        return out

    cand = [pick(b_rows, i) for i in range(n_loc)]
    cand_s = [pick(s_rows, i) for i in range(n_loc)]
    m1, l1, w1 = cand[0], jnp.zeros(grp.shape, jnp.int32), cand_s[0]
    for i in range(1, n_loc):
        upd = cand[i] > m1
        m1 = jnp.where(upd, cand[i], m1)
        l1 = jnp.where(upd, i, l1)
        w1 = jnp.where(upd, cand_s[i], w1)
    m2, l2, w2 = None, None, None
    for i in range(n_loc):
        rest = jnp.where(l1 == i, -jnp.inf, cand[i])
        if i == 0:
            m2, l2, w2 = rest, jnp.zeros(grp.shape, jnp.int32), cand_s[0]
        else:
            upd = rest > m2
            m2 = jnp.where(upd, rest, m2)
            l2 = jnp.where(upd, i, l2)
            w2 = jnp.where(upd, cand_s[i], w2)
    tot = w1 + w2
    e_idx = jnp.concatenate([grp * n_loc + l1, grp * n_loc + l2], axis=0)
    gate = jnp.concatenate([w1 / tot, w2 / tot], axis=0)
    return e_idx, gate


def _outproj_kernel(of_ref, ob_ref, g_ref, ywt_ref, ygt_ref, x_ref, mod_ref, n2_ref, wf_ref, wr_ref, br_ref,
                    xo_ref, h_ref, e_ref, gate_ref, w_ref):
    @pl.when(pl.program_id(0) == 0)
    def _():
        w_ref[...] = wf_ref[...].astype(BF16)

    o = of_ref[...].astype(F32) + ob_ref[...].astype(F32)
    normed = []
    for h in range(RET_HEADS):
        oh = o[:, h * RET_DV:(h + 1) * RET_DV]
        mu = jnp.mean(oh, axis=-1, keepdims=True)
        var = jnp.mean(jnp.square(oh - mu), axis=-1, keepdims=True)
        normed.append((oh - mu) * lax.rsqrt(var + EPS))
    yr = _silu(g_ref[...].astype(F32)) * jnp.concatenate(normed, axis=-1)
    acc = _dot(yr.astype(BF16), w_ref[0:D_RET, :])
    acc += _dot_tn(ywt_ref[...], w_ref[D_RET:D_RET + D_ATT, :])
    acc += _dot_tn(ygt_ref[...], w_ref[D_RET + D_ATT:D_RET + 2 * D_ATT, :])
    x = x_ref[...] + mod_ref[0, 2:3, :] * acc
    xo_ref[...] = x
    ms = jnp.mean(x * x, axis=-1, keepdims=True)
    h2 = x * lax.rsqrt(ms + EPS) * n2_ref[...]
    h2 = (h2 * (1.0 + mod_ref[0, 4:5, :]) + mod_ref[0, 3:4, :]).astype(BF16)
    h_ref[...] = h2
    e_idx, gate = _route_rows(_dot_nt(wr_ref[...], h2), br_ref[...])
    e_ref[...] = e_idx
    gate_ref[...] = gate


def _outproj(o_f, o_b, px, yw, yg, xa, mod_l, n2, w_out, layer, wr_t, b_r, nb, n_x_tiles, tiles_per_seq, n_tiles):
    ta, d = xa.shape
    once = pl.Buffered(1)

    def mod_idx(i):
        return (jnp.where(i < n_x_tiles, i // tiles_per_seq, nb), 0, 0)

    row = lambda i: (i, 0)
    return pl.pallas_call(
        _outproj_kernel,
        grid=(n_tiles,),
        in_specs=[pl.BlockSpec((TM, D_RET), row),
                  pl.BlockSpec((TM, D_RET), row),
                  pl.BlockSpec((TM, D_RET), lambda i: (i, P_RG // D_RET)),
                  pl.BlockSpec((D_ATT, TM), lambda i: (0, i)),
                  pl.BlockSpec((D_ATT, TM), lambda i: (0, i)),
                  pl.BlockSpec((TM, d), row),
                  pl.BlockSpec((1, 6, d), mod_idx),
                  pl.BlockSpec((1, d), lambda i: (0, 0)),
                  pl.BlockSpec((None, d, d), lambda i: (layer, 0, 0), pipeline_mode=once),
                  pl.BlockSpec((N_EXPERTS, d), lambda i: (0, 0)),
                  pl.BlockSpec((N_EXPERTS, 1), lambda i: (0, 0))],
        out_specs=[pl.BlockSpec((TM, d), row),
                   pl.BlockSpec((TM, d), row),
                   pl.BlockSpec((TOP_K, TM), lambda i: (0, i)),
                   pl.BlockSpec((TOP_K, TM), lambda i: (0, i))],
        out_shape=[jax.ShapeDtypeStruct((ta, d), F32),
                   jax.ShapeDtypeStruct((n_tiles * TM, d), BF16),
                   jax.ShapeDtypeStruct((TOP_K, n_tiles * TM), jnp.int32),
                   jax.ShapeDtypeStruct((TOP_K, n_tiles * TM), F32)],
        scratch_shapes=[pltpu.VMEM((d, d), BF16)],
        input_output_aliases={5: 0},
        compiler_params=_params(("arbitrary",)),
        name="out_proj",
    )(o_f, o_b, px, yw, yg, xa, mod_l, n2, w_out, wr_t, b_r)


def _moe_kernel(be_ref, first_ref, nu_ref, x_ref, wgu_f_ref, wd_f_ref, o_ref, wgu_ref, wd_ref):
    i = pl.program_id(0)

    @pl.when(jnp.logical_and(i < nu_ref[0], first_ref[i] == 1))
    def _():
        wgu_ref[...] = wgu_f_ref[...].astype(BF16)
        wd_ref[...] = wd_f_ref[...].astype(BF16)

    @pl.when(i < nu_ref[0])
    def _():
        f = wd_ref.shape[0]
        au = _dot(x_ref[...], wgu_ref[...])
        mid = (_silu(au[:, :f]) * au[:, f:]).astype(BF16)
        o_ref[...] = _dot(mid, wd_ref[...]).astype(BF16)

    @pl.when(i >= nu_ref[0])
    def _():
        o_ref[...] = jnp.zeros_like(o_ref)


def _moe_ffn(blk_e, first, n_used, xs, w_gu, w_down, layer):
    rows, d = xs.shape
    f2 = w_gu.shape[3]
    f = w_down.shape[2]
    grid_spec = pltpu.PrefetchScalarGridSpec(
        num_scalar_prefetch=3,
        grid=(rows // TM_E,),
        in_specs=[pl.BlockSpec((TM_E, d), lambda i, be, fi, nu: (i, 0)),
                  pl.BlockSpec((None, None, d, f2), lambda i, be, fi, nu: (layer, be[i], 0, 0)),
                  pl.BlockSpec((None, None, f, d), lambda i, be, fi, nu: (layer, be[i], 0, 0))],
        out_specs=pl.BlockSpec((TM_E, d), lambda i, be, fi, nu: (i, 0)),
        scratch_shapes=[pltpu.VMEM((d, f2), BF16), pltpu.VMEM((f, d), BF16)],
    )
    return pl.pallas_call(
        _moe_kernel,
        grid_spec=grid_spec,
        out_shape=jax.ShapeDtypeStruct((rows, d), BF16),
        compiler_params=_params(("arbitrary",)),
        name="moe_ffn",
    )(blk_e, first, n_used, xs, w_gu, w_down)


def _moe(h2, e_idx_t, w_gu, w_down, layer, n_tok):
    n_asg = n_tok * TOP_K
    flat_e = e_idx_t[:, :n_tok].reshape(-1)
    onehot = (flat_e[:, None] == jnp.arange(N_EXPERTS, dtype=jnp.int32)[None, :]).astype(jnp.int32)
    csum = jnp.cumsum(onehot, axis=0)
    counts = csum[-1]
    rank = jnp.sum(csum * onehot, axis=1) - 1
    padded = (counts + TM_E - 1) // TM_E * TM_E
    pad_end = jnp.cumsum(padded)
    pad_start = pad_end - padded
    cnt_start = jnp.cumsum(counts) - counts
    pos = (jnp.sum(pad_start[None, :] * onehot, axis=1) + rank).reshape(TOP_K, n_tok)
    n_blocks = (n_asg + N_EXPERTS * (TM_E - 1) + TM_E - 1) // TM_E
    blk_start = jnp.arange(n_blocks, dtype=jnp.int32) * TM_E
    blk_e = jnp.minimum(jnp.sum(blk_start[:, None] >= pad_end[None, :], axis=1), N_EXPERTS - 1).astype(jnp.int32)
    first = jnp.concatenate([jnp.ones((1,), jnp.int32), (blk_e[1:] != blk_e[:-1]).astype(jnp.int32)])
    n_used = (pad_end[-1] // TM_E).astype(jnp.int32).reshape(1)
    order = jnp.argsort(flat_e)
    e_row = jnp.repeat(blk_e, TM_E)
    r = jnp.arange(n_blocks * TM_E, dtype=jnp.int32) - pad_start[e_row]
    src = order[jnp.clip(cnt_start[e_row] + jnp.minimum(r, counts[e_row] - 1), 0, n_asg - 1)] % n_tok
    yp = _moe_ffn(blk_e, first, n_used, h2[src], w_gu, w_down, layer)
    return yp[pos[0]], yp[pos[1]]


def _combine_kernel(x_ref, y0_ref, y1_ref, gate_ref, mod_ref, o_ref):
    g = gate_ref[...]
    y = y0_ref[...].astype(F32) * g[:, 0:1] + y1_ref[...].astype(F32) * g[:, 1:2]
    o_ref[...] = x_ref[...] + mod_ref[0, 5:6, :] * y


def _combine(xa, y0, y1, gate, mod_l, nb, n_x_tiles, tiles_per_seq, n_tiles, in_place):
    d = xa.shape[1]

    def mod_idx(i):
        return (jnp.where(i < n_x_tiles, i // tiles_per_seq, nb), 0, 0)

    row = lambda i: (i, 0)
    return pl.pallas_call(
        _combine_kernel,
        grid=(n_tiles,),
        in_specs=[pl.BlockSpec((TM, d), row),
                  pl.BlockSpec((TM, d), row),
                  pl.BlockSpec((TM, d), row),
                  pl.BlockSpec((TM, TOP_K), row),
                  pl.BlockSpec((1, 6, d), mod_idx)],
        out_specs=pl.BlockSpec((TM, d), row),
        out_shape=jax.ShapeDtypeStruct((xa.shape[0] if in_place else n_tiles * TM, d), F32),
        input_output_aliases={0: 0} if in_place else {},
        compiler_params=_params(("parallel",)),
        name="moe_combine",
    )(xa, y0, y1, gate, mod_l)


def _rope_tables(seq, dtype):
    rows = seq // GRID_W
    row = jnp.repeat(jnp.arange(rows), GRID_W).astype(jnp.float32)
    col = (jnp.arange(rows * GRID_W) % GRID_W).astype(jnp.float32)
    half = HEAD_DIM // 2
    inv = jnp.power(ROPE_BASE, -jnp.arange(0, half, 2, dtype=jnp.float32) / half)
    ang_r, ang_c = row[:, None] * inv, col[:, None] * inv
    cos_r, cos_c = jnp.cos(ang_r).astype(dtype), jnp.cos(ang_c).astype(dtype)
    sin_r, sin_c = jnp.sin(ang_r).astype(dtype), jnp.sin(ang_c).astype(dtype)
    cos_h = jnp.concatenate([cos_r, cos_r, cos_c, cos_c], -1)
    sin_h = jnp.concatenate([-sin_r, sin_r, -sin_c, sin_c], -1)
    cos_t = jnp.concatenate([jnp.tile(cos_h, (1, D_ATT // HEAD_DIM)), jnp.ones((TM, D_ATT), dtype)], 0)
    sin_t = jnp.concatenate([jnp.tile(sin_h, (1, D_ATT // HEAD_DIM)), jnp.zeros((TM, D_ATT), dtype)], 0)
    return cos_t, sin_t


def kernel(x, c, ctx, c_ctx, ada_w, ada_b, norm1, norm2, w_in, w_out, ret_decay, win_qk_gain, win_sink,
           glb_qk_gain, w_router, b_router, w_gate_up, w_down):
    nb, seq, d = x.shape
    n_ctx = ctx.shape[1]
    depth = ada_w.shape[0]
    n_x = nb * seq
    ta = n_x + nb * n_ctx
    assert seq % TM == 0 and (nb * n_ctx) % TM == 0 and seq % GRID_W == 0
    tiles_per_seq = seq // TM
    n_x_tiles = n_x // TM
    n_tiles = ta // TM

    xa = jnp.concatenate([x.reshape(n_x, d), ctx.reshape(nb * n_ctx, d)], 0)
    mod_rows = (nb + 1 + 7) // 8 * 8
    cc = jnp.zeros((mod_rows, d), F32).at[:nb].set(c).at[nb].set(c_ctx)
    mod = _ada(cc, ada_w, ada_b).reshape(depth, mod_rows, 6, d)

    cos_t, sin_t = _rope_tables(seq, F32)
    head_of = jnp.arange(D_ATT) // HEAD_DIM
    hm = jnp.where(head_of[:, None] == head_of[None, :], 1.0 / HEAD_DIM, 0.0).astype(BF16)
    wr_t = w_router.T.astype(BF16)
    b_r = b_router.astype(F32)[:, None]
    key_j = jnp.arange(WINDOW)[:, None]
    qry_i = jnp.tile(jnp.arange(WINDOW), WIN_Q_HEADS)[None, :]
    tri = jnp.stack([jnp.where(key_j >= qry_i, 0.0, NEG_INF), jnp.where(key_j <= qry_i, 0.0, NEG_INF)]).astype(F32)

    for l in range(depth):
        last = l == depth - 1
        gains = jnp.zeros((8, D_ATT), F32)
        gains = gains.at[0].set(jnp.tile(win_qk_gain[l, 0], WIN_Q_HEADS))
        gains = gains.at[1, :WIN_KV_HEADS * HEAD_DIM].set(jnp.tile(win_qk_gain[l, 1], WIN_KV_HEADS))
        gains = gains.at[2].set(jnp.tile(glb_qk_gain[l, 0], GLB_Q_HEADS))
        gains = gains.at[3, :GLB_KV_HEADS * HEAD_DIM].set(jnp.tile(glb_qk_gain[l, 1], GLB_KV_HEADS))
        sink_row = jnp.repeat(win_sink[l].astype(F32) * LOG2_E, WINDOW)[None, :]

        px, vt = _inproj(xa, mod[l], norm1[l][None, :], w_in, l, cos_t, sin_t, gains, hm,
                         nb, n_x_tiles, tiles_per_seq)
        gc, dmat, xi, zeta = _ret_tables(ret_decay[l])
        o_f, o_b = _retention(px, gc, dmat, xi, zeta, nb, seq, n_ctx)
        yw = _window(px, vt, sink_row, tri, nb, seq, n_ctx, not last)
        yg = _global(px, vt, nb, seq, n_ctx, not last)
        if not last:
            yw, yg = _ctx_attention(px, vt, sink_row, yw, yg, nb, seq, n_ctx)
        tiles = n_x_tiles if last else n_tiles
        xa, h2, e_idx_t, gate_t = _outproj(o_f, o_b, px, yw, yg, xa, mod[l], norm2[l][None, :], w_out, l,
                                           wr_t, b_r, nb, n_x_tiles, tiles_per_seq, tiles)
        n_tok = n_x if last else ta
        y0, y1 = _moe(h2, e_idx_t, w_gate_up, w_down, l, n_tok)
        xa = _combine(xa, y0, y1, gate_t.T, mod[l], nb, n_x_tiles, tiles_per_seq, tiles, not last)
    return xa.reshape(nb, seq, d)
```

```python
import functools

import jax
import jax.numpy as jnp
from jax import lax
from jax.experimental import pallas as pl
from jax.experimental.pallas import tpu as pltpu

GRID_W = 64
HEAD_DIM = 64
RET_HEADS = 4
RET_DK = 64
RET_DV = 128
WIN_Q_HEADS = 4
WIN_KV_HEADS = 2
WINDOW = 128
GLB_Q_HEADS = 4
GLB_KV_HEADS = 2
ROPE_BASE = 10000.0
N_EXPERTS = 16
N_GROUPS = 4
EXPERTS_PER_GROUP = N_EXPERTS // N_GROUPS
TOP_K = 2
EPS = 1e-6
NEG_INF = -1e30

RQ, RK, RV, RG = 0, 256, 512, 1024
WQ, WK, WV = 1536, 1792, 1920
GQ, GK, GV = 2048, 2304, 2432
D_IN = 2560
D_RET = RET_HEADS * RET_DV
D_ATT = WIN_Q_HEADS * HEAD_DIM
N_KV = WIN_KV_HEADS * HEAD_DIM
P_RQ, P_RK, P_RV, P_RG = 0, 256, 512, 1024
P_WQ, P_GQ, P_WK, P_GK = 1536, 1792, 2048, 2176
PX_W = 2304

TM = 512
RET_C = 256
TQ_W = 512
TQ_G = 256
KC_G = 256
LOG2_E = 1.4426950408889634
assert GLB_KV_HEADS == 2
TM_E = 512
N_STREAMS = 2
VMEM_LIMIT = 56 * 1024 * 1024

F32 = jnp.float32
BF16 = jnp.bfloat16


def _dot(a, b):
    return jnp.dot(a, b, preferred_element_type=F32)


def _dot_nt(a, b):
    return lax.dot_general(a, b, (((1,), (1,)), ((), ())), preferred_element_type=F32)


def _dot_tn(a, b):
    return lax.dot_general(a, b, (((0,), (0,)), ((), ())), preferred_element_type=F32)


def _silu(x):
    return x * jax.nn.sigmoid(x)


def _params(sem):
    return pltpu.CompilerParams(dimension_semantics=sem, vmem_limit_bytes=VMEM_LIMIT)


def _ada_kernel(c_ref, w_ref, b_ref, o_ref):
    s = _silu(c_ref[...]).astype(BF16)
    o_ref[0] = _dot(s, w_ref[0].astype(BF16)) + b_ref[0]


def _ada(cc, ada_w, ada_b):
    depth, d, n = ada_w.shape
    tn = 1536
    rows = cc.shape[0]
    return pl.pallas_call(
        _ada_kernel,
        grid=(depth, n // tn),
        in_specs=[pl.BlockSpec((rows, d), lambda l, j: (0, 0)),
                  pl.BlockSpec((1, d, tn), lambda l, j: (l, 0, j)),
                  pl.BlockSpec((1, 1, tn), lambda l, j: (l, 0, j))],
        out_specs=pl.BlockSpec((1, rows, tn), lambda l, j: (l, 0, j)),
        out_shape=jax.ShapeDtypeStruct((depth, rows, n), F32),
        compiler_params=_params(("arbitrary", "arbitrary")),
        name="ada_mod",
    )(cc, ada_w, ada_b.reshape(depth, 1, n))


def _inproj_kernel(x_ref, mod_ref, n1_ref, wf_ref, wvt_ref, cos_ref, sin_ref, gains_ref, hm_ref, o_ref, vt_ref,
                   w_ref):
    @pl.when(pl.program_id(0) == 0)
    def _():
        w_ref[...] = wf_ref[...].astype(BF16)

    x = x_ref[...]
    ms = jnp.mean(x * x, axis=-1, keepdims=True)
    h = x * lax.rsqrt(ms + EPS) * n1_ref[...]
    h = h * (1.0 + mod_ref[0, 1:2, :]) + mod_ref[0, 0:1, :]
    hb = h.astype(BF16)

    def proj(lo, width):
        return _dot(hb, w_ref[:, lo:lo + width])

    def head_msq(y):
        sq = y * y
        sq_hi = sq.astype(BF16)
        sq_lo = (sq - sq_hi.astype(F32)).astype(BF16)
        hm = hm_ref[0:y.shape[1], 0:y.shape[1]]
        return _dot(sq_hi, hm) + _dot(sq_lo, hm)

    def qk_finish(y, msq, gain_row, scale):
        width = y.shape[1]
        yn = y * lax.rsqrt(msq + EPS) * gains_ref[gain_row:gain_row + 1, 0:width]
        nxt = pltpu.roll(yn, width - HEAD_DIM // 4, 1)
        prv = pltpu.roll(yn, HEAD_DIM // 4, 1)
        lane = lax.broadcasted_iota(jnp.int32, yn.shape, 1)
        partner = jnp.where((lane % (HEAD_DIM // 2)) < HEAD_DIM // 4, nxt, prv)
        yr = yn * cos_ref[:, 0:width] + partner * sin_ref[:, 0:width]
        return (yr * scale).astype(BF16)

    q_scale = HEAD_DIM ** -0.5 * LOG2_E
    qk_segs = ((WQ, D_ATT, 0, q_scale, P_WQ), (GQ, D_ATT, 2, q_scale, P_GQ),
               (WK, N_KV, 1, 1.0, P_WK), (GK, N_KV, 3, 1.0, P_GK))
    plain_segs = ((RQ, RK - RQ, 1.0, P_RQ), (RK, RV - RK, RET_DK ** -0.5, P_RK),
                  (RV, RG - RV, 1.0, P_RV), (RG, WQ - RG, 1.0, P_RG))
    ys = [proj(lo, width) for lo, width, _, _, _ in qk_segs]
    stats = [head_msq(y) for y in ys]
    vt_ref[...] = _dot_nt(wvt_ref[...], hb).astype(BF16)
    for (lo, width, scale, dst), (_, qwidth, gain_row, qscale, qdst), y, msq in zip(plain_segs, qk_segs, ys, stats):
        p = proj(lo, width)
        o_ref[:, dst:dst + width] = (p if scale == 1.0 else p * scale).astype(BF16)
        o_ref[:, qdst:qdst + qwidth] = qk_finish(y, msq, gain_row, qscale)


def _inproj(xa, mod_l, n1, w_in, layer, cos_t, sin_t, gains, hm, nb, n_x_tiles, tiles_per_seq):
    ta, d = xa.shape
    n_tiles = ta // TM
    w_vt = jnp.concatenate([w_in[layer, :, WV:WV + N_KV], w_in[layer, :, GV:GV + N_KV]], axis=1).T.astype(BF16)
    n_gv = 2 * N_KV
    once = pl.Buffered(1)

    def mod_idx(i):
        return (jnp.where(i < n_x_tiles, i // tiles_per_seq, nb), 0, 0)

    def rope_idx(i):
        return (jnp.where(i < n_x_tiles, i % tiles_per_seq, tiles_per_seq), 0)

    return pl.pallas_call(
        _inproj_kernel,
        grid=(n_tiles,),
        in_specs=[pl.BlockSpec((TM, d), lambda i: (i, 0)),
                  pl.BlockSpec((1, 6, d), mod_idx),
                  pl.BlockSpec((1, d), lambda i: (0, 0)),
                  pl.BlockSpec((None, d, D_IN), lambda i: (layer, 0, 0), pipeline_mode=once),
                  pl.BlockSpec((n_gv, d), lambda i: (0, 0), pipeline_mode=once),
                  pl.BlockSpec((TM, D_ATT), rope_idx),
                  pl.BlockSpec((TM, D_ATT), rope_idx),
                  pl.BlockSpec((8, D_ATT), lambda i: (0, 0)),
                  pl.BlockSpec((D_ATT, D_ATT), lambda i: (0, 0))],
        out_specs=[pl.BlockSpec((TM, PX_W), lambda i: (i, 0)),
                   pl.BlockSpec((n_gv, TM), lambda i: (0, i))],
        out_shape=[jax.ShapeDtypeStruct((ta, PX_W), BF16),
                   jax.ShapeDtypeStruct((n_gv, ta), BF16)],
        scratch_shapes=[pltpu.VMEM((d, D_IN), BF16)],
        compiler_params=_params(("arbitrary",)),
        name="in_proj",
    )(xa, mod_l, n1, w_in, w_vt, cos_t, sin_t, gains, hm)


def _ret_kernel(gc_ref, qf_ref, kf_ref, vf_ref, qb_ref, kb_ref, vb_ref, dmat_ref, xi_ref, zeta_ref,
                of_ref, ob_ref, s_ref):
    @pl.when(pl.program_id(1) == 0)
    def _():
        s_ref[...] = jnp.zeros_like(s_ref)

    dirs = ((qf_ref, kf_ref, vf_ref, of_ref), (qb_ref, kb_ref, vb_ref, ob_ref))
    for d, (q_ref, k_ref, v_ref, o_ref) in enumerate(dirs):
        q = q_ref[...]
        k = k_ref[...]
        v = v_ref[...]
        kz = (k.astype(F32) * zeta_ref[d]).astype(BF16)
        outs = []
        for h in range(RET_HEADS):
            i = d * RET_HEADS + h
            qh = q[:, h * RET_DK:(h + 1) * RET_DK]
            kh = k[:, h * RET_DK:(h + 1) * RET_DK]
            vh = v[:, h * RET_DV:(h + 1) * RET_DV]
            att = _dot_nt(qh, kh) * dmat_ref[i]
            state = s_ref[i]
            outs.append(_dot(att.astype(BF16), vh) + _dot(qh, state.astype(BF16)) * xi_ref[i])
            s_ref[i] = gc_ref[i] * state + _dot_tn(kz[:, h * RET_DK:(h + 1) * RET_DK], vh)
        o_ref[...] = jnp.concatenate(outs, axis=-1).astype(BF16)


def _retention(px, gc, dmat, xi, zeta, nb, seq, n_ctx):
    ta = px.shape[0]
    n_x = seq // RET_C
    n_c = n_ctx // RET_C
    ctx_base = nb * n_x
    steps = n_c + n_x

    def row_f(b, c):
        return jnp.where(c < n_c, ctx_base + b * n_c + c, b * n_x + (c - n_c))

    def row_b(b, c):
        return jnp.where(c < n_c, ctx_base + b * n_c + (n_c - 1 - c), b * n_x + (steps - 1 - c))

    def spec(width, col, row):
        return pl.BlockSpec((RET_C, width), lambda b, c: (row(b, c), col))

    const3 = lambda b, c: (0, 0, 0)
    return pl.pallas_call(
        _ret_kernel,
        grid=(nb, steps),
        in_specs=[pl.BlockSpec(memory_space=pltpu.SMEM),
                  spec(256, P_RQ // 256, row_f), spec(256, P_RK // 256, row_f), spec(D_RET, P_RV // D_RET, row_f),
                  spec(256, P_RQ // 256, row_b), spec(256, P_RK // 256, row_b), spec(D_RET, P_RV // D_RET, row_b),
                  pl.BlockSpec((2 * RET_HEADS, RET_C, RET_C), const3),
                  pl.BlockSpec((2 * RET_HEADS, RET_C, RET_DV), const3),
                  pl.BlockSpec((2, RET_C, RET_HEADS * RET_DK), const3)],
        out_specs=[spec(D_RET, 0, row_f), spec(D_RET, 0, row_b)],
        out_shape=[jax.ShapeDtypeStruct((ta, D_RET), BF16)] * 2,
        scratch_shapes=[pltpu.VMEM((2 * RET_HEADS, RET_DK, RET_DV), F32)],
        compiler_params=_params(("parallel", "arbitrary")),
        name="retention",
    )(gc, px, px, px, px, px, px, dmat, xi, zeta)


def _ret_tables(decay_logit):
    log_g = jax.nn.log_sigmoid(decay_logit.astype(F32)).reshape(2 * RET_HEADS)
    idx = jnp.arange(RET_C, dtype=F32)
    diff = idx[:, None] - idx[None, :]
    lg = log_g[:, None, None]
    d_fwd = jnp.where(diff >= 0, jnp.exp(jnp.maximum(diff, 0.0) * lg), 0.0)
    d_bwd = jnp.where(diff <= 0, jnp.exp(jnp.maximum(-diff, 0.0) * lg), 0.0)
    is_bwd = (jnp.arange(2 * RET_HEADS) >= RET_HEADS)[:, None, None]
    dmat = jnp.where(is_bwd, d_bwd, d_fwd)
    pos = jnp.where(is_bwd[:, :, 0], RET_C - 1.0 - idx[None, :], idx[None, :])
    xi = jnp.exp((pos + 1.0) * log_g[:, None])
    zeta = jnp.exp((RET_C - 1.0 - pos) * log_g[:, None])
    gc = jnp.exp(RET_C * log_g)
    xi = jnp.broadcast_to(xi[:, :, None], (2 * RET_HEADS, RET_C, RET_DV))
    zeta = jnp.repeat(zeta.reshape(2, RET_HEADS, RET_C).transpose(0, 2, 1), RET_DK, axis=-1)
    return gc, dmat, xi, zeta


def _pad_heads(q, n_kv):
    rows, width = q.shape
    group = width // HEAD_DIM // n_kv
    zeros = jnp.zeros((rows, HEAD_DIM), BF16)
    out = []
    for h in range(n_kv):
        for g in range(group):
            qh = q[:, (h * group + g) * HEAD_DIM:(h * group + g + 1) * HEAD_DIM]
            out.append(jnp.concatenate([qh if hh == h else zeros for hh in range(n_kv)], axis=1))
    return jnp.concatenate(out, axis=0)


def _fill_values(v_ext, lo, vt):
    n = vt.shape[1]
    for h in range(v_ext.shape[0]):
        v_ext[h, 0:HEAD_DIM, lo:lo + n] = vt[h * HEAD_DIM:(h + 1) * HEAD_DIM, :]


def _store_heads(o_ref, col, h, group, oe, den):
    n = oe.shape[1] // group
    o = oe[0:HEAD_DIM] / den
    for g in range(group):
        r0 = (h * group + g) * HEAD_DIM
        o_ref[r0:r0 + HEAD_DIM, col:col + n] = o[:, g * n:(g + 1) * n].astype(BF16)


def _win_kernel(q_ref, k_ref, kc_ref, vt_ref, vtc_ref, sink_ref, tri_ref, o_ref, k_pad, v_ext, *, seq, n_ctx):
    i = pl.program_id(1)
    group = WIN_Q_HEADS // WIN_KV_HEADS
    sub = WINDOW
    w = group * sub
    span = 3 * WINDOW
    c0 = seq + 2 * WINDOW
    n_sub = q_ref.shape[0] // sub

    @pl.when(i == 0)
    def _():
        zk = jnp.zeros((WINDOW, N_KV), BF16)
        k_pad[0:WINDOW, :] = zk
        k_pad[WINDOW:WINDOW + seq, :] = k_ref[...]
        k_pad[WINDOW + seq:c0, :] = zk
        k_pad[c0:c0 + n_ctx, :] = kc_ref[...]
        v_ext[:, 0:HEAD_DIM, 0:WINDOW] = jnp.zeros((WIN_KV_HEADS, HEAD_DIM, WINDOW), BF16)
        v_ext[:, 0:HEAD_DIM, WINDOW + seq:c0] = jnp.zeros((WIN_KV_HEADS, HEAD_DIM, WINDOW), BF16)
        _fill_values(v_ext, WINDOW, vt_ref[...])
        _fill_values(v_ext, c0, vtc_ref[...])
        v_ext[:, HEAD_DIM:2 * HEAD_DIM, :] = jnp.ones((WIN_KV_HEADS, HEAD_DIM, c0 + n_ctx), BF16)

    sink = sink_ref[...]
    kc = k_pad[c0:c0 + n_ctx, :]
    start = i * q_ref.shape[0]

    def scores(a):
        qs = pl.multiple_of(start + a * sub, sub)
        q4 = _pad_heads(q_ref[a * sub:(a + 1) * sub, :], WIN_KV_HEADS)
        return a, qs, _dot_nt(k_pad[pl.ds(qs, span), :], q4), _dot_nt(kc, q4)

    def softmax(a, qs, st, sc):
        lo_edge = jnp.where(qs == 0, NEG_INF, 0.0)
        hi_edge = jnp.where(qs + sub == seq, NEG_INF, 0.0)
        s0 = st[0:WINDOW] + (tri_ref[0] + lo_edge)
        s1 = st[WINDOW:2 * WINDOW]
        s2 = st[2 * WINDOW:span] + (tri_ref[1] + hi_edge)
        m = jnp.maximum(jnp.maximum(jnp.max(s0, axis=0, keepdims=True), jnp.max(s1, axis=0, keepdims=True)),
                        jnp.maximum(jnp.max(s2, axis=0, keepdims=True), jnp.max(sc, axis=0, keepdims=True)))
        m = jnp.maximum(m, sink)
        pw = jnp.concatenate([jnp.exp2(s0 - m), jnp.exp2(s1 - m), jnp.exp2(s2 - m)], axis=0).astype(BF16)
        return a, qs, m, pw, jnp.exp2(sc - m).astype(BF16)

    def values(a, qs, m, pw, pc):
        for h in range(WIN_KV_HEADS):
            cs = slice(h * w, (h + 1) * w)
            oe = _dot(v_ext[h, :, pl.ds(qs, span)], pw[:, cs]) + _dot(v_ext[h, :, c0:c0 + n_ctx], pc[:, cs])
            den = oe[HEAD_DIM:HEAD_DIM + 1] + jnp.exp2(sink[:, cs] - m[:, cs])
            _store_heads(o_ref, a * sub, h, group, oe, den)

    nxt = scores(0)
    pend = None
    for a in range(n_sub):
        cur = nxt
        if a + 1 < n_sub:
            nxt = scores(a + 1)
        if pend is not None:
            values(*pend)
        pend = softmax(*cur)
    values(*pend)


def _window(px, vt, sink_row, tri, nb, seq, n_ctx, with_ctx_cols):
    ta = px.shape[0] if with_ctx_cols else nb * seq
    assert seq % TQ_W == 0 and TQ_W % WINDOW == 0
    n_q = seq // TQ_W
    ctx_rows = (nb * seq) // n_ctx
    n_keys = seq + 2 * WINDOW + n_ctx
    return pl.pallas_call(
        functools.partial(_win_kernel, seq=seq, n_ctx=n_ctx),
        grid=(nb, n_q),
        in_specs=[pl.BlockSpec((TQ_W, D_ATT), lambda b, i: (b * n_q + i, P_WQ // D_ATT)),
                  pl.BlockSpec((seq, N_KV), lambda b, i: (b, P_WK // N_KV)),
                  pl.BlockSpec((n_ctx, N_KV), lambda b, i: (ctx_rows + b, P_WK // N_KV)),
                  pl.BlockSpec((N_KV, seq), lambda b, i: (0, b)),
                  pl.BlockSpec((N_KV, n_ctx), lambda b, i: (0, ctx_rows + b)),
                  pl.BlockSpec((1, WIN_Q_HEADS * WINDOW), lambda b, i: (0, 0)),
                  pl.BlockSpec((2, WINDOW, WIN_Q_HEADS * WINDOW), lambda b, i: (0, 0, 0))],
        out_specs=pl.BlockSpec((D_ATT, TQ_W), lambda b, i: (0, b * n_q + i)),
        out_shape=jax.ShapeDtypeStruct((D_ATT, ta), BF16),
        scratch_shapes=[pltpu.VMEM((n_keys, N_KV), BF16),
                        pltpu.VMEM((WIN_KV_HEADS, 2 * HEAD_DIM, n_keys), BF16)],
        compiler_params=_params(("parallel", "arbitrary")),
        name="window_attn",
    )(px, px, px, vt, vt, sink_row, tri)


def _ctx_kernel(qw_ref, kw_ref, qg_ref, kg_ref, vt_ref, sink_ref, yw_in, yg_in, yw_ref, yg_ref):
    del yw_in, yg_in
    n_ctx = qw_ref.shape[0]
    sub = WINDOW
    ones = jnp.ones((HEAD_DIM, n_ctx), BF16)
    for q_ref, k_ref, row0, o_ref, sink in ((qw_ref, kw_ref, 0, yw_ref, sink_ref[...]),
                                            (qg_ref, kg_ref, N_KV, yg_ref, None)):
        n_kv = k_ref.shape[1] // HEAD_DIM
        group = q_ref.shape[1] // HEAD_DIM // n_kv
        w = group * sub
        k = k_ref[...]
        for a in range(n_ctx // sub):
            sc = _dot_nt(k, _pad_heads(q_ref[a * sub:(a + 1) * sub, :], n_kv))
            m = jnp.max(sc, axis=0, keepdims=True)
            if sink is not None:
                m = jnp.maximum(m, sink)
            p = jnp.exp2(sc - m).astype(BF16)
            for h in range(n_kv):
                cs = slice(h * w, (h + 1) * w)
                ve = jnp.concatenate([vt_ref[row0 + h * HEAD_DIM:row0 + (h + 1) * HEAD_DIM, :], ones], axis=0)
                oe = _dot(ve, p[:, cs])
                den = oe[HEAD_DIM:HEAD_DIM + 1]
                if sink is not None:
                    den = den + jnp.exp2(sink[:, cs] - m[:, cs])
                _store_heads(o_ref, a * sub, h, group, oe, den)


def _ctx_attention(px, vt, sink_row, ywt, ygt, nb, seq, n_ctx):
    ctx_rows = (nb * seq) // n_ctx
    row = lambda col: (lambda b: (ctx_rows + b, col))
    any_spec = pl.BlockSpec(memory_space=pl.ANY)
    out_spec = pl.BlockSpec((D_ATT, n_ctx), lambda b: (0, ctx_rows + b))
    return pl.pallas_call(
        _ctx_kernel,
        grid=(nb,),
        in_specs=[pl.BlockSpec((n_ctx, D_ATT), row(P_WQ // D_ATT)),
                  pl.BlockSpec((n_ctx, N_KV), row(P_WK // N_KV)),
                  pl.BlockSpec((n_ctx, D_ATT), row(P_GQ // D_ATT)),
                  pl.BlockSpec((n_ctx, N_KV), row(P_GK // N_KV)),
                  pl.BlockSpec((2 * N_KV, n_ctx), lambda b: (0, ctx_rows + b)),
                  pl.BlockSpec((1, WIN_Q_HEADS * WINDOW), lambda b: (0, 0)),
                  any_spec, any_spec],
        out_specs=[out_spec, out_spec],
        out_shape=[jax.ShapeDtypeStruct(ywt.shape, BF16), jax.ShapeDtypeStruct(ygt.shape, BF16)],
        input_output_aliases={6: 0, 7: 1},
        compiler_params=_params(("parallel",)),
        name="ctx_attn",
    )(px, px, px, px, vt, sink_row, ywt, ygt)


def _glb_kernel(q_ref, k_ref, kc_ref, vt_ref, vtc_ref, o_ref, k_all, v_ext, *, seq, n_ctx):
    i = pl.program_id(1)
    group = GLB_Q_HEADS // GLB_KV_HEADS
    tq = q_ref.shape[0]
    w = group * tq

    @pl.when(i == 0)
    def _():
        k_all[0:seq, :] = k_ref[...]
        k_all[seq:seq + n_ctx, :] = kc_ref[...]
        _fill_values(v_ext, 0, vt_ref[...])
        _fill_values(v_ext, seq, vtc_ref[...])
        v_ext[:, HEAD_DIM:2 * HEAD_DIM, :] = jnp.ones((GLB_KV_HEADS, HEAD_DIM, seq + n_ctx), BF16)

    def attend(chunks):
        q4 = _pad_heads(q_ref[...], GLB_KV_HEADS)

        def scores(chunk):
            lo, nk = chunk
            return _dot_nt(k_all[lo:lo + nk, :], q4)

        m = None
        acc = [None] * GLB_KV_HEADS

        def accumulate(pend):
            pt, alpha, (lo, nk) = pend
            for h in range(GLB_KV_HEADS):
                pv = _dot(v_ext[h, :, lo:lo + nk], pt[:, h * w:(h + 1) * w])
                acc[h] = pv if alpha is None else acc[h] * alpha[:, h * w:(h + 1) * w] + pv

        st_next = scores(chunks[0])
        pend = None
        for ci, chunk in enumerate(chunks):
            st = st_next
            if ci + 1 < len(chunks):
                st_next = scores(chunks[ci + 1])
            if pend is not None:
                accumulate(pend)
            cm = jnp.max(st, axis=0, keepdims=True)
            m_new = cm if m is None else jnp.maximum(m, cm)
            pt = jnp.exp2(st - m_new).astype(BF16)
            alpha = None if m is None else jnp.exp2(m - m_new)
            pend = (pt, alpha, chunk)
            m = m_new
        accumulate(pend)
        for h in range(GLB_KV_HEADS):
            _store_heads(o_ref, 0, h, group, acc[h], acc[h][HEAD_DIM:HEAD_DIM + 1])

    attend([(c * KC_G, KC_G) for c in range(seq // KC_G)] + [(seq, n_ctx)])


def _global(px, vt, nb, seq, n_ctx, with_ctx_cols):
    ta = px.shape[0] if with_ctx_cols else nb * seq
    assert seq % KC_G == 0 and seq % TQ_G == 0
    n_q = seq // TQ_G
    ctx_rows = (nb * seq) // n_ctx
    return pl.pallas_call(
        functools.partial(_glb_kernel, seq=seq, n_ctx=n_ctx),
        grid=(nb, n_q),
        in_specs=[pl.BlockSpec((TQ_G, D_ATT), lambda b, i: (b * n_q + i, P_GQ // D_ATT)),
                  pl.BlockSpec((seq, N_KV), lambda b, i: (b, P_GK // N_KV)),
                  pl.BlockSpec((n_ctx, N_KV), lambda b, i: (ctx_rows + b, P_GK // N_KV)),
                  pl.BlockSpec((N_KV, seq), lambda b, i: (1, b)),
                  pl.BlockSpec((N_KV, n_ctx), lambda b, i: (1, ctx_rows + b))],
        out_specs=pl.BlockSpec((D_ATT, TQ_G), lambda b, i: (0, b * n_q + i)),
        out_shape=jax.ShapeDtypeStruct((D_ATT, ta), BF16),
        scratch_shapes=[pltpu.VMEM((seq + n_ctx, N_KV), BF16),
                        pltpu.VMEM((GLB_KV_HEADS, 2 * HEAD_DIM, seq + n_ctx), BF16)],
        compiler_params=_params(("parallel", "arbitrary")),
        name="global_attn",
    )(px, px, px, vt, vt)


def _route_rows(logits, bias):
    sig = jax.nn.sigmoid(logits)
    biased = sig + bias
    b_rows = [biased[e:e + 1, :] for e in range(N_EXPERTS)]
    s_rows = [sig[e:e + 1, :] for e in range(N_EXPERTS)]
    n_loc = EXPERTS_PER_GROUP

    best_score, grp = None, None
    for g in range(N_GROUPS):
        a = b_rows[g * n_loc:(g + 1) * n_loc]
        top2 = None
        for i in range(n_loc):
            for j in range(i + 1, n_loc):
                pair = a[i] + a[j]
                top2 = pair if top2 is None else jnp.maximum(top2, pair)
        if g == 0:
            best_score, grp = top2, jnp.zeros(top2.shape, jnp.int32)
        else:
            upd = top2 > best_score
            grp = jnp.where(upd, g, grp)
            best_score = jnp.where(upd, top2, best_score)

    def pick(rows, i):
        out = rows[i]
        for g in range(1, N_GROUPS):
            out = jnp.where(grp == g, rows[g * n_loc + i], out)
        return out

    cand = [pick(b_rows, i) for i in range(n_loc)]
    cand_s = [pick(s_rows, i) for i in range(n_loc)]
    m1, l1, w1 = cand[0], jnp.zeros(grp.shape, jnp.int32), cand_s[0]
    for i in range(1, n_loc):
        upd = cand[i] > m1
        m1 = jnp.where(upd, cand[i], m1)
        l1 = jnp.where(upd, i, l1)
        w1 = jnp.where(upd, cand_s[i], w1)
    m2, l2, w2 = None, None, None
    for i in range(n_loc):
        rest = jnp.where(l1 == i, -jnp.inf, cand[i])
        if i == 0:
            m2, l2, w2 = rest, jnp.zeros(grp.shape, jnp.int32), cand_s[0]
        else:
            upd = rest > m2
            m2 = jnp.where(upd, rest, m2)
            l2 = jnp.where(upd, i, l2)
            w2 = jnp.where(upd, cand_s[i], w2)
    tot = w1 + w2
    e_idx = jnp.concatenate([grp * n_loc + l1, grp * n_loc + l2], axis=0)
    gate = jnp.concatenate([w1 / tot, w2 / tot], axis=0)
    return e_idx, gate


def _outproj_kernel(of_ref, ob_ref, g_ref, ywt_ref, ygt_ref, x_ref, mod_ref, n2_ref, wf_ref, wr_ref, br_ref,
                    xo_ref, h_ref, e_ref, gate_ref, w_ref):
    @pl.when(pl.program_id(0) == 0)
    def _():
        w_ref[...] = wf_ref[...].astype(BF16)

    def ret_out(r):
        o = of_ref[r, :].astype(F32) + ob_ref[r, :].astype(F32)
        normed = []
        for h in range(RET_HEADS):
            oh = o[:, h * RET_DV:(h + 1) * RET_DV]
            mu = jnp.mean(oh, axis=-1, keepdims=True)
            var = jnp.mean(jnp.square(oh - mu), axis=-1, keepdims=True)
            normed.append((oh - mu) * lax.rsqrt(var + EPS))
        return (_silu(g_ref[r, :].astype(F32)) * jnp.concatenate(normed, axis=-1)).astype(BF16)

    def project(r, yr):
        acc = _dot_tn(ywt_ref[:, r], w_ref[D_RET:D_RET + D_ATT, :])
        acc += _dot_tn(ygt_ref[:, r], w_ref[D_RET + D_ATT:D_RET + 2 * D_ATT, :])
        return acc + _dot(yr, w_ref[0:D_RET, :])

    def residual_norm(r, acc):
        x = x_ref[r, :] + mod_ref[0, 2:3, :] * acc
        xo_ref[r, :] = x
        ms = jnp.mean(x * x, axis=-1, keepdims=True)
        h2 = x * lax.rsqrt(ms + EPS) * n2_ref[...]
        h2 = (h2 * (1.0 + mod_ref[0, 4:5, :]) + mod_ref[0, 3:4, :]).astype(BF16)
        h_ref[r, :] = h2
        return h2

    def route(r, h2):
        e_idx, gate = _route_rows(_dot_nt(wr_ref[...], h2), br_ref[...])
        e_ref[:, r] = e_idx
        gate_ref[:, r] = gate

    half = x_ref.shape[0] // 2
    ra, rb = slice(0, half), slice(half, 2 * half)
    yr_a = ret_out(ra)
    acc_a = project(ra, yr_a)
    yr_b = ret_out(rb)
    h2_a = residual_norm(ra, acc_a)
    acc_b = project(rb, yr_b)
    route(ra, h2_a)
    h2_b = residual_norm(rb, acc_b)
    route(rb, h2_b)


def _outproj(o_f, o_b, px, yw, yg, xa, mod_l, n2, w_out, layer, wr_t, b_r, nb, n_x_tiles, tiles_per_seq, n_tiles):
    ta, d = xa.shape
    once = pl.Buffered(1)

    def mod_idx(i):
        return (jnp.where(i < n_x_tiles, i // tiles_per_seq, nb), 0, 0)

    row = lambda i: (i, 0)
    return pl.pallas_call(
        _outproj_kernel,
        grid=(n_tiles,),
        in_specs=[pl.BlockSpec((TM, D_RET), row),
                  pl.BlockSpec((TM, D_RET), row),
                  pl.BlockSpec((TM, D_RET), lambda i: (i, P_RG // D_RET)),
                  pl.BlockSpec((D_ATT, TM), lambda i: (0, i)),
                  pl.BlockSpec((D_ATT, TM), lambda i: (0, i)),
                  pl.BlockSpec((TM, d), row),
                  pl.BlockSpec((1, 6, d), mod_idx),
                  pl.BlockSpec((1, d), lambda i: (0, 0)),
                  pl.BlockSpec((None, d, d), lambda i: (layer, 0, 0), pipeline_mode=once),
                  pl.BlockSpec((N_EXPERTS, d), lambda i: (0, 0)),
                  pl.BlockSpec((N_EXPERTS, 1), lambda i: (0, 0))],
        out_specs=[pl.BlockSpec((TM, d), row),
                   pl.BlockSpec((TM, d), row),
                   pl.BlockSpec((TOP_K, TM), lambda i: (0, i)),
                   pl.BlockSpec((TOP_K, TM), lambda i: (0, i))],
        out_shape=[jax.ShapeDtypeStruct((ta, d), F32),
                   jax.ShapeDtypeStruct((n_tiles * TM, d), BF16),
                   jax.ShapeDtypeStruct((TOP_K, n_tiles * TM), jnp.int32),
                   jax.ShapeDtypeStruct((TOP_K, n_tiles * TM), F32)],
        scratch_shapes=[pltpu.VMEM((d, d), BF16)],
        input_output_aliases={5: 0},
        compiler_params=_params(("arbitrary",)),
        name="out_proj",
    )(o_f, o_b, px, yw, yg, xa, mod_l, n2, w_out, wr_t, b_r)


def _moe_kernel(be_ref, first_ref, nu_ref, x_ref, wgu_f_ref, wd_f_ref, o_ref, wgu_ref, wd_ref):
    i = pl.program_id(0)

    @pl.when(jnp.logical_and(i < nu_ref[0], first_ref[i] == 1))
    def _():
        wgu_ref[...] = wgu_f_ref[...].astype(BF16)
        wd_ref[...] = wd_f_ref[...].astype(BF16)

    @pl.when(i < nu_ref[0])
    def _():
        f = wd_ref.shape[0]
        au = _dot(x_ref[...], wgu_ref[...])
        mid = (_silu(au[:, :f]) * au[:, f:]).astype(BF16)
        o_ref[...] = _dot(mid, wd_ref[...]).astype(BF16)

    @pl.when(i >= nu_ref[0])
    def _():
        o_ref[...] = jnp.zeros_like(o_ref)


def _moe_ffn(blk_e, first, n_used, xs, w_gu, w_down, layer):
    rows, d = xs.shape
    f2 = w_gu.shape[3]
    f = w_down.shape[2]
    grid_spec = pltpu.PrefetchScalarGridSpec(
        num_scalar_prefetch=3,
        grid=(rows // TM_E,),
        in_specs=[pl.BlockSpec((TM_E, d), lambda i, be, fi, nu: (i, 0)),
                  pl.BlockSpec((None, None, d, f2), lambda i, be, fi, nu: (layer, be[i], 0, 0)),
                  pl.BlockSpec((None, None, f, d), lambda i, be, fi, nu: (layer, be[i], 0, 0))],
        out_specs=pl.BlockSpec((TM_E, d), lambda i, be, fi, nu: (i, 0)),
        scratch_shapes=[pltpu.VMEM((d, f2), BF16), pltpu.VMEM((f, d), BF16)],
    )
    return pl.pallas_call(
        _moe_kernel,
        grid_spec=grid_spec,
        out_shape=jax.ShapeDtypeStruct((rows, d), BF16),
        compiler_params=_params(("arbitrary",)),
        name="moe_ffn",
    )(blk_e, first, n_used, xs, w_gu, w_down)


def _moe(h2, e_idx_t, w_gu, w_down, layer, n_tok):
    n_asg = n_tok * TOP_K
    flat_e = e_idx_t[:, :n_tok].reshape(-1)
    onehot = (flat_e[:, None] == jnp.arange(N_EXPERTS, dtype=jnp.int32)[None, :]).astype(jnp.int32)
    csum = jnp.cumsum(onehot, axis=0)
    counts = csum[-1]
    rank = jnp.sum(csum * onehot, axis=1) - 1
    padded = (counts + TM_E - 1) // TM_E * TM_E
    pad_end = jnp.cumsum(padded)
    pad_start = pad_end - padded
    cnt_start = jnp.cumsum(counts) - counts
    pos = (jnp.sum(pad_start[None, :] * onehot, axis=1) + rank).reshape(TOP_K, n_tok)
    n_blocks = (n_asg + N_EXPERTS * (TM_E - 1) + TM_E - 1) // TM_E
    blk_start = jnp.arange(n_blocks, dtype=jnp.int32) * TM_E
    blk_e = jnp.minimum(jnp.sum(blk_start[:, None] >= pad_end[None, :], axis=1), N_EXPERTS - 1).astype(jnp.int32)
    first = jnp.concatenate([jnp.ones((1,), jnp.int32), (blk_e[1:] != blk_e[:-1]).astype(jnp.int32)])
    n_used = (pad_end[-1] // TM_E).astype(jnp.int32).reshape(1)
    order = jnp.argsort(flat_e)
    e_row = jnp.repeat(blk_e, TM_E)
    p = jnp.arange(n_blocks * TM_E, dtype=jnp.int32)
    r = p - pad_start[e_row]
    src = jnp.where(r < counts[e_row], order[jnp.clip(cnt_start[e_row] + r, 0, n_asg - 1)], p) % n_tok
    yp = _moe_ffn(blk_e, first, n_used, h2[src], w_gu, w_down, layer)
    return yp[pos[0]], yp[pos[1]]


def _combine_kernel(x_ref, y0_ref, y1_ref, gate_ref, mod_ref, o_ref):
    g = gate_ref[...]
    y = y0_ref[...].astype(F32) * g[:, 0:1] + y1_ref[...].astype(F32) * g[:, 1:2]
    o_ref[...] = x_ref[...] + mod_ref[0, 5:6, :] * y


def _combine_final_kernel(x_ref, y0_ref, y1_ref, gate_ref, mod_ref, prev_ref, o_ref):
    del prev_ref
    _combine_kernel(x_ref, y0_ref, y1_ref, gate_ref, mod_ref, o_ref)


def _combine_final(xa, y0, y1, gate, mod_l, tiles_per_seq, n_tiles, prev, tile_off, total_rows):
    d = xa.shape[1]
    row = lambda i: (i, 0)
    ins = [pl.BlockSpec((TM, d), row), pl.BlockSpec((TM, d), row), pl.BlockSpec((TM, d), row),
           pl.BlockSpec((TM, TOP_K), row), pl.BlockSpec((1, 6, d), lambda i: (i // tiles_per_seq, 0, 0))]
    args = [xa, y0, y1, gate, mod_l]
    if prev is not None:
        ins.append(pl.BlockSpec(memory_space=pl.ANY))
        args.append(prev)
    return pl.pallas_call(
        _combine_kernel if prev is None else _combine_final_kernel,
        grid=(n_tiles,),
        in_specs=ins,
        out_specs=pl.BlockSpec((TM, d), lambda i: (tile_off + i, 0)),
        out_shape=jax.ShapeDtypeStruct((total_rows, d), F32),
        input_output_aliases={} if prev is None else {5: 0},
        compiler_params=_params(("parallel",)),
        name="moe_combine_out",
    )(*args)


def _combine(xa, y0, y1, gate, mod_l, nb, n_x_tiles, tiles_per_seq, n_tiles, in_place):
    d = xa.shape[1]

    def mod_idx(i):
        return (jnp.where(i < n_x_tiles, i // tiles_per_seq, nb), 0, 0)

    row = lambda i: (i, 0)
    return pl.pallas_call(
        _combine_kernel,
        grid=(n_tiles,),
        in_specs=[pl.BlockSpec((TM, d), row),
                  pl.BlockSpec((TM, d), row),
                  pl.BlockSpec((TM, d), row),
                  pl.BlockSpec((TM, TOP_K), row),
                  pl.BlockSpec((1, 6, d), mod_idx)],
        out_specs=pl.BlockSpec((TM, d), row),
        out_shape=jax.ShapeDtypeStruct((xa.shape[0] if in_place else n_tiles * TM, d), F32),
        input_output_aliases={0: 0} if in_place else {},
        compiler_params=_params(("parallel",)),
        name="moe_combine",
    )(xa, y0, y1, gate, mod_l)


def _rope_tables(seq, dtype):
    rows = seq // GRID_W
    row = jnp.repeat(jnp.arange(rows), GRID_W).astype(jnp.float32)
    col = (jnp.arange(rows * GRID_W) % GRID_W).astype(jnp.float32)
    half = HEAD_DIM // 2
    inv = jnp.power(ROPE_BASE, -jnp.arange(0, half, 2, dtype=jnp.float32) / half)
    ang_r, ang_c = row[:, None] * inv, col[:, None] * inv
    cos_r, cos_c = jnp.cos(ang_r).astype(dtype), jnp.cos(ang_c).astype(dtype)
    sin_r, sin_c = jnp.sin(ang_r).astype(dtype), jnp.sin(ang_c).astype(dtype)
    cos_h = jnp.concatenate([cos_r, cos_r, cos_c, cos_c], -1)
    sin_h = jnp.concatenate([-sin_r, sin_r, -sin_c, sin_c], -1)
    cos_t = jnp.concatenate([jnp.tile(cos_h, (1, D_ATT // HEAD_DIM)), jnp.ones((TM, D_ATT), dtype)], 0)
    sin_t = jnp.concatenate([jnp.tile(sin_h, (1, D_ATT // HEAD_DIM)), jnp.zeros((TM, D_ATT), dtype)], 0)
    return cos_t, sin_t


def kernel(x, c, ctx, c_ctx, ada_w, ada_b, norm1, norm2, w_in, w_out, ret_decay, win_qk_gain, win_sink,
           glb_qk_gain, w_router, b_router, w_gate_up, w_down):
    nb_all, seq, d = x.shape
    n_ctx = ctx.shape[1]
    depth = ada_w.shape[0]
    n_streams = N_STREAMS if nb_all % N_STREAMS == 0 and (nb_all // N_STREAMS * n_ctx) % TM == 0 else 1
    nb = nb_all // n_streams
    n_x = nb * seq
    ta = n_x + nb * n_ctx
    assert seq % TM == 0 and (nb * n_ctx) % TM == 0 and seq % GRID_W == 0
    tiles_per_seq = seq // TM
    n_x_tiles = n_x // TM
    n_tiles = ta // TM

    mod_rows = (nb_all + 1 + 7) // 8 * 8
    cc = jnp.zeros((mod_rows, d), F32).at[:nb_all].set(c).at[nb_all].set(c_ctx)
    mod_all = _ada(cc, ada_w, ada_b).reshape(depth, mod_rows, 6, d)

    cos_t, sin_t = _rope_tables(seq, F32)
    head_of = jnp.arange(D_ATT) // HEAD_DIM
    hm = jnp.where(head_of[:, None] == head_of[None, :], 1.0 / HEAD_DIM, 0.0).astype(BF16)
    wr_t = w_router.T.astype(BF16)
    b_r = b_router.astype(F32)[:, None]
    key_j = jnp.arange(WINDOW)[:, None]
    qry_i = jnp.tile(jnp.arange(WINDOW), WIN_Q_HEADS)[None, :]
    tri = jnp.stack([jnp.where(key_j >= qry_i, 0.0, NEG_INF), jnp.where(key_j <= qry_i, 0.0, NEG_INF)]).astype(F32)

    layer_tabs = []
    for l in range(depth):
        gains = jnp.zeros((8, D_ATT), F32)
        gains = gains.at[0].set(jnp.tile(win_qk_gain[l, 0], WIN_Q_HEADS))
        gains = gains.at[1, :WIN_KV_HEADS * HEAD_DIM].set(jnp.tile(win_qk_gain[l, 1], WIN_KV_HEADS))
        gains = gains.at[2].set(jnp.tile(glb_qk_gain[l, 0], GLB_Q_HEADS))
        gains = gains.at[3, :GLB_KV_HEADS * HEAD_DIM].set(jnp.tile(glb_qk_gain[l, 1], GLB_KV_HEADS))
        sink_row = jnp.repeat(win_sink[l].astype(F32) * LOG2_E, WINDOW)[None, :]
        layer_tabs.append((gains, sink_row, _ret_tables(ret_decay[l])))

    def mix(l, s, xa):
        last = l == depth - 1
        gains, sink_row, (gc, dmat, xi, zeta) = layer_tabs[l]
        px, vt = _inproj(xa, mods[s][l], norm1[l][None, :], w_in, l, cos_t, sin_t, gains, hm,
                         nb, n_x_tiles, tiles_per_seq)
        o_f, o_b = _retention(px, gc, dmat, xi, zeta, nb, seq, n_ctx)
        yw = _window(px, vt, sink_row, tri, nb, seq, n_ctx, not last)
        yg = _global(px, vt, nb, seq, n_ctx, not last)
        if not last:
            yw, yg = _ctx_attention(px, vt, sink_row, yw, yg, nb, seq, n_ctx)
        return _outproj(o_f, o_b, px, yw, yg, xa, mods[s][l], norm2[l][None, :], w_out, l,
                        wr_t, b_r, nb, n_x_tiles, tiles_per_seq, n_x_tiles if last else n_tiles)

    def experts(l, s, state, out):
        last = l == depth - 1
        xa, h2, e_idx_t, gate_t = state
        y0, y1 = _moe(h2, e_idx_t, w_gate_up, w_down, l, n_x if last else ta)
        if last:
            return _combine_final(xa, y0, y1, gate_t.T, mods[s][l], tiles_per_seq, n_x_tiles, out,
                                  s * n_x_tiles, n_streams * n_x)
        return _combine(xa, y0, y1, gate_t.T, mods[s][l], nb, n_x_tiles, tiles_per_seq, n_tiles, True)

    mods, xas = [], []
    for s in range(n_streams):
        b0 = s * nb
        mods.append(jnp.concatenate([mod_all[:, b0:b0 + nb], mod_all[:, nb_all:nb_all + 1]], axis=1))
        xas.append(jnp.concatenate([x[b0:b0 + nb].reshape(n_x, d), ctx[b0:b0 + nb].reshape(nb * n_ctx, d)], 0))

    out = None
    states = [None] * n_streams
    for l in range(depth):
        for s in range(n_streams):
            if l > 0:
                xas[s] = experts(l - 1, s, states[s], None)
            states[s] = mix(l, s, xas[s])
    for s in range(n_streams):
        out = experts(depth - 1, s, states[s], out)
    return out.reshape(nb_all, seq, d)
```

```python
import functools

import jax
import jax.numpy as jnp
from jax import lax
from jax.experimental import pallas as pl
from jax.experimental.pallas import tpu as pltpu

GRID_W = 64
HEAD_DIM = 64
RET_HEADS = 4
RET_DK = 64
RET_DV = 128
WIN_Q_HEADS = 4
WIN_KV_HEADS = 2
WINDOW = 128
GLB_Q_HEADS = 4
GLB_KV_HEADS = 2
ROPE_BASE = 10000.0
N_EXPERTS = 16
N_GROUPS = 4
EXPERTS_PER_GROUP = N_EXPERTS // N_GROUPS
TOP_K = 2
EPS = 1e-6
NEG_INF = -1e30

RQ, RK, RV, RG = 0, 256, 512, 1024
WQ, WK, WV = 1536, 1792, 1920
GQ, GK, GV = 2048, 2304, 2432
D_IN = 2560
D_RET = RET_HEADS * RET_DV
D_ATT = WIN_Q_HEADS * HEAD_DIM
N_KV = WIN_KV_HEADS * HEAD_DIM
P_RQ, P_RK, P_RV, P_RG = 0, 256, 512, 1024
P_WQ, P_GQ, P_WK, P_GK = 1536, 1792, 2048, 2176
PX_W = 2304

TM = 512
RET_C = 256
TQ_W = 512
TQ_G = 256
KC_G = 256
LOG2_E = 1.4426950408889634
assert GLB_KV_HEADS == 2
TM_E = 512
N_STREAMS = 2
VMEM_LIMIT = 56 * 1024 * 1024

F32 = jnp.float32
BF16 = jnp.bfloat16


def _dot(a, b):
    return jnp.dot(a, b, preferred_element_type=F32)


def _dot_nt(a, b):
    return lax.dot_general(a, b, (((1,), (1,)), ((), ())), preferred_element_type=F32)


def _dot_tn(a, b):
    return lax.dot_general(a, b, (((0,), (0,)), ((), ())), preferred_element_type=F32)


def _silu(x):
    return x * jax.nn.sigmoid(x)


def _params(sem):
    return pltpu.CompilerParams(dimension_semantics=sem, vmem_limit_bytes=VMEM_LIMIT)


def _ada_kernel(c_ref, w_ref, b_ref, o_ref):
    s = _silu(c_ref[...]).astype(BF16)
    o_ref[0] = _dot(s, w_ref[0].astype(BF16)) + b_ref[0]


def _ada(cc, ada_w, ada_b):
    depth, d, n = ada_w.shape
    tn = 1536
    rows = cc.shape[0]
    return pl.pallas_call(
        _ada_kernel,
        grid=(depth, n // tn),
        in_specs=[pl.BlockSpec((rows, d), lambda l, j: (0, 0)),
                  pl.BlockSpec((1, d, tn), lambda l, j: (l, 0, j)),
                  pl.BlockSpec((1, 1, tn), lambda l, j: (l, 0, j))],
        out_specs=pl.BlockSpec((1, rows, tn), lambda l, j: (l, 0, j)),
        out_shape=jax.ShapeDtypeStruct((depth, rows, n), F32),
        compiler_params=_params(("arbitrary", "arbitrary")),
        name="ada_mod",
    )(cc, ada_w, ada_b.reshape(depth, 1, n))


def _inproj_fused_kernel(x_ref, y0_ref, y1_ref, gate_ref, modp_ref, mod_ref, n1_ref, wf_ref, wvt_ref, cos_ref,
                         sin_ref, gains_ref, hm_ref, xo_ref, o_ref, vt_ref, w_ref):
    g = gate_ref[...]
    y = y0_ref[...].astype(F32) * g[:, 0:1] + y1_ref[...].astype(F32) * g[:, 1:2]
    x = x_ref[...] + modp_ref[0, 5:6, :] * y
    xo_ref[...] = x
    _inproj_body(x, mod_ref, n1_ref, wf_ref, wvt_ref, cos_ref, sin_ref, gains_ref, hm_ref, o_ref, vt_ref, w_ref)


def _inproj_kernel(x_ref, mod_ref, n1_ref, wf_ref, wvt_ref, cos_ref, sin_ref, gains_ref, hm_ref, o_ref, vt_ref,
                   w_ref):
    _inproj_body(x_ref[...], mod_ref, n1_ref, wf_ref, wvt_ref, cos_ref, sin_ref, gains_ref, hm_ref, o_ref, vt_ref,
                 w_ref)


def _inproj_body(x, mod_ref, n1_ref, wf_ref, wvt_ref, cos_ref, sin_ref, gains_ref, hm_ref, o_ref, vt_ref, w_ref):
    @pl.when(pl.program_id(0) == 0)
    def _():
        w_ref[...] = wf_ref[...].astype(BF16)

    ms = jnp.mean(x * x, axis=-1, keepdims=True)
    h = x * lax.rsqrt(ms + EPS) * n1_ref[...]
    h = h * (1.0 + mod_ref[0, 1:2, :]) + mod_ref[0, 0:1, :]
    hb = h.astype(BF16)

    def proj(lo, width):
        return _dot(hb, w_ref[:, lo:lo + width])

    def head_msq(y):
        sq = y * y
        sq_hi = sq.astype(BF16)
        sq_lo = (sq - sq_hi.astype(F32)).astype(BF16)
        hm = hm_ref[0:y.shape[1], 0:y.shape[1]]
        return _dot(sq_hi, hm) + _dot(sq_lo, hm)

    def qk_finish(y, msq, gain_row, scale):
        width = y.shape[1]
        yn = y * lax.rsqrt(msq + EPS) * gains_ref[gain_row:gain_row + 1, 0:width]
        nxt = pltpu.roll(yn, width - HEAD_DIM // 4, 1)
        prv = pltpu.roll(yn, HEAD_DIM // 4, 1)
        lane = lax.broadcasted_iota(jnp.int32, yn.shape, 1)
        partner = jnp.where((lane % (HEAD_DIM // 2)) < HEAD_DIM // 4, nxt, prv)
        yr = yn * cos_ref[:, 0:width] + partner * sin_ref[:, 0:width]
        return (yr * scale).astype(BF16)

    q_scale = HEAD_DIM ** -0.5 * LOG2_E
    qk_segs = ((WQ, D_ATT, 0, q_scale, P_WQ), (GQ, D_ATT, 2, q_scale, P_GQ),
               (WK, N_KV, 1, 1.0, P_WK), (GK, N_KV, 3, 1.0, P_GK))
    plain_segs = ((RQ, RK - RQ, 1.0, P_RQ), (RK, RV - RK, RET_DK ** -0.5, P_RK),
                  (RV, RG - RV, 1.0, P_RV), (RG, WQ - RG, 1.0, P_RG))
    ys = [proj(lo, width) for lo, width, _, _, _ in qk_segs]
    stats = [head_msq(y) for y in ys]
    vt_ref[...] = _dot_nt(wvt_ref[...], hb).astype(BF16)
    for (lo, width, scale, dst), (_, qwidth, gain_row, qscale, qdst), y, msq in zip(plain_segs, qk_segs, ys, stats):
        p = proj(lo, width)
        o_ref[:, dst:dst + width] = (p if scale == 1.0 else p * scale).astype(BF16)
        o_ref[:, qdst:qdst + qwidth] = qk_finish(y, msq, gain_row, qscale)


def _inproj(xa, mod_l, n1, w_in, layer, cos_t, sin_t, gains, hm, nb, n_x_tiles, tiles_per_seq, pending=None):
    ta, d = xa.shape
    n_tiles = ta // TM
    w_vt = jnp.concatenate([w_in[layer, :, WV:WV + N_KV], w_in[layer, :, GV:GV + N_KV]], axis=1).T.astype(BF16)
    n_gv = 2 * N_KV
    once = pl.Buffered(1)

    def mod_idx(i):
        return (jnp.where(i < n_x_tiles, i // tiles_per_seq, nb), 0, 0)

    def rope_idx(i):
        return (jnp.where(i < n_x_tiles, i % tiles_per_seq, tiles_per_seq), 0)

    row = lambda i: (i, 0)
    in_specs = [pl.BlockSpec((1, 6, d), mod_idx),
                pl.BlockSpec((1, d), lambda i: (0, 0)),
                pl.BlockSpec((None, d, D_IN), lambda i: (layer, 0, 0), pipeline_mode=once),
                pl.BlockSpec((n_gv, d), lambda i: (0, 0), pipeline_mode=once),
                pl.BlockSpec((TM, D_ATT), rope_idx),
                pl.BlockSpec((TM, D_ATT), rope_idx),
                pl.BlockSpec((8, D_ATT), lambda i: (0, 0)),
                pl.BlockSpec((D_ATT, D_ATT), lambda i: (0, 0))]
    args = [mod_l, n1, w_in, w_vt, cos_t, sin_t, gains, hm]
    out_specs = [pl.BlockSpec((TM, PX_W), row), pl.BlockSpec((n_gv, TM), lambda i: (0, i))]
    out_shape = [jax.ShapeDtypeStruct((ta, PX_W), BF16), jax.ShapeDtypeStruct((n_gv, ta), BF16)]
    if pending is None:
        kern, aliases = _inproj_kernel, {}
        in_specs = [pl.BlockSpec((TM, d), row)] + in_specs
        args = [xa] + args
    else:
        y0, y1, gate, mod_prev = pending
        kern, aliases = _inproj_fused_kernel, {0: 0}
        in_specs = [pl.BlockSpec((TM, d), row), pl.BlockSpec((TM, d), row), pl.BlockSpec((TM, d), row),
                    pl.BlockSpec((TM, TOP_K), row), pl.BlockSpec((1, 6, d), mod_idx)] + in_specs
        args = [xa, y0, y1, gate, mod_prev] + args
        out_specs = [pl.BlockSpec((TM, d), row)] + out_specs
        out_shape = [jax.ShapeDtypeStruct((ta, d), F32)] + out_shape
    res = pl.pallas_call(
        kern,
        grid=(n_tiles,),
        in_specs=in_specs,
        out_specs=out_specs,
        out_shape=out_shape,
        scratch_shapes=[pltpu.VMEM((d, D_IN), BF16)],
        input_output_aliases=aliases,
        compiler_params=_params(("arbitrary",)),
        name="in_proj",
    )(*args)
    return (xa,) + tuple(res) if pending is None else tuple(res)


def _ret_kernel(gc_ref, qf_ref, kf_ref, vf_ref, qb_ref, kb_ref, vb_ref, dmat_ref, xi_ref, zeta_ref,
                of_ref, ob_ref, s_ref):
    @pl.when(pl.program_id(1) == 0)
    def _():
        s_ref[...] = jnp.zeros_like(s_ref)

    dirs = ((qf_ref, kf_ref, vf_ref, of_ref), (qb_ref, kb_ref, vb_ref, ob_ref))
    for d, (q_ref, k_ref, v_ref, o_ref) in enumerate(dirs):
        q = q_ref[...]
        k = k_ref[...]
        v = v_ref[...]
        kz = (k.astype(F32) * zeta_ref[d]).astype(BF16)
        outs = []
        for h in range(RET_HEADS):
            i = d * RET_HEADS + h
            qh = q[:, h * RET_DK:(h + 1) * RET_DK]
            kh = k[:, h * RET_DK:(h + 1) * RET_DK]
            vh = v[:, h * RET_DV:(h + 1) * RET_DV]
            att = _dot_nt(qh, kh) * dmat_ref[i]
            state = s_ref[i]
            outs.append(_dot(att.astype(BF16), vh) + _dot(qh, state.astype(BF16)) * xi_ref[i])
            s_ref[i] = gc_ref[i] * state + _dot_tn(kz[:, h * RET_DK:(h + 1) * RET_DK], vh)
        o_ref[...] = jnp.concatenate(outs, axis=-1).astype(BF16)


def _retention(px, gc, dmat, xi, zeta, nb, seq, n_ctx):
    ta = px.shape[0]
    n_x = seq // RET_C
    n_c = n_ctx // RET_C
    ctx_base = nb * n_x
    steps = n_c + n_x

    def row_f(b, c):
        return jnp.where(c < n_c, ctx_base + b * n_c + c, b * n_x + (c - n_c))

    def row_b(b, c):
        return jnp.where(c < n_c, ctx_base + b * n_c + (n_c - 1 - c), b * n_x + (steps - 1 - c))

    def spec(width, col, row):
        return pl.BlockSpec((RET_C, width), lambda b, c: (row(b, c), col))

    const3 = lambda b, c: (0, 0, 0)
    return pl.pallas_call(
        _ret_kernel,
        grid=(nb, steps),
        in_specs=[pl.BlockSpec(memory_space=pltpu.SMEM),
                  spec(256, P_RQ // 256, row_f), spec(256, P_RK // 256, row_f), spec(D_RET, P_RV // D_RET, row_f),
                  spec(256, P_RQ // 256, row_b), spec(256, P_RK // 256, row_b), spec(D_RET, P_RV // D_RET, row_b),
                  pl.BlockSpec((2 * RET_HEADS, RET_C, RET_C), const3),
                  pl.BlockSpec((2 * RET_HEADS, RET_C, RET_DV), const3),
                  pl.BlockSpec((2, RET_C, RET_HEADS * RET_DK), const3)],
        out_specs=[spec(D_RET, 0, row_f), spec(D_RET, 0, row_b)],
        out_shape=[jax.ShapeDtypeStruct((ta, D_RET), BF16)] * 2,
        scratch_shapes=[pltpu.VMEM((2 * RET_HEADS, RET_DK, RET_DV), F32)],
        compiler_params=_params(("parallel", "arbitrary")),
        name="retention",
    )(gc, px, px, px, px, px, px, dmat, xi, zeta)


def _ret_tables(decay_logit):
    log_g = jax.nn.log_sigmoid(decay_logit.astype(F32)).reshape(2 * RET_HEADS)
    idx = jnp.arange(RET_C, dtype=F32)
    diff = idx[:, None] - idx[None, :]
    lg = log_g[:, None, None]
    d_fwd = jnp.where(diff >= 0, jnp.exp(jnp.maximum(diff, 0.0) * lg), 0.0)
    d_bwd = jnp.where(diff <= 0, jnp.exp(jnp.maximum(-diff, 0.0) * lg), 0.0)
    is_bwd = (jnp.arange(2 * RET_HEADS) >= RET_HEADS)[:, None, None]
    dmat = jnp.where(is_bwd, d_bwd, d_fwd)
    pos = jnp.where(is_bwd[:, :, 0], RET_C - 1.0 - idx[None, :], idx[None, :])
    xi = jnp.exp((pos + 1.0) * log_g[:, None])
    zeta = jnp.exp((RET_C - 1.0 - pos) * log_g[:, None])
    gc = jnp.exp(RET_C * log_g)
    xi = jnp.broadcast_to(xi[:, :, None], (2 * RET_HEADS, RET_C, RET_DV))
    zeta = jnp.repeat(zeta.reshape(2, RET_HEADS, RET_C).transpose(0, 2, 1), RET_DK, axis=-1)
    return gc, dmat, xi, zeta


def _pad_heads(q, n_kv):
    rows, width = q.shape
    group = width // HEAD_DIM // n_kv
    zeros = jnp.zeros((rows, HEAD_DIM), BF16)
    out = []
    for h in range(n_kv):
        for g in range(group):
            qh = q[:, (h * group + g) * HEAD_DIM:(h * group + g + 1) * HEAD_DIM]
            out.append(jnp.concatenate([qh if hh == h else zeros for hh in range(n_kv)], axis=1))
    return jnp.concatenate(out, axis=0)


def _fill_values(v_ext, lo, vt):
    n = vt.shape[1]
    for h in range(v_ext.shape[0]):
        v_ext[h, 0:HEAD_DIM, lo:lo + n] = vt[h * HEAD_DIM:(h + 1) * HEAD_DIM, :]


def _store_heads(o_ref, col, h, group, oe, den):
    n = oe.shape[1] // group
    o = oe[0:HEAD_DIM] / den
    for g in range(group):
        r0 = (h * group + g) * HEAD_DIM
        o_ref[r0:r0 + HEAD_DIM, col:col + n] = o[:, g * n:(g + 1) * n].astype(BF16)


def _win_kernel(q_ref, k_ref, kc_ref, vt_ref, vtc_ref, sink_ref, tri_ref, o_ref, k_pad, v_ext, *, seq, n_ctx):
    i = pl.program_id(1)
    group = WIN_Q_HEADS // WIN_KV_HEADS
    sub = WINDOW
    w = group * sub
    span = 3 * WINDOW
    c0 = seq + 2 * WINDOW
    n_sub = q_ref.shape[0] // sub

    @pl.when(i == 0)
    def _():
        zk = jnp.zeros((WINDOW, N_KV), BF16)
        k_pad[0:WINDOW, :] = zk
        k_pad[WINDOW:WINDOW + seq, :] = k_ref[...]
        k_pad[WINDOW + seq:c0, :] = zk
        k_pad[c0:c0 + n_ctx, :] = kc_ref[...]
        v_ext[:, 0:HEAD_DIM, 0:WINDOW] = jnp.zeros((WIN_KV_HEADS, HEAD_DIM, WINDOW), BF16)
        v_ext[:, 0:HEAD_DIM, WINDOW + seq:c0] = jnp.zeros((WIN_KV_HEADS, HEAD_DIM, WINDOW), BF16)
        _fill_values(v_ext, WINDOW, vt_ref[...])
        _fill_values(v_ext, c0, vtc_ref[...])
        v_ext[:, HEAD_DIM:2 * HEAD_DIM, :] = jnp.ones((WIN_KV_HEADS, HEAD_DIM, c0 + n_ctx), BF16)

    sink = sink_ref[...]
    kc = k_pad[c0:c0 + n_ctx, :]
    start = i * q_ref.shape[0]

    def scores(a):
        qs = pl.multiple_of(start + a * sub, sub)
        q4 = _pad_heads(q_ref[a * sub:(a + 1) * sub, :], WIN_KV_HEADS)
        return a, qs, _dot_nt(k_pad[pl.ds(qs, span), :], q4), _dot_nt(kc, q4)

    def softmax(a, qs, st, sc):
        lo_edge = jnp.where(qs == 0, NEG_INF, 0.0)
        hi_edge = jnp.where(qs + sub == seq, NEG_INF, 0.0)
        s0 = st[0:WINDOW] + (tri_ref[0] + lo_edge)
        s1 = st[WINDOW:2 * WINDOW]
        s2 = st[2 * WINDOW:span] + (tri_ref[1] + hi_edge)
        m = jnp.maximum(jnp.maximum(jnp.max(s0, axis=0, keepdims=True), jnp.max(s1, axis=0, keepdims=True)),
                        jnp.maximum(jnp.max(s2, axis=0, keepdims=True), jnp.max(sc, axis=0, keepdims=True)))
        m = jnp.maximum(m, sink)
        pw = jnp.concatenate([jnp.exp2(s0 - m), jnp.exp2(s1 - m), jnp.exp2(s2 - m)], axis=0).astype(BF16)
        return a, qs, m, pw, jnp.exp2(sc - m).astype(BF16)

    def values(a, qs, m, pw, pc):
        for h in range(WIN_KV_HEADS):
            cs = slice(h * w, (h + 1) * w)
            oe = _dot(v_ext[h, :, pl.ds(qs, span)], pw[:, cs]) + _dot(v_ext[h, :, c0:c0 + n_ctx], pc[:, cs])
            den = oe[HEAD_DIM:HEAD_DIM + 1] + jnp.exp2(sink[:, cs] - m[:, cs])
            _store_heads(o_ref, a * sub, h, group, oe, den)

    nxt = scores(0)
    pend = None
    for a in range(n_sub):
        cur = nxt
        if a + 1 < n_sub:
            nxt = scores(a + 1)
        if pend is not None:
            values(*pend)
        pend = softmax(*cur)
    values(*pend)


def _window(px, vt, sink_row, tri, nb, seq, n_ctx, with_ctx_cols):
    ta = px.shape[0] if with_ctx_cols else nb * seq
    assert seq % TQ_W == 0 and TQ_W % WINDOW == 0
    n_q = seq // TQ_W
    ctx_rows = (nb * seq) // n_ctx
    n_keys = seq + 2 * WINDOW + n_ctx
    return pl.pallas_call(
        functools.partial(_win_kernel, seq=seq, n_ctx=n_ctx),
        grid=(nb, n_q),
        in_specs=[pl.BlockSpec((TQ_W, D_ATT), lambda b, i: (b * n_q + i, P_WQ // D_ATT)),
                  pl.BlockSpec((seq, N_KV), lambda b, i: (b, P_WK // N_KV)),
                  pl.BlockSpec((n_ctx, N_KV), lambda b, i: (ctx_rows + b, P_WK // N_KV)),
                  pl.BlockSpec((N_KV, seq), lambda b, i: (0, b)),
                  pl.BlockSpec((N_KV, n_ctx), lambda b, i: (0, ctx_rows + b)),
                  pl.BlockSpec((1, WIN_Q_HEADS * WINDOW), lambda b, i: (0, 0)),
                  pl.BlockSpec((2, WINDOW, WIN_Q_HEADS * WINDOW), lambda b, i: (0, 0, 0))],
        out_specs=pl.BlockSpec((D_ATT, TQ_W), lambda b, i: (0, b * n_q + i)),
        out_shape=jax.ShapeDtypeStruct((D_ATT, ta), BF16),
        scratch_shapes=[pltpu.VMEM((n_keys, N_KV), BF16),
                        pltpu.VMEM((WIN_KV_HEADS, 2 * HEAD_DIM, n_keys), BF16)],
        compiler_params=_params(("parallel", "arbitrary")),
        name="window_attn",
    )(px, px, px, vt, vt, sink_row, tri)


def _ctx_kernel(qw_ref, kw_ref, qg_ref, kg_ref, vt_ref, sink_ref, yw_in, yg_in, yw_ref, yg_ref):
    del yw_in, yg_in
    n_ctx = qw_ref.shape[0]
    sub = WINDOW
    ones = jnp.ones((HEAD_DIM, n_ctx), BF16)
    for q_ref, k_ref, row0, o_ref, sink in ((qw_ref, kw_ref, 0, yw_ref, sink_ref[...]),
                                            (qg_ref, kg_ref, N_KV, yg_ref, None)):
        n_kv = k_ref.shape[1] // HEAD_DIM
        group = q_ref.shape[1] // HEAD_DIM // n_kv
        w = group * sub
        k = k_ref[...]
        for a in range(n_ctx // sub):
            sc = _dot_nt(k, _pad_heads(q_ref[a * sub:(a + 1) * sub, :], n_kv))
            m = jnp.max(sc, axis=0, keepdims=True)
            if sink is not None:
                m = jnp.maximum(m, sink)
            p = jnp.exp2(sc - m).astype(BF16)
            for h in range(n_kv):
                cs = slice(h * w, (h + 1) * w)
                ve = jnp.concatenate([vt_ref[row0 + h * HEAD_DIM:row0 + (h + 1) * HEAD_DIM, :], ones], axis=0)
                oe = _dot(ve, p[:, cs])
                den = oe[HEAD_DIM:HEAD_DIM + 1]
                if sink is not None:
                    den = den + jnp.exp2(sink[:, cs] - m[:, cs])
                _store_heads(o_ref, a * sub, h, group, oe, den)


def _ctx_attention(px, vt, sink_row, ywt, ygt, nb, seq, n_ctx):
    ctx_rows = (nb * seq) // n_ctx
    row = lambda col: (lambda b: (ctx_rows + b, col))
    any_spec = pl.BlockSpec(memory_space=pl.ANY)
    out_spec = pl.BlockSpec((D_ATT, n_ctx), lambda b: (0, ctx_rows + b))
    return pl.pallas_call(
        _ctx_kernel,
        grid=(nb,),
        in_specs=[pl.BlockSpec((n_ctx, D_ATT), row(P_WQ // D_ATT)),
                  pl.BlockSpec((n_ctx, N_KV), row(P_WK // N_KV)),
                  pl.BlockSpec((n_ctx, D_ATT), row(P_GQ // D_ATT)),
                  pl.BlockSpec((n_ctx, N_KV), row(P_GK // N_KV)),
                  pl.BlockSpec((2 * N_KV, n_ctx), lambda b: (0, ctx_rows + b)),
                  pl.BlockSpec((1, WIN_Q_HEADS * WINDOW), lambda b: (0, 0)),
                  any_spec, any_spec],
        out_specs=[out_spec, out_spec],
        out_shape=[jax.ShapeDtypeStruct(ywt.shape, BF16), jax.ShapeDtypeStruct(ygt.shape, BF16)],
        input_output_aliases={6: 0, 7: 1},
        compiler_params=_params(("parallel",)),
        name="ctx_attn",
    )(px, px, px, px, vt, sink_row, ywt, ygt)


def _glb_kernel(q_ref, k_ref, kc_ref, vt_ref, vtc_ref, o_ref, k_all, v_ext, *, seq, n_ctx):
    i = pl.program_id(1)
    group = GLB_Q_HEADS // GLB_KV_HEADS
    tq = q_ref.shape[0]
    w = group * tq

    @pl.when(i == 0)
    def _():
        k_all[0:seq, :] = k_ref[...]
        k_all[seq:seq + n_ctx, :] = kc_ref[...]
        _fill_values(v_ext, 0, vt_ref[...])
        _fill_values(v_ext, seq, vtc_ref[...])
        v_ext[:, HEAD_DIM:2 * HEAD_DIM, :] = jnp.ones((GLB_KV_HEADS, HEAD_DIM, seq + n_ctx), BF16)

    def attend(chunks):
        q4 = _pad_heads(q_ref[...], GLB_KV_HEADS)

        def scores(chunk):
            lo, nk = chunk
            return _dot_nt(k_all[lo:lo + nk, :], q4)

        m = None
        acc = [None] * GLB_KV_HEADS

        def accumulate(pend):
            pt, alpha, (lo, nk) = pend
            for h in range(GLB_KV_HEADS):
                pv = _dot(v_ext[h, :, lo:lo + nk], pt[:, h * w:(h + 1) * w])
                acc[h] = pv if alpha is None else acc[h] * alpha[:, h * w:(h + 1) * w] + pv

        st_next = scores(chunks[0])
        pend = None
        for ci, chunk in enumerate(chunks):
            st = st_next
            if ci + 1 < len(chunks):
                st_next = scores(chunks[ci + 1])
            if pend is not None:
                accumulate(pend)
            cm = jnp.max(st, axis=0, keepdims=True)
            m_new = cm if m is None else jnp.maximum(m, cm)
            pt = jnp.exp2(st - m_new).astype(BF16)
            alpha = None if m is None else jnp.exp2(m - m_new)
            pend = (pt, alpha, chunk)
            m = m_new
        accumulate(pend)
        for h in range(GLB_KV_HEADS):
            _store_heads(o_ref, 0, h, group, acc[h], acc[h][HEAD_DIM:HEAD_DIM + 1])

    attend([(c * KC_G, KC_G) for c in range(seq // KC_G)] + [(seq, n_ctx)])


def _global(px, vt, nb, seq, n_ctx, with_ctx_cols):
    ta = px.shape[0] if with_ctx_cols else nb * seq
    assert seq % KC_G == 0 and seq % TQ_G == 0
    n_q = seq // TQ_G
    ctx_rows = (nb * seq) // n_ctx
    return pl.pallas_call(
        functools.partial(_glb_kernel, seq=seq, n_ctx=n_ctx),
        grid=(nb, n_q),
        in_specs=[pl.BlockSpec((TQ_G, D_ATT), lambda b, i: (b * n_q + i, P_GQ // D_ATT)),
                  pl.BlockSpec((seq, N_KV), lambda b, i: (b, P_GK // N_KV)),
                  pl.BlockSpec((n_ctx, N_KV), lambda b, i: (ctx_rows + b, P_GK // N_KV)),
                  pl.BlockSpec((N_KV, seq), lambda b, i: (1, b)),
                  pl.BlockSpec((N_KV, n_ctx), lambda b, i: (1, ctx_rows + b))],
        out_specs=pl.BlockSpec((D_ATT, TQ_G), lambda b, i: (0, b * n_q + i)),
        out_shape=jax.ShapeDtypeStruct((D_ATT, ta), BF16),
        scratch_shapes=[pltpu.VMEM((seq + n_ctx, N_KV), BF16),
                        pltpu.VMEM((GLB_KV_HEADS, 2 * HEAD_DIM, seq + n_ctx), BF16)],
        compiler_params=_params(("parallel", "arbitrary")),
        name="global_attn",
    )(px, px, px, vt, vt)


def _route_rows(logits, bias):
    sig = jax.nn.sigmoid(logits)
    biased = sig + bias
    b_rows = [biased[e:e + 1, :] for e in range(N_EXPERTS)]
    s_rows = [sig[e:e + 1, :] for e in range(N_EXPERTS)]
    n_loc = EXPERTS_PER_GROUP

    best_score, grp = None, None
    for g in range(N_GROUPS):
        a = b_rows[g * n_loc:(g + 1) * n_loc]
        top2 = None
        for i in range(n_loc):
            for j in range(i + 1, n_loc):
                pair = a[i] + a[j]
                top2 = pair if top2 is None else jnp.maximum(top2, pair)
        if g == 0:
            best_score, grp = top2, jnp.zeros(top2.shape, jnp.int32)
        else:
            upd = top2 > best_score
            grp = jnp.where(upd, g, grp)
            best_score = jnp.where(upd, top2, best_score)

    def pick(rows, i):
        out = rows[i]
        for g in range(1, N_GROUPS):
            out = jnp.where(grp == g, rows[g * n_loc + i], out)
        return out

    cand = [pick(b_rows, i) for i in range(n_loc)]
    cand_s = [pick(s_rows, i) for i in range(n_loc)]
    m1, l1, w1 = cand[0], jnp.zeros(grp.shape, jnp.int32), cand_s[0]
    for i in range(1, n_loc):
        upd = cand[i] > m1
        m1 = jnp.where(upd, cand[i], m1)
        l1 = jnp.where(upd, i, l1)
        w1 = jnp.where(upd, cand_s[i], w1)
    m2, l2, w2 = None, None, None
    for i in range(n_loc):
        rest = jnp.where(l1 == i, -jnp.inf, cand[i])
        if i == 0:
            m2, l2, w2 = rest, jnp.zeros(grp.shape, jnp.int32), cand_s[0]
        else:
            upd = rest > m2
            m2 = jnp.where(upd, rest, m2)
            l2 = jnp.where(upd, i, l2)
            w2 = jnp.where(upd, cand_s[i], w2)
    tot = w1 + w2
    e_idx = jnp.concatenate([grp * n_loc + l1, grp * n_loc + l2], axis=0)
    gate = jnp.concatenate([w1 / tot, w2 / tot], axis=0)
    return e_idx, gate


def _outproj_kernel(of_ref, ob_ref, g_ref, ywt_ref, ygt_ref, x_ref, mod_ref, n2_ref, wf_ref, wr_ref, br_ref,
                    xo_ref, h_ref, e_ref, gate_ref, w_ref):
    @pl.when(pl.program_id(0) == 0)
    def _():
        w_ref[...] = wf_ref[...].astype(BF16)

    def ret_out(r):
        o = of_ref[r, :].astype(F32) + ob_ref[r, :].astype(F32)
        normed = []
        for h in range(RET_HEADS):
            oh = o[:, h * RET_DV:(h + 1) * RET_DV]
            mu = jnp.mean(oh, axis=-1, keepdims=True)
            var = jnp.mean(jnp.square(oh - mu), axis=-1, keepdims=True)
            normed.append((oh - mu) * lax.rsqrt(var + EPS))
        return (_silu(g_ref[r, :].astype(F32)) * jnp.concatenate(normed, axis=-1)).astype(BF16)

    def project(r, yr):
        acc = _dot_tn(ywt_ref[:, r], w_ref[D_RET:D_RET + D_ATT, :])
        acc += _dot_tn(ygt_ref[:, r], w_ref[D_RET + D_ATT:D_RET + 2 * D_ATT, :])
        return acc + _dot(yr, w_ref[0:D_RET, :])

    def residual_norm(r, acc):
        x = x_ref[r, :] + mod_ref[0, 2:3, :] * acc
        xo_ref[r, :] = x
        ms = jnp.mean(x * x, axis=-1, keepdims=True)
        h2 = x * lax.rsqrt(ms + EPS) * n2_ref[...]
        h2 = (h2 * (1.0 + mod_ref[0, 4:5, :]) + mod_ref[0, 3:4, :]).astype(BF16)
        h_ref[r, :] = h2
        return h2

    def route(r, h2):
        e_idx, gate = _route_rows(_dot_nt(wr_ref[...], h2), br_ref[...])
        e_ref[:, r] = e_idx
        gate_ref[:, r] = gate

    half = x_ref.shape[0] // 2
    ra, rb = slice(0, half), slice(half, 2 * half)
    yr_a = ret_out(ra)
    acc_a = project(ra, yr_a)
    yr_b = ret_out(rb)
    h2_a = residual_norm(ra, acc_a)
    acc_b = project(rb, yr_b)
    route(ra, h2_a)
    h2_b = residual_norm(rb, acc_b)
    route(rb, h2_b)


def _outproj(o_f, o_b, px, yw, yg, xa, mod_l, n2, w_out, layer, wr_t, b_r, nb, n_x_tiles, tiles_per_seq, n_tiles):
    ta, d = xa.shape
    once = pl.Buffered(1)

    def mod_idx(i):
        return (jnp.where(i < n_x_tiles, i // tiles_per_seq, nb), 0, 0)

    row = lambda i: (i, 0)
    return pl.pallas_call(
        _outproj_kernel,
        grid=(n_tiles,),
        in_specs=[pl.BlockSpec((TM, D_RET), row),
                  pl.BlockSpec((TM, D_RET), row),
                  pl.BlockSpec((TM, D_RET), lambda i: (i, P_RG // D_RET)),
                  pl.BlockSpec((D_ATT, TM), lambda i: (0, i)),
                  pl.BlockSpec((D_ATT, TM), lambda i: (0, i)),
                  pl.BlockSpec((TM, d), row),
                  pl.BlockSpec((1, 6, d), mod_idx),
                  pl.BlockSpec((1, d), lambda i: (0, 0)),
                  pl.BlockSpec((None, d, d), lambda i: (layer, 0, 0), pipeline_mode=once),
                  pl.BlockSpec((N_EXPERTS, d), lambda i: (0, 0)),
                  pl.BlockSpec((N_EXPERTS, 1), lambda i: (0, 0))],
        out_specs=[pl.BlockSpec((TM, d), row),
                   pl.BlockSpec((TM, d), row),
                   pl.BlockSpec((TOP_K, TM), lambda i: (0, i)),
                   pl.BlockSpec((TOP_K, TM), lambda i: (0, i))],
        out_shape=[jax.ShapeDtypeStruct((ta, d), F32),
                   jax.ShapeDtypeStruct((n_tiles * TM, d), BF16),
                   jax.ShapeDtypeStruct((TOP_K, n_tiles * TM), jnp.int32),
                   jax.ShapeDtypeStruct((TOP_K, n_tiles * TM), F32)],
        scratch_shapes=[pltpu.VMEM((d, d), BF16)],
        input_output_aliases={5: 0},
        compiler_params=_params(("arbitrary",)),
        name="out_proj",
    )(o_f, o_b, px, yw, yg, xa, mod_l, n2, w_out, wr_t, b_r)


def _moe_kernel(be_ref, first_ref, nu_ref, x_ref, wgu_f_ref, wd_f_ref, o_ref, wgu_ref, wd_ref):
    i = pl.program_id(0)

    @pl.when(jnp.logical_and(i < nu_ref[0], first_ref[i] == 1))
    def _():
        wgu_ref[...] = wgu_f_ref[...].astype(BF16)
        wd_ref[...] = wd_f_ref[...].astype(BF16)

    @pl.when(i < nu_ref[0])
    def _():
        f = wd_ref.shape[0]
        au = _dot(x_ref[...], wgu_ref[...])
        mid = (_silu(au[:, :f]) * au[:, f:]).astype(BF16)
        o_ref[...] = _dot(mid, wd_ref[...]).astype(BF16)

    @pl.when(i >= nu_ref[0])
    def _():
        o_ref[...] = jnp.zeros_like(o_ref)


def _moe_ffn(blk_e, first, n_used, xs, w_gu, w_down, layer):
    rows, d = xs.shape
    f2 = w_gu.shape[3]
    f = w_down.shape[2]
    grid_spec = pltpu.PrefetchScalarGridSpec(
        num_scalar_prefetch=3,
        grid=(rows // TM_E,),
        in_specs=[pl.BlockSpec((TM_E, d), lambda i, be, fi, nu: (i, 0)),
                  pl.BlockSpec((None, None, d, f2), lambda i, be, fi, nu: (layer, be[i], 0, 0)),
                  pl.BlockSpec((None, None, f, d), lambda i, be, fi, nu: (layer, be[i], 0, 0))],
        out_specs=pl.BlockSpec((TM_E, d), lambda i, be, fi, nu: (i, 0)),
        scratch_shapes=[pltpu.VMEM((d, f2), BF16), pltpu.VMEM((f, d), BF16)],
    )
    return pl.pallas_call(
        _moe_kernel,
        grid_spec=grid_spec,
        out_shape=jax.ShapeDtypeStruct((rows, d), BF16),
        compiler_params=_params(("arbitrary",)),
        name="moe_ffn",
    )(blk_e, first, n_used, xs, w_gu, w_down)


def _moe_plan(e_idx_t, n_tok):
    n_asg = n_tok * TOP_K
    flat_e = e_idx_t[:, :n_tok].reshape(-1)
    onehot = (flat_e[:, None] == jnp.arange(N_EXPERTS, dtype=jnp.int32)[None, :]).astype(jnp.int32)
    csum = jnp.cumsum(onehot, axis=0)
    counts = csum[-1]
    rank = jnp.sum(csum * onehot, axis=1) - 1
    padded = (counts + TM_E - 1) // TM_E * TM_E
    pad_end = jnp.cumsum(padded)
    pad_start = pad_end - padded
    cnt_start = jnp.cumsum(counts) - counts
    pos = (jnp.sum(pad_start[None, :] * onehot, axis=1) + rank).reshape(TOP_K, n_tok)
    n_blocks = (n_asg + N_EXPERTS * (TM_E - 1) + TM_E - 1) // TM_E
    blk_start = jnp.arange(n_blocks, dtype=jnp.int32) * TM_E
    blk_e = jnp.minimum(jnp.sum(blk_start[:, None] >= pad_end[None, :], axis=1), N_EXPERTS - 1).astype(jnp.int32)
    first = jnp.concatenate([jnp.ones((1,), jnp.int32), (blk_e[1:] != blk_e[:-1]).astype(jnp.int32)])
    n_used = (pad_end[-1] // TM_E).astype(jnp.int32).reshape(1)
    order = jnp.argsort(flat_e)
    e_row = jnp.repeat(blk_e, TM_E)
    p = jnp.arange(n_blocks * TM_E, dtype=jnp.int32)
    r = p - pad_start[e_row]
    src = jnp.where(r < counts[e_row], order[jnp.clip(cnt_start[e_row] + r, 0, n_asg - 1)], p) % n_tok
    return (blk_e, first, n_used), src, pos


def _combine_kernel(x_ref, y0_ref, y1_ref, gate_ref, mod_ref, o_ref):
    g = gate_ref[...]
    y = y0_ref[...].astype(F32) * g[:, 0:1] + y1_ref[...].astype(F32) * g[:, 1:2]
    o_ref[...] = x_ref[...] + mod_ref[0, 5:6, :] * y


def _combine_final_kernel(x_ref, y0_ref, y1_ref, gate_ref, mod_ref, prev_ref, o_ref):
    del prev_ref
    _combine_kernel(x_ref, y0_ref, y1_ref, gate_ref, mod_ref, o_ref)


def _combine_final(xa, y0, y1, gate, mod_l, tiles_per_seq, n_tiles, prev, tile_off, total_rows):
    d = xa.shape[1]
    row = lambda i: (i, 0)
    ins = [pl.BlockSpec((TM, d), row), pl.BlockSpec((TM, d), row), pl.BlockSpec((TM, d), row),
           pl.BlockSpec((TM, TOP_K), row), pl.BlockSpec((1, 6, d), lambda i: (i // tiles_per_seq, 0, 0))]
    args = [xa, y0, y1, gate, mod_l]
    if prev is not None:
        ins.append(pl.BlockSpec(memory_space=pl.ANY))
        args.append(prev)
    return pl.pallas_call(
        _combine_kernel if prev is None else _combine_final_kernel,
        grid=(n_tiles,),
        in_specs=ins,
        out_specs=pl.BlockSpec((TM, d), lambda i: (tile_off + i, 0)),
        out_shape=jax.ShapeDtypeStruct((total_rows, d), F32),
        input_output_aliases={} if prev is None else {5: 0},
        compiler_params=_params(("parallel",)),
        name="moe_combine_out",
    )(*args)


def _rope_tables(seq, dtype):
    rows = seq // GRID_W
    row = jnp.repeat(jnp.arange(rows), GRID_W).astype(jnp.float32)
    col = (jnp.arange(rows * GRID_W) % GRID_W).astype(jnp.float32)
    half = HEAD_DIM // 2
    inv = jnp.power(ROPE_BASE, -jnp.arange(0, half, 2, dtype=jnp.float32) / half)
    ang_r, ang_c = row[:, None] * inv, col[:, None] * inv
    cos_r, cos_c = jnp.cos(ang_r).astype(dtype), jnp.cos(ang_c).astype(dtype)
    sin_r, sin_c = jnp.sin(ang_r).astype(dtype), jnp.sin(ang_c).astype(dtype)
    cos_h = jnp.concatenate([cos_r, cos_r, cos_c, cos_c], -1)
    sin_h = jnp.concatenate([-sin_r, sin_r, -sin_c, sin_c], -1)
    cos_t = jnp.concatenate([jnp.tile(cos_h, (1, D_ATT // HEAD_DIM)), jnp.ones((TM, D_ATT), dtype)], 0)
    sin_t = jnp.concatenate([jnp.tile(sin_h, (1, D_ATT // HEAD_DIM)), jnp.zeros((TM, D_ATT), dtype)], 0)
    return cos_t, sin_t


def kernel(x, c, ctx, c_ctx, ada_w, ada_b, norm1, norm2, w_in, w_out, ret_decay, win_qk_gain, win_sink,
           glb_qk_gain, w_router, b_router, w_gate_up, w_down):
    nb_all, seq, d = x.shape
    n_ctx = ctx.shape[1]
    depth = ada_w.shape[0]
    n_streams = N_STREAMS if nb_all % N_STREAMS == 0 and (nb_all // N_STREAMS * n_ctx) % TM == 0 else 1
    nb = nb_all // n_streams
    n_x = nb * seq
    ta = n_x + nb * n_ctx
    assert seq % TM == 0 and (nb * n_ctx) % TM == 0 and seq % GRID_W == 0
    tiles_per_seq = seq // TM
    n_x_tiles = n_x // TM
    n_tiles = ta // TM

    mod_rows = (nb_all + 1 + 7) // 8 * 8
    cc = jnp.zeros((mod_rows, d), F32).at[:nb_all].set(c).at[nb_all].set(c_ctx)
    mod_all = _ada(cc, ada_w, ada_b).reshape(depth, mod_rows, 6, d)

    cos_t, sin_t = _rope_tables(seq, F32)
    head_of = jnp.arange(D_ATT) // HEAD_DIM
    hm = jnp.where(head_of[:, None] == head_of[None, :], 1.0 / HEAD_DIM, 0.0).astype(BF16)
    wr_t = w_router.T.astype(BF16)
    b_r = b_router.astype(F32)[:, None]
    key_j = jnp.arange(WINDOW)[:, None]
    qry_i = jnp.tile(jnp.arange(WINDOW), WIN_Q_HEADS)[None, :]
    tri = jnp.stack([jnp.where(key_j >= qry_i, 0.0, NEG_INF), jnp.where(key_j <= qry_i, 0.0, NEG_INF)]).astype(F32)

    layer_tabs = []
    for l in range(depth):
        gains = jnp.zeros((8, D_ATT), F32)
        gains = gains.at[0].set(jnp.tile(win_qk_gain[l, 0], WIN_Q_HEADS))
        gains = gains.at[1, :WIN_KV_HEADS * HEAD_DIM].set(jnp.tile(win_qk_gain[l, 1], WIN_KV_HEADS))
        gains = gains.at[2].set(jnp.tile(glb_qk_gain[l, 0], GLB_Q_HEADS))
        gains = gains.at[3, :GLB_KV_HEADS * HEAD_DIM].set(jnp.tile(glb_qk_gain[l, 1], GLB_KV_HEADS))
        sink_row = jnp.repeat(win_sink[l].astype(F32) * LOG2_E, WINDOW)[None, :]
        layer_tabs.append((gains, sink_row, _ret_tables(ret_decay[l])))

    def mix(l, s, xa, pending):
        last = l == depth - 1
        gains, sink_row, (gc, dmat, xi, zeta) = layer_tabs[l]
        xa, px, vt = _inproj(xa, mods[s][l], norm1[l][None, :], w_in, l, cos_t, sin_t, gains, hm,
                             nb, n_x_tiles, tiles_per_seq, pending)
        o_f, o_b = _retention(px, gc, dmat, xi, zeta, nb, seq, n_ctx)
        yw = _window(px, vt, sink_row, tri, nb, seq, n_ctx, not last)
        yg = _global(px, vt, nb, seq, n_ctx, not last)
        if not last:
            yw, yg = _ctx_attention(px, vt, sink_row, yw, yg, nb, seq, n_ctx)
        return _outproj(o_f, o_b, px, yw, yg, xa, mods[s][l], norm2[l][None, :], w_out, l,
                        wr_t, b_r, nb, n_x_tiles, tiles_per_seq, n_x_tiles if last else n_tiles)

    def after(value, token):
        return lax.optimization_barrier((value, token))[0]

    mods, xas = [], []
    for s in range(n_streams):
        b0 = s * nb
        mods.append(jnp.concatenate([mod_all[:, b0:b0 + nb], mod_all[:, nb_all:nb_all + 1]], axis=1))
        xas.append(jnp.concatenate([x[b0:b0 + nb].reshape(n_x, d), ctx[b0:b0 + nb].reshape(nb * n_ctx, d)], 0))

    out = None
    pending = [None] * n_streams
    last_yp = None
    for l in range(depth):
        last = l == depth - 1
        n_tok = n_x if last else ta
        states, plans = [], []
        for s in range(n_streams):
            xin, pend = xas[s], pending[s]
            if s > 0:
                xin = after(xin, plans[s - 1][1])
            elif last_yp is not None and n_streams > 1:
                xin = after(xin, last_yp)
            states.append(mix(l, s, xin, pend))
            plans.append(_moe_plan(states[s][2], n_tok))
        xs = [states[s][1][plans[s][1]] for s in range(n_streams)]
        for s in range(n_streams):
            xa, _, _, gate_t = states[s]
            tables, _, pos = plans[s]
            x_in = xs[s]
            if n_streams > 1:
                x_in = after(x_in, plans[-1][1] if s == 0 else last_yp)
            last_yp = _moe_ffn(*tables, x_in, w_gate_up, w_down, l)
            y0, y1 = last_yp[pos[0]], last_yp[pos[1]]
            if last:
                out = _combine_final(xa, y0, y1, gate_t.T, mods[s][l], tiles_per_seq, n_x_tiles, out,
                                     s * n_x_tiles, n_streams * n_x)
            else:
                xas[s], pending[s] = xa, (y0, y1, gate_t.T, mods[s][l])
    return out.reshape(nb_all, seq, d)
```

```python
import functools

import jax
import jax.numpy as jnp
from jax import lax
from jax.experimental import pallas as pl
from jax.experimental.pallas import tpu as pltpu

GRID_W = 64
HEAD_DIM = 64
RET_HEADS = 4
RET_DK = 64
RET_DV = 128
WIN_Q_HEADS = 4
WIN_KV_HEADS = 2
WINDOW = 128
GLB_Q_HEADS = 4
GLB_KV_HEADS = 2
ROPE_BASE = 10000.0
N_EXPERTS = 16
N_GROUPS = 4
EXPERTS_PER_GROUP = N_EXPERTS // N_GROUPS
TOP_K = 2
EPS = 1e-6
NEG_INF = -1e30

RQ, RK, RV, RG = 0, 256, 512, 1024
WQ, WK, WV = 1536, 1792, 1920
GQ, GK, GV = 2048, 2304, 2432
D_IN = 2560
D_RET = RET_HEADS * RET_DV
D_ATT = WIN_Q_HEADS * HEAD_DIM
N_KV = WIN_KV_HEADS * HEAD_DIM
P_RQ, P_RK, P_RV, P_RG = 0, 256, 512, 1024
P_WQ, P_GQ, P_WK, P_GK = 1536, 1792, 2048, 2176
PX_W = 2304

TM = 512
RET_C = 256
TQ_W = 512
TQ_G = 256
KC_G = 256
LOG2_E = 1.4426950408889634
assert GLB_KV_HEADS == 2
TM_E = 512
N_STREAMS = 2
VMEM_LIMIT = 56 * 1024 * 1024

F32 = jnp.float32
BF16 = jnp.bfloat16


def _dot(a, b):
    return jnp.dot(a, b, preferred_element_type=F32)


def _dot_nt(a, b):
    return lax.dot_general(a, b, (((1,), (1,)), ((), ())), preferred_element_type=F32)


def _dot_tn(a, b):
    return lax.dot_general(a, b, (((0,), (0,)), ((), ())), preferred_element_type=F32)


def _silu(x):
    return x * jax.nn.sigmoid(x)


def _params(sem):
    return pltpu.CompilerParams(dimension_semantics=sem, vmem_limit_bytes=VMEM_LIMIT)


def _ada_kernel(c_ref, w_ref, b_ref, o_ref):
    s = _silu(c_ref[...]).astype(BF16)
    o_ref[0] = _dot(s, w_ref[0].astype(BF16)) + b_ref[0]


def _ada(cc, ada_w, ada_b):
    depth, d, n = ada_w.shape
    tn = 1536
    rows = cc.shape[0]
    return pl.pallas_call(
        _ada_kernel,
        grid=(depth, n // tn),
        in_specs=[pl.BlockSpec((rows, d), lambda l, j: (0, 0)),
                  pl.BlockSpec((1, d, tn), lambda l, j: (l, 0, j)),
                  pl.BlockSpec((1, 1, tn), lambda l, j: (l, 0, j))],
        out_specs=pl.BlockSpec((1, rows, tn), lambda l, j: (l, 0, j)),
        out_shape=jax.ShapeDtypeStruct((depth, rows, n), F32),
        compiler_params=_params(("arbitrary", "arbitrary")),
        name="ada_mod",
    )(cc, ada_w, ada_b.reshape(depth, 1, n))


def _inproj_fused_kernel(x_ref, y0_ref, y1_ref, gate_ref, modp_ref, mod_ref, n1_ref, wf_ref, wvt_ref, cos_ref,
                         sin_ref, gains_ref, hm_ref, xo_ref, o_ref, vt_ref, w_ref):
    g = gate_ref[...]
    y = y0_ref[...].astype(F32) * g[:, 0:1] + y1_ref[...].astype(F32) * g[:, 1:2]
    x = x_ref[...] + modp_ref[0, 5:6, :] * y
    xo_ref[...] = x
    _inproj_body(x, mod_ref, n1_ref, wf_ref, wvt_ref, cos_ref, sin_ref, gains_ref, hm_ref, o_ref, vt_ref, w_ref)


def _inproj_kernel(x_ref, mod_ref, n1_ref, wf_ref, wvt_ref, cos_ref, sin_ref, gains_ref, hm_ref, o_ref, vt_ref,
                   w_ref):
    _inproj_body(x_ref[...], mod_ref, n1_ref, wf_ref, wvt_ref, cos_ref, sin_ref, gains_ref, hm_ref, o_ref, vt_ref,
                 w_ref)


def _inproj_body(x, mod_ref, n1_ref, wf_ref, wvt_ref, cos_ref, sin_ref, gains_ref, hm_ref, o_ref, vt_ref, w_ref):
    @pl.when(pl.program_id(0) == 0)
    def _():
        w_ref[...] = wf_ref[...].astype(BF16)

    ms = jnp.mean(x * x, axis=-1, keepdims=True)
    h = x * lax.rsqrt(ms + EPS) * n1_ref[...]
    h = h * (1.0 + mod_ref[0, 1:2, :]) + mod_ref[0, 0:1, :]
    hb = h.astype(BF16)

    def proj(lo, width):
        return _dot(hb, w_ref[:, lo:lo + width])

    def head_msq(y):
        sq = y * y
        sq_hi = sq.astype(BF16)
        sq_lo = (sq - sq_hi.astype(F32)).astype(BF16)
        hm = hm_ref[0:y.shape[1], 0:y.shape[1]]
        return _dot(sq_hi, hm) + _dot(sq_lo, hm)

    def qk_finish(y, msq, gain_row, scale):
        width = y.shape[1]
        yn = y * lax.rsqrt(msq + EPS) * gains_ref[gain_row:gain_row + 1, 0:width]
        nxt = pltpu.roll(yn, width - HEAD_DIM // 4, 1)
        prv = pltpu.roll(yn, HEAD_DIM // 4, 1)
        lane = lax.broadcasted_iota(jnp.int32, yn.shape, 1)
        partner = jnp.where((lane % (HEAD_DIM // 2)) < HEAD_DIM // 4, nxt, prv)
        yr = yn * cos_ref[:, 0:width] + partner * sin_ref[:, 0:width]
        return (yr * scale).astype(BF16)

    q_scale = HEAD_DIM ** -0.5 * LOG2_E
    qk_segs = ((WQ, D_ATT, 0, q_scale, P_WQ), (GQ, D_ATT, 2, q_scale, P_GQ),
               (WK, N_KV, 1, 1.0, P_WK), (GK, N_KV, 3, 1.0, P_GK))
    plain_segs = ((RQ, RK - RQ, 1.0, P_RQ), (RK, RV - RK, RET_DK ** -0.5, P_RK),
                  (RV, RG - RV, 1.0, P_RV), (RG, WQ - RG, 1.0, P_RG))
    ys = [proj(lo, width) for lo, width, _, _, _ in qk_segs]
    stats = [head_msq(y) for y in ys]
    vt_ref[...] = _dot_nt(wvt_ref[...], hb).astype(BF16)
    for (lo, width, scale, dst), (_, qwidth, gain_row, qscale, qdst), y, msq in zip(plain_segs, qk_segs, ys, stats):
        p = proj(lo, width)
        o_ref[:, dst:dst + width] = (p if scale == 1.0 else p * scale).astype(BF16)
        o_ref[:, qdst:qdst + qwidth] = qk_finish(y, msq, gain_row, qscale)


def _inproj(xa, mod_l, n1, w_in, layer, cos_t, sin_t, gains, hm, nb, n_x_tiles, tiles_per_seq, pending=None):
    ta, d = xa.shape
    n_tiles = ta // TM
    w_vt = jnp.concatenate([w_in[layer, :, WV:WV + N_KV], w_in[layer, :, GV:GV + N_KV]], axis=1).T.astype(BF16)
    n_gv = 2 * N_KV
    once = pl.Buffered(1)

    def mod_idx(i):
        return (jnp.where(i < n_x_tiles, i // tiles_per_seq, nb), 0, 0)

    def rope_idx(i):
        return (jnp.where(i < n_x_tiles, i % tiles_per_seq, tiles_per_seq), 0)

    row = lambda i: (i, 0)
    in_specs = [pl.BlockSpec((1, 6, d), mod_idx),
                pl.BlockSpec((1, d), lambda i: (0, 0)),
                pl.BlockSpec((None, d, D_IN), lambda i: (layer, 0, 0), pipeline_mode=once),
                pl.BlockSpec((n_gv, d), lambda i: (0, 0), pipeline_mode=once),
                pl.BlockSpec((TM, D_ATT), rope_idx),
                pl.BlockSpec((TM, D_ATT), rope_idx),
                pl.BlockSpec((8, D_ATT), lambda i: (0, 0)),
                pl.BlockSpec((D_ATT, D_ATT), lambda i: (0, 0))]
    args = [mod_l, n1, w_in, w_vt, cos_t, sin_t, gains, hm]
    out_specs = [pl.BlockSpec((TM, PX_W), row), pl.BlockSpec((n_gv, TM), lambda i: (0, i))]
    out_shape = [jax.ShapeDtypeStruct((ta, PX_W), BF16), jax.ShapeDtypeStruct((n_gv, ta), BF16)]
    if pending is None:
        kern, aliases = _inproj_kernel, {}
        in_specs = [pl.BlockSpec((TM, d), row)] + in_specs
        args = [xa] + args
    else:
        y0, y1, gate, mod_prev = pending
        kern, aliases = _inproj_fused_kernel, {0: 0}
        in_specs = [pl.BlockSpec((TM, d), row), pl.BlockSpec((TM, d), row), pl.BlockSpec((TM, d), row),
                    pl.BlockSpec((TM, TOP_K), row), pl.BlockSpec((1, 6, d), mod_idx)] + in_specs
        args = [xa, y0, y1, gate, mod_prev] + args
        out_specs = [pl.BlockSpec((TM, d), row)] + out_specs
        out_shape = [jax.ShapeDtypeStruct((ta, d), F32)] + out_shape
    res = pl.pallas_call(
        kern,
        grid=(n_tiles,),
        in_specs=in_specs,
        out_specs=out_specs,
        out_shape=out_shape,
        scratch_shapes=[pltpu.VMEM((d, D_IN), BF16)],
        input_output_aliases=aliases,
        compiler_params=_params(("arbitrary",)),
        name="in_proj",
    )(*args)
    return (xa,) + tuple(res) if pending is None else tuple(res)


def _ret_kernel(gc_ref, qf_ref, kf_ref, vf_ref, qb_ref, kb_ref, vb_ref, dmat_ref, xi_ref, zeta_ref,
                of_ref, ob_ref, s_ref):
    @pl.when(pl.program_id(1) == 0)
    def _():
        s_ref[...] = jnp.zeros_like(s_ref)

    dirs = ((qf_ref, kf_ref, vf_ref, of_ref), (qb_ref, kb_ref, vb_ref, ob_ref))
    for d, (q_ref, k_ref, v_ref, o_ref) in enumerate(dirs):
        q = q_ref[...]
        k = k_ref[...]
        v = v_ref[...]
        kz = (k.astype(F32) * zeta_ref[d]).astype(BF16)
        outs = []
        for h in range(RET_HEADS):
            i = d * RET_HEADS + h
            qh = q[:, h * RET_DK:(h + 1) * RET_DK]
            kh = k[:, h * RET_DK:(h + 1) * RET_DK]
            vh = v[:, h * RET_DV:(h + 1) * RET_DV]
            att = _dot_nt(qh, kh) * dmat_ref[i]
            state = s_ref[i]
            outs.append(_dot(att.astype(BF16), vh) + _dot(qh, state.astype(BF16)) * xi_ref[i])
            s_ref[i] = gc_ref[i] * state + _dot_tn(kz[:, h * RET_DK:(h + 1) * RET_DK], vh)
        o_ref[...] = jnp.concatenate(outs, axis=-1).astype(BF16)


def _retention(px, gc, dmat, xi, zeta, nb, seq, n_ctx):
    ta = px.shape[0]
    n_x = seq // RET_C
    n_c = n_ctx // RET_C
    ctx_base = nb * n_x
    steps = n_c + n_x

    def row_f(b, c):
        return jnp.where(c < n_c, ctx_base + b * n_c + c, b * n_x + (c - n_c))

    def row_b(b, c):
        return jnp.where(c < n_c, ctx_base + b * n_c + (n_c - 1 - c), b * n_x + (steps - 1 - c))

    def spec(width, col, row):
        return pl.BlockSpec((RET_C, width), lambda b, c: (row(b, c), col))

    const3 = lambda b, c: (0, 0, 0)
    return pl.pallas_call(
        _ret_kernel,
        grid=(nb, steps),
        in_specs=[pl.BlockSpec(memory_space=pltpu.SMEM),
                  spec(256, P_RQ // 256, row_f), spec(256, P_RK // 256, row_f), spec(D_RET, P_RV // D_RET, row_f),
                  spec(256, P_RQ // 256, row_b), spec(256, P_RK // 256, row_b), spec(D_RET, P_RV // D_RET, row_b),
                  pl.BlockSpec((2 * RET_HEADS, RET_C, RET_C), const3),
                  pl.BlockSpec((2 * RET_HEADS, RET_C, RET_DV), const3),
                  pl.BlockSpec((2, RET_C, RET_HEADS * RET_DK), const3)],
        out_specs=[spec(D_RET, 0, row_f), spec(D_RET, 0, row_b)],
        out_shape=[jax.ShapeDtypeStruct((ta, D_RET), BF16)] * 2,
        scratch_shapes=[pltpu.VMEM((2 * RET_HEADS, RET_DK, RET_DV), F32)],
        compiler_params=_params(("parallel", "arbitrary")),
        name="retention",
    )(gc, px, px, px, px, px, px, dmat, xi, zeta)


def _ret_tables(decay_logit):
    log_g = jax.nn.log_sigmoid(decay_logit.astype(F32)).reshape(2 * RET_HEADS)
    idx = jnp.arange(RET_C, dtype=F32)
    diff = idx[:, None] - idx[None, :]
    lg = log_g[:, None, None]
    d_fwd = jnp.where(diff >= 0, jnp.exp(jnp.maximum(diff, 0.0) * lg), 0.0)
    d_bwd = jnp.where(diff <= 0, jnp.exp(jnp.maximum(-diff, 0.0) * lg), 0.0)
    is_bwd = (jnp.arange(2 * RET_HEADS) >= RET_HEADS)[:, None, None]
    dmat = jnp.where(is_bwd, d_bwd, d_fwd)
    pos = jnp.where(is_bwd[:, :, 0], RET_C - 1.0 - idx[None, :], idx[None, :])
    xi = jnp.exp((pos + 1.0) * log_g[:, None])
    zeta = jnp.exp((RET_C - 1.0 - pos) * log_g[:, None])
    gc = jnp.exp(RET_C * log_g)
    xi = jnp.broadcast_to(xi[:, :, None], (2 * RET_HEADS, RET_C, RET_DV))
    zeta = jnp.repeat(zeta.reshape(2, RET_HEADS, RET_C).transpose(0, 2, 1), RET_DK, axis=-1)
    return gc, dmat, xi, zeta


def _pad_heads(q, n_kv):
    rows, width = q.shape
    group = width // HEAD_DIM // n_kv
    zeros = jnp.zeros((rows, HEAD_DIM), BF16)
    out = []
    for h in range(n_kv):
        for g in range(group):
            qh = q[:, (h * group + g) * HEAD_DIM:(h * group + g + 1) * HEAD_DIM]
            out.append(jnp.concatenate([qh if hh == h else zeros for hh in range(n_kv)], axis=1))
    return jnp.concatenate(out, axis=0)


def _fill_values(v_ext, lo, vt):
    n = vt.shape[1]
    for h in range(v_ext.shape[0]):
        v_ext[h, 0:HEAD_DIM, lo:lo + n] = vt[h * HEAD_DIM:(h + 1) * HEAD_DIM, :]


def _store_heads(o_ref, col, h, group, oe, den):
    n = oe.shape[1] // group
    o = oe[0:HEAD_DIM] / den
    for g in range(group):
        r0 = (h * group + g) * HEAD_DIM
        o_ref[r0:r0 + HEAD_DIM, col:col + n] = o[:, g * n:(g + 1) * n].astype(BF16)


def _win_kernel(q_ref, k_ref, kc_ref, vt_ref, vtc_ref, sink_ref, tri_ref, o_ref, k_pad, v_ext, *, seq, n_ctx):
    i = pl.program_id(1)
    group = WIN_Q_HEADS // WIN_KV_HEADS
    sub = WINDOW
    w = group * sub
    span = 3 * WINDOW
    c0 = seq + 2 * WINDOW
    n_sub = q_ref.shape[0] // sub

    @pl.when(i == 0)
    def _():
        zk = jnp.zeros((WINDOW, N_KV), BF16)
        k_pad[0:WINDOW, :] = zk
        k_pad[WINDOW:WINDOW + seq, :] = k_ref[...]
        k_pad[WINDOW + seq:c0, :] = zk
        k_pad[c0:c0 + n_ctx, :] = kc_ref[...]
        v_ext[:, 0:HEAD_DIM, 0:WINDOW] = jnp.zeros((WIN_KV_HEADS, HEAD_DIM, WINDOW), BF16)
        v_ext[:, 0:HEAD_DIM, WINDOW + seq:c0] = jnp.zeros((WIN_KV_HEADS, HEAD_DIM, WINDOW), BF16)
        _fill_values(v_ext, WINDOW, vt_ref[...])
        _fill_values(v_ext, c0, vtc_ref[...])
        v_ext[:, HEAD_DIM:2 * HEAD_DIM, :] = jnp.ones((WIN_KV_HEADS, HEAD_DIM, c0 + n_ctx), BF16)

    sink = sink_ref[...]
    kc = k_pad[c0:c0 + n_ctx, :]
    start = i * q_ref.shape[0]

    def scores(a):
        qs = pl.multiple_of(start + a * sub, sub)
        q4 = _pad_heads(q_ref[a * sub:(a + 1) * sub, :], WIN_KV_HEADS)
        return a, qs, _dot_nt(k_pad[pl.ds(qs, span), :], q4), _dot_nt(kc, q4)

    def softmax(a, qs, st, sc):
        lo_edge = jnp.where(qs == 0, NEG_INF, 0.0)
        hi_edge = jnp.where(qs + sub == seq, NEG_INF, 0.0)
        s0 = st[0:WINDOW] + (tri_ref[0] + lo_edge)
        s1 = st[WINDOW:2 * WINDOW]
        s2 = st[2 * WINDOW:span] + (tri_ref[1] + hi_edge)
        m = jnp.maximum(jnp.maximum(jnp.max(s0, axis=0, keepdims=True), jnp.max(s1, axis=0, keepdims=True)),
                        jnp.maximum(jnp.max(s2, axis=0, keepdims=True), jnp.max(sc, axis=0, keepdims=True)))
        m = jnp.maximum(m, sink)
        pw = jnp.concatenate([jnp.exp2(s0 - m), jnp.exp2(s1 - m), jnp.exp2(s2 - m)], axis=0).astype(BF16)
        return a, qs, m, pw, jnp.exp2(sc - m).astype(BF16)

    def values(a, qs, m, pw, pc):
        for h in range(WIN_KV_HEADS):
            cs = slice(h * w, (h + 1) * w)
            oe = _dot(v_ext[h, :, pl.ds(qs, span)], pw[:, cs]) + _dot(v_ext[h, :, c0:c0 + n_ctx], pc[:, cs])
            den = oe[HEAD_DIM:HEAD_DIM + 1] + jnp.exp2(sink[:, cs] - m[:, cs])
            _store_heads(o_ref, a * sub, h, group, oe, den)

    nxt = scores(0)
    pend = None
    for a in range(n_sub):
        cur = nxt
        if a + 1 < n_sub:
            nxt = scores(a + 1)
        if pend is not None:
            values(*pend)
        pend = softmax(*cur)
    values(*pend)


def _window(px, vt, sink_row, tri, nb, seq, n_ctx, with_ctx_cols):
    ta = px.shape[0] if with_ctx_cols else nb * seq
    assert seq % TQ_W == 0 and TQ_W % WINDOW == 0
    n_q = seq // TQ_W
    ctx_rows = (nb * seq) // n_ctx
    n_keys = seq + 2 * WINDOW + n_ctx
    return pl.pallas_call(
        functools.partial(_win_kernel, seq=seq, n_ctx=n_ctx),
        grid=(nb, n_q),
        in_specs=[pl.BlockSpec((TQ_W, D_ATT), lambda b, i: (b * n_q + i, P_WQ // D_ATT)),
                  pl.BlockSpec((seq, N_KV), lambda b, i: (b, P_WK // N_KV)),
                  pl.BlockSpec((n_ctx, N_KV), lambda b, i: (ctx_rows + b, P_WK // N_KV)),
                  pl.BlockSpec((N_KV, seq), lambda b, i: (0, b)),
                  pl.BlockSpec((N_KV, n_ctx), lambda b, i: (0, ctx_rows + b)),
                  pl.BlockSpec((1, WIN_Q_HEADS * WINDOW), lambda b, i: (0, 0)),
                  pl.BlockSpec((2, WINDOW, WIN_Q_HEADS * WINDOW), lambda b, i: (0, 0, 0))],
        out_specs=pl.BlockSpec((D_ATT, TQ_W), lambda b, i: (0, b * n_q + i)),
        out_shape=jax.ShapeDtypeStruct((D_ATT, ta), BF16),
        scratch_shapes=[pltpu.VMEM((n_keys, N_KV), BF16),
                        pltpu.VMEM((WIN_KV_HEADS, 2 * HEAD_DIM, n_keys), BF16)],
        compiler_params=_params(("parallel", "arbitrary")),
        name="window_attn",
    )(px, px, px, vt, vt, sink_row, tri)


def _ctx_kernel(qw_ref, kw_ref, qg_ref, kg_ref, vt_ref, sink_ref, yw_in, yg_in, yw_ref, yg_ref):
    del yw_in, yg_in
    n_ctx = qw_ref.shape[0]
    sub = WINDOW
    ones = jnp.ones((HEAD_DIM, n_ctx), BF16)
    for q_ref, k_ref, row0, o_ref, sink in ((qw_ref, kw_ref, 0, yw_ref, sink_ref[...]),
                                            (qg_ref, kg_ref, N_KV, yg_ref, None)):
        n_kv = k_ref.shape[1] // HEAD_DIM
        group = q_ref.shape[1] // HEAD_DIM // n_kv
        w = group * sub
        k = k_ref[...]
        for a in range(n_ctx // sub):
            sc = _dot_nt(k, _pad_heads(q_ref[a * sub:(a + 1) * sub, :], n_kv))
            m = jnp.max(sc, axis=0, keepdims=True)
            if sink is not None:
                m = jnp.maximum(m, sink)
            p = jnp.exp2(sc - m).astype(BF16)
            for h in range(n_kv):
                cs = slice(h * w, (h + 1) * w)
                ve = jnp.concatenate([vt_ref[row0 + h * HEAD_DIM:row0 + (h + 1) * HEAD_DIM, :], ones], axis=0)
                oe = _dot(ve, p[:, cs])
                den = oe[HEAD_DIM:HEAD_DIM + 1]
                if sink is not None:
                    den = den + jnp.exp2(sink[:, cs] - m[:, cs])
                _store_heads(o_ref, a * sub, h, group, oe, den)


def _ctx_attention(px, vt, sink_row, ywt, ygt, nb, seq, n_ctx):
    ctx_rows = (nb * seq) // n_ctx
    row = lambda col: (lambda b: (ctx_rows + b, col))
    any_spec = pl.BlockSpec(memory_space=pl.ANY)
    out_spec = pl.BlockSpec((D_ATT, n_ctx), lambda b: (0, ctx_rows + b))
    return pl.pallas_call(
        _ctx_kernel,
        grid=(nb,),
        in_specs=[pl.BlockSpec((n_ctx, D_ATT), row(P_WQ // D_ATT)),
                  pl.BlockSpec((n_ctx, N_KV), row(P_WK // N_KV)),
                  pl.BlockSpec((n_ctx, D_ATT), row(P_GQ // D_ATT)),
                  pl.BlockSpec((n_ctx, N_KV), row(P_GK // N_KV)),
                  pl.BlockSpec((2 * N_KV, n_ctx), lambda b: (0, ctx_rows + b)),
                  pl.BlockSpec((1, WIN_Q_HEADS * WINDOW), lambda b: (0, 0)),
                  any_spec, any_spec],
        out_specs=[out_spec, out_spec],
        out_shape=[jax.ShapeDtypeStruct(ywt.shape, BF16), jax.ShapeDtypeStruct(ygt.shape, BF16)],
        input_output_aliases={6: 0, 7: 1},
        compiler_params=_params(("parallel",)),
        name="ctx_attn",
    )(px, px, px, px, vt, sink_row, ywt, ygt)


def _glb_kernel(q_ref, k_ref, kc_ref, vt_ref, vtc_ref, o_ref, k_all, v_ext, *, seq, n_ctx):
    i = pl.program_id(1)
    group = GLB_Q_HEADS // GLB_KV_HEADS
    tq = q_ref.shape[0]
    w = group * tq

    @pl.when(i == 0)
    def _():
        k_all[0:seq, :] = k_ref[...]
        k_all[seq:seq + n_ctx, :] = kc_ref[...]
        _fill_values(v_ext, 0, vt_ref[...])
        _fill_values(v_ext, seq, vtc_ref[...])
        v_ext[:, HEAD_DIM:2 * HEAD_DIM, :] = jnp.ones((GLB_KV_HEADS, HEAD_DIM, seq + n_ctx), BF16)

    def attend(chunks):
        q4 = _pad_heads(q_ref[...], GLB_KV_HEADS)

        def scores(chunk):
            lo, nk = chunk
            return _dot_nt(k_all[lo:lo + nk, :], q4)

        m = None
        acc = [None] * GLB_KV_HEADS

        def accumulate(pend):
            pt, alpha, (lo, nk) = pend
            for h in range(GLB_KV_HEADS):
                pv = _dot(v_ext[h, :, lo:lo + nk], pt[:, h * w:(h + 1) * w])
                acc[h] = pv if alpha is None else acc[h] * alpha[:, h * w:(h + 1) * w] + pv

        st_next = scores(chunks[0])
        pend = None
        for ci, chunk in enumerate(chunks):
            st = st_next
            if ci + 1 < len(chunks):
                st_next = scores(chunks[ci + 1])
            if pend is not None:
                accumulate(pend)
            cm = jnp.max(st, axis=0, keepdims=True)
            m_new = cm if m is None else jnp.maximum(m, cm)
            pt = jnp.exp2(st - m_new).astype(BF16)
            alpha = None if m is None else jnp.exp2(m - m_new)
            pend = (pt, alpha, chunk)
            m = m_new
        accumulate(pend)
        for h in range(GLB_KV_HEADS):
            _store_heads(o_ref, 0, h, group, acc[h], acc[h][HEAD_DIM:HEAD_DIM + 1])

    attend([(c * KC_G, KC_G) for c in range(seq // KC_G)] + [(seq, n_ctx)])


def _global(px, vt, nb, seq, n_ctx, with_ctx_cols):
    ta = px.shape[0] if with_ctx_cols else nb * seq
    assert seq % KC_G == 0 and seq % TQ_G == 0
    n_q = seq // TQ_G
    ctx_rows = (nb * seq) // n_ctx
    return pl.pallas_call(
        functools.partial(_glb_kernel, seq=seq, n_ctx=n_ctx),
        grid=(nb, n_q),
        in_specs=[pl.BlockSpec((TQ_G, D_ATT), lambda b, i: (b * n_q + i, P_GQ // D_ATT)),
                  pl.BlockSpec((seq, N_KV), lambda b, i: (b, P_GK // N_KV)),
                  pl.BlockSpec((n_ctx, N_KV), lambda b, i: (ctx_rows + b, P_GK // N_KV)),
                  pl.BlockSpec((N_KV, seq), lambda b, i: (1, b)),
                  pl.BlockSpec((N_KV, n_ctx), lambda b, i: (1, ctx_rows + b))],
        out_specs=pl.BlockSpec((D_ATT, TQ_G), lambda b, i: (0, b * n_q + i)),
        out_shape=jax.ShapeDtypeStruct((D_ATT, ta), BF16),
        scratch_shapes=[pltpu.VMEM((seq + n_ctx, N_KV), BF16),
                        pltpu.VMEM((GLB_KV_HEADS, 2 * HEAD_DIM, seq + n_ctx), BF16)],
        compiler_params=_params(("parallel", "arbitrary")),
        name="global_attn",
    )(px, px, px, vt, vt)


def _route_rows(logits, bias):
    sig = jax.nn.sigmoid(logits)
    biased = sig + bias
    b_rows = [biased[e:e + 1, :] for e in range(N_EXPERTS)]
    s_rows = [sig[e:e + 1, :] for e in range(N_EXPERTS)]
    n_loc = EXPERTS_PER_GROUP

    best_score, grp = None, None
    for g in range(N_GROUPS):
        a = b_rows[g * n_loc:(g + 1) * n_loc]
        top2 = None
        for i in range(n_loc):
            for j in range(i + 1, n_loc):
                pair = a[i] + a[j]
                top2 = pair if top2 is None else jnp.maximum(top2, pair)
        if g == 0:
            best_score, grp = top2, jnp.zeros(top2.shape, jnp.int32)
        else:
            upd = top2 > best_score
            grp = jnp.where(upd, g, grp)
            best_score = jnp.where(upd, top2, best_score)

    def pick(rows, i):
        out = rows[i]
        for g in range(1, N_GROUPS):
            out = jnp.where(grp == g, rows[g * n_loc + i], out)
        return out

    cand = [pick(b_rows, i) for i in range(n_loc)]
    cand_s = [pick(s_rows, i) for i in range(n_loc)]
    m1, l1, w1 = cand[0], jnp.zeros(grp.shape, jnp.int32), cand_s[0]
    for i in range(1, n_loc):
        upd = cand[i] > m1
        m1 = jnp.where(upd, cand[i], m1)
        l1 = jnp.where(upd, i, l1)
        w1 = jnp.where(upd, cand_s[i], w1)
    m2, l2, w2 = None, None, None
    for i in range(n_loc):
        rest = jnp.where(l1 == i, -jnp.inf, cand[i])
        if i == 0:
            m2, l2, w2 = rest, jnp.zeros(grp.shape, jnp.int32), cand_s[0]
        else:
            upd = rest > m2
            m2 = jnp.where(upd, rest, m2)
            l2 = jnp.where(upd, i, l2)
            w2 = jnp.where(upd, cand_s[i], w2)
    tot = w1 + w2
    e_idx = jnp.concatenate([grp * n_loc + l1, grp * n_loc + l2], axis=0)
    gate = jnp.concatenate([w1 / tot, w2 / tot], axis=0)
    return e_idx, gate


def _outproj_kernel(of_ref, ob_ref, g_ref, ywt_ref, ygt_ref, x_ref, mod_ref, n2_ref, wf_ref, wr_ref, br_ref,
                    xo_ref, h_ref, e_ref, gate_ref, w_ref):
    @pl.when(pl.program_id(0) == 0)
    def _():
        w_ref[...] = wf_ref[...].astype(BF16)

    def ret_out(r):
        o = of_ref[r, :].astype(F32) + ob_ref[r, :].astype(F32)
        normed = []
        for h in range(RET_HEADS):
            oh = o[:, h * RET_DV:(h + 1) * RET_DV]
            mu = jnp.mean(oh, axis=-1, keepdims=True)
            var = jnp.mean(jnp.square(oh - mu), axis=-1, keepdims=True)
            normed.append((oh - mu) * lax.rsqrt(var + EPS))
        return (_silu(g_ref[r, :].astype(F32)) * jnp.concatenate(normed, axis=-1)).astype(BF16)

    def project(r, yr):
        acc = _dot_tn(ywt_ref[:, r], w_ref[D_RET:D_RET + D_ATT, :])
        acc += _dot_tn(ygt_ref[:, r], w_ref[D_RET + D_ATT:D_RET + 2 * D_ATT, :])
        return acc + _dot(yr, w_ref[0:D_RET, :])

    def residual_norm(r, acc):
        x = x_ref[r, :] + mod_ref[0, 2:3, :] * acc
        xo_ref[r, :] = x
        ms = jnp.mean(x * x, axis=-1, keepdims=True)
        h2 = x * lax.rsqrt(ms + EPS) * n2_ref[...]
        h2 = (h2 * (1.0 + mod_ref[0, 4:5, :]) + mod_ref[0, 3:4, :]).astype(BF16)
        h_ref[r, :] = h2
        return h2

    def route(r, h2):
        e_idx, gate = _route_rows(_dot_nt(wr_ref[...], h2), br_ref[...])
        e_ref[:, r] = e_idx
        gate_ref[:, r] = gate

    half = x_ref.shape[0] // 2
    ra, rb = slice(0, half), slice(half, 2 * half)
    yr_a = ret_out(ra)
    acc_a = project(ra, yr_a)
    yr_b = ret_out(rb)
    h2_a = residual_norm(ra, acc_a)
    acc_b = project(rb, yr_b)
    route(ra, h2_a)
    h2_b = residual_norm(rb, acc_b)
    route(rb, h2_b)


def _outproj(o_f, o_b, px, yw, yg, xa, mod_l, n2, w_out, layer, wr_t, b_r, nb, n_x_tiles, tiles_per_seq, n_tiles):
    ta, d = xa.shape
    once = pl.Buffered(1)

    def mod_idx(i):
        return (jnp.where(i < n_x_tiles, i // tiles_per_seq, nb), 0, 0)

    row = lambda i: (i, 0)
    return pl.pallas_call(
        _outproj_kernel,
        grid=(n_tiles,),
        in_specs=[pl.BlockSpec((TM, D_RET), row),
                  pl.BlockSpec((TM, D_RET), row),
                  pl.BlockSpec((TM, D_RET), lambda i: (i, P_RG // D_RET)),
                  pl.BlockSpec((D_ATT, TM), lambda i: (0, i)),
                  pl.BlockSpec((D_ATT, TM), lambda i: (0, i)),
                  pl.BlockSpec((TM, d), row),
                  pl.BlockSpec((1, 6, d), mod_idx),
                  pl.BlockSpec((1, d), lambda i: (0, 0)),
                  pl.BlockSpec((None, d, d), lambda i: (layer, 0, 0), pipeline_mode=once),
                  pl.BlockSpec((N_EXPERTS, d), lambda i: (0, 0)),
                  pl.BlockSpec((N_EXPERTS, 1), lambda i: (0, 0))],
        out_specs=[pl.BlockSpec((TM, d), row),
                   pl.BlockSpec((TM, d), row),
                   pl.BlockSpec((TOP_K, TM), lambda i: (0, i)),
                   pl.BlockSpec((TOP_K, TM), lambda i: (0, i))],
        out_shape=[jax.ShapeDtypeStruct((ta, d), F32),
                   jax.ShapeDtypeStruct((n_tiles * TM, d), BF16),
                   jax.ShapeDtypeStruct((TOP_K, n_tiles * TM), jnp.int32),
                   jax.ShapeDtypeStruct((TOP_K, n_tiles * TM), F32)],
        scratch_shapes=[pltpu.VMEM((d, d), BF16)],
        input_output_aliases={5: 0},
        compiler_params=_params(("arbitrary",)),
        name="out_proj",
    )(o_f, o_b, px, yw, yg, xa, mod_l, n2, w_out, wr_t, b_r)


def _moe_kernel(be_ref, first_ref, nu_ref, x_ref, wgu_f_ref, wd_f_ref, o_ref, wgu_ref, wd_ref):
    i = pl.program_id(0)

    @pl.when(jnp.logical_and(i < nu_ref[0], first_ref[i] == 1))
    def _():
        wgu_ref[...] = wgu_f_ref[...].astype(BF16)
        wd_ref[...] = wd_f_ref[...].astype(BF16)

    @pl.when(i < nu_ref[0])
    def _():
        f = wd_ref.shape[0]
        au = _dot(x_ref[...], wgu_ref[...])
        mid = (_silu(au[:, :f]) * au[:, f:]).astype(BF16)
        o_ref[...] = _dot(mid, wd_ref[...]).astype(BF16)

    @pl.when(i >= nu_ref[0])
    def _():
        o_ref[...] = jnp.zeros_like(o_ref)


def _moe_ffn(blk_e, first, n_used, xs, w_gu, w_down, layer):
    rows, d = xs.shape
    f2 = w_gu.shape[3]
    f = w_down.shape[2]
    grid_spec = pltpu.PrefetchScalarGridSpec(
        num_scalar_prefetch=3,
        grid=(rows // TM_E,),
        in_specs=[pl.BlockSpec((TM_E, d), lambda i, be, fi, nu: (i, 0)),
                  pl.BlockSpec((None, None, d, f2), lambda i, be, fi, nu: (layer, be[i], 0, 0)),
                  pl.BlockSpec((None, None, f, d), lambda i, be, fi, nu: (layer, be[i], 0, 0))],
        out_specs=pl.BlockSpec((TM_E, d), lambda i, be, fi, nu: (i, 0)),
        scratch_shapes=[pltpu.VMEM((d, f2), BF16), pltpu.VMEM((f, d), BF16)],
    )
    return pl.pallas_call(
        _moe_kernel,
        grid_spec=grid_spec,
        out_shape=jax.ShapeDtypeStruct((rows, d), BF16),
        compiler_params=_params(("arbitrary",)),
        name="moe_ffn",
    )(blk_e, first, n_used, xs, w_gu, w_down)


def _moe_plan(e_idx_t, n_tok):
    n_asg = n_tok * TOP_K
    flat_e = e_idx_t[:, :n_tok].reshape(-1)
    onehot = (flat_e[:, None] == jnp.arange(N_EXPERTS, dtype=jnp.int32)[None, :]).astype(jnp.int32)
    csum = jnp.cumsum(onehot, axis=0)
    counts = csum[-1]
    rank = jnp.sum(csum * onehot, axis=1) - 1
    padded = (counts + TM_E - 1) // TM_E * TM_E
    pad_end = jnp.cumsum(padded)
    pad_start = pad_end - padded
    cnt_start = jnp.cumsum(counts) - counts
    pos = (jnp.sum(pad_start[None, :] * onehot, axis=1) + rank).reshape(TOP_K, n_tok)
    n_blocks = (n_asg + N_EXPERTS * (TM_E - 1) + TM_E - 1) // TM_E
    blk_start = jnp.arange(n_blocks, dtype=jnp.int32) * TM_E
    blk_e = jnp.minimum(jnp.sum(blk_start[:, None] >= pad_end[None, :], axis=1), N_EXPERTS - 1).astype(jnp.int32)
    first = jnp.concatenate([jnp.ones((1,), jnp.int32), (blk_e[1:] != blk_e[:-1]).astype(jnp.int32)])
    n_used = (pad_end[-1] // TM_E).astype(jnp.int32).reshape(1)
    order = jnp.argsort(flat_e)
    e_row = jnp.repeat(blk_e, TM_E)
    p = jnp.arange(n_blocks * TM_E, dtype=jnp.int32)
    r = p - pad_start[e_row]
    src = jnp.where(r < counts[e_row], order[jnp.clip(cnt_start[e_row] + r, 0, n_asg - 1)], p) % n_tok
    return (blk_e, first, n_used), src, pos


def _combine_kernel(x_ref, y0_ref, y1_ref, gate_ref, mod_ref, o_ref):
    g = gate_ref[...]
    y = y0_ref[...].astype(F32) * g[:, 0:1] + y1_ref[...].astype(F32) * g[:, 1:2]
    o_ref[...] = x_ref[...] + mod_ref[0, 5:6, :] * y


def _combine_final_kernel(x_ref, y0_ref, y1_ref, gate_ref, mod_ref, prev_ref, o_ref):
    del prev_ref
    _combine_kernel(x_ref, y0_ref, y1_ref, gate_ref, mod_ref, o_ref)


def _combine_final(xa, y0, y1, gate, mod_l, tiles_per_seq, n_tiles, prev, tile_off, total_rows):
    d = xa.shape[1]
    row = lambda i: (i, 0)
    ins = [pl.BlockSpec((TM, d), row), pl.BlockSpec((TM, d), row), pl.BlockSpec((TM, d), row),
           pl.BlockSpec((TM, TOP_K), row), pl.BlockSpec((1, 6, d), lambda i: (i // tiles_per_seq, 0, 0))]
    args = [xa, y0, y1, gate, mod_l]
    if prev is not None:
        ins.append(pl.BlockSpec(memory_space=pl.ANY))
        args.append(prev)
    return pl.pallas_call(
        _combine_kernel if prev is None else _combine_final_kernel,
        grid=(n_tiles,),
        in_specs=ins,
        out_specs=pl.BlockSpec((TM, d), lambda i: (tile_off + i, 0)),
        out_shape=jax.ShapeDtypeStruct((total_rows, d), F32),
        input_output_aliases={} if prev is None else {5: 0},
        compiler_params=_params(("parallel",)),
        name="moe_combine_out",
    )(*args)


def _rope_tables(seq, dtype):
    rows = seq // GRID_W
    row = jnp.repeat(jnp.arange(rows), GRID_W).astype(jnp.float32)
    col = (jnp.arange(rows * GRID_W) % GRID_W).astype(jnp.float32)
    half = HEAD_DIM // 2
    inv = jnp.power(ROPE_BASE, -jnp.arange(0, half, 2, dtype=jnp.float32) / half)
    ang_r, ang_c = row[:, None] * inv, col[:, None] * inv
    cos_r, cos_c = jnp.cos(ang_r).astype(dtype), jnp.cos(ang_c).astype(dtype)
    sin_r, sin_c = jnp.sin(ang_r).astype(dtype), jnp.sin(ang_c).astype(dtype)
    cos_h = jnp.concatenate([cos_r, cos_r, cos_c, cos_c], -1)
    sin_h = jnp.concatenate([-sin_r, sin_r, -sin_c, sin_c], -1)
    cos_t = jnp.concatenate([jnp.tile(cos_h, (1, D_ATT // HEAD_DIM)), jnp.ones((TM, D_ATT), dtype)], 0)
    sin_t = jnp.concatenate([jnp.tile(sin_h, (1, D_ATT // HEAD_DIM)), jnp.zeros((TM, D_ATT), dtype)], 0)
    return cos_t, sin_t


def kernel(x, c, ctx, c_ctx, ada_w, ada_b, norm1, norm2, w_in, w_out, ret_decay, win_qk_gain, win_sink,
           glb_qk_gain, w_router, b_router, w_gate_up, w_down):
    nb_all, seq, d = x.shape
    n_ctx = ctx.shape[1]
    depth = ada_w.shape[0]
    n_streams = N_STREAMS if nb_all % N_STREAMS == 0 and (nb_all // N_STREAMS * n_ctx) % TM == 0 else 1
    nb = nb_all // n_streams
    n_x = nb * seq
    ta = n_x + nb * n_ctx
    assert seq % TM == 0 and (nb * n_ctx) % TM == 0 and seq % GRID_W == 0
    tiles_per_seq = seq // TM
    n_x_tiles = n_x // TM
    n_tiles = ta // TM

    mod_rows = (nb_all + 1 + 7) // 8 * 8
    cc = jnp.zeros((mod_rows, d), F32).at[:nb_all].set(c).at[nb_all].set(c_ctx)
    mod_all = _ada(cc, ada_w, ada_b).reshape(depth, mod_rows, 6, d)

    cos_t, sin_t = _rope_tables(seq, F32)
    head_of = jnp.arange(D_ATT) // HEAD_DIM
    hm = jnp.where(head_of[:, None] == head_of[None, :], 1.0 / HEAD_DIM, 0.0).astype(BF16)
    wr_t = w_router.T.astype(BF16)
    b_r = b_router.astype(F32)[:, None]
    key_j = jnp.arange(WINDOW)[:, None]
    qry_i = jnp.tile(jnp.arange(WINDOW), WIN_Q_HEADS)[None, :]
    tri = jnp.stack([jnp.where(key_j >= qry_i, 0.0, NEG_INF), jnp.where(key_j <= qry_i, 0.0, NEG_INF)]).astype(F32)

    layer_tabs = []
    for l in range(depth):
        gains = jnp.zeros((8, D_ATT), F32)
        gains = gains.at[0].set(jnp.tile(win_qk_gain[l, 0], WIN_Q_HEADS))
        gains = gains.at[1, :WIN_KV_HEADS * HEAD_DIM].set(jnp.tile(win_qk_gain[l, 1], WIN_KV_HEADS))
        gains = gains.at[2].set(jnp.tile(glb_qk_gain[l, 0], GLB_Q_HEADS))
        gains = gains.at[3, :GLB_KV_HEADS * HEAD_DIM].set(jnp.tile(glb_qk_gain[l, 1], GLB_KV_HEADS))
        sink_row = jnp.repeat(win_sink[l].astype(F32) * LOG2_E, WINDOW)[None, :]
        layer_tabs.append((gains, sink_row, _ret_tables(ret_decay[l])))

    def mix(l, s, xa, pending):
        last = l == depth - 1
        gains, sink_row, (gc, dmat, xi, zeta) = layer_tabs[l]
        xa, px, vt = _inproj(xa, mods[s][l], norm1[l][None, :], w_in, l, cos_t, sin_t, gains, hm,
                             nb, n_x_tiles, tiles_per_seq, pending)
        o_f, o_b = _retention(px, gc, dmat, xi, zeta, nb, seq, n_ctx)
        yw = _window(px, vt, sink_row, tri, nb, seq, n_ctx, not last)
        yg = _global(px, vt, nb, seq, n_ctx, not last)
        if not last:
            yw, yg = _ctx_attention(px, vt, sink_row, yw, yg, nb, seq, n_ctx)
        return _outproj(o_f, o_b, px, yw, yg, xa, mods[s][l], norm2[l][None, :], w_out, l,
                        wr_t, b_r, nb, n_x_tiles, tiles_per_seq, n_x_tiles if last else n_tiles)

    def tie(a, b):
        if n_streams == 1:
            return a, b
        return lax.optimization_barrier((a, b))

    mods, xas = [], []
    for s in range(n_streams):
        b0 = s * nb
        mods.append(jnp.concatenate([mod_all[:, b0:b0 + nb], mod_all[:, nb_all:nb_all + 1]], axis=1))
        xas.append(jnp.concatenate([x[b0:b0 + nb].reshape(n_x, d), ctx[b0:b0 + nb].reshape(nb * n_ctx, d)], 0))

    out = None
    pending = [None] * n_streams
    carry = None
    s_last = n_streams - 1

    def gather_back(yp, pos):
        return yp[pos[0]], yp[pos[1]]

    for l in range(depth):
        last = l == depth - 1
        n_tok = n_x if last else ta
        states, srcs, plans = [], [], []
        for s in range(n_streams):
            xin = xas[s]
            if s > 0:
                xin, srcs[s - 1] = tie(xin, srcs[s - 1])
            elif carry is not None:
                cs, yp_c, pos_c, gate_c, mod_c = carry
                xin, yp_c = tie(xin, yp_c)
                pending[cs] = gather_back(yp_c, pos_c) + (gate_c, mod_c)
                carry = None
            states.append(mix(l, s, xin, pending[s]))
            tables, src, pos = _moe_plan(states[s][2], n_tok)
            plans.append((tables, pos))
            srcs.append(src)
        yp_prev = None
        for s in range(n_streams):
            xa, h2, _, gate_t = states[s]
            tables, pos = plans[s]
            xs = h2[srcs[s]]
            if s < s_last:
                xs, srcs[s + 1] = tie(xs, srcs[s + 1])
            if s > 0:
                xs, yp_prev = tie(xs, yp_prev)
                ps, ppos, pgate, pmod, pxa = prev_info
                done = gather_back(yp_prev, ppos) + (pgate, pmod)
                if last:
                    out = _combine_final(pxa, *done, tiles_per_seq, n_x_tiles, out, ps * n_x_tiles, n_streams * n_x)
                else:
                    pending[ps] = done
            yp_prev = _moe_ffn(*tables, xs, w_gate_up, w_down, l)
            prev_info = (s, pos, gate_t.T, mods[s][l], xa)
            xas[s] = xa
        ps, ppos, pgate, pmod, pxa = prev_info
        if last:
            out = _combine_final(pxa, *gather_back(yp_prev, ppos), pgate, pmod, tiles_per_seq, n_x_tiles, out,
                                 ps * n_x_tiles, n_streams * n_x)
        elif n_streams == 1:
            pending[ps] = gather_back(yp_prev, ppos) + (pgate, pmod)
        else:
            carry = (ps, yp_prev, ppos, pgate, pmod)
    return out.reshape(nb_all, seq, d)
```

```python
import functools

import jax
import jax.numpy as jnp
from jax import lax
from jax.experimental import pallas as pl
from jax.experimental.pallas import tpu as pltpu

GRID_W = 64
HEAD_DIM = 64
RET_HEADS = 4
RET_DK = 64
RET_DV = 128
WIN_Q_HEADS = 4
WIN_KV_HEADS = 2
WINDOW = 128
GLB_Q_HEADS = 4
GLB_KV_HEADS = 2
ROPE_BASE = 10000.0
N_EXPERTS = 16
N_GROUPS = 4
EXPERTS_PER_GROUP = N_EXPERTS // N_GROUPS
TOP_K = 2
EPS = 1e-6
NEG_INF = -1e30

RQ, RK, RV, RG = 0, 256, 512, 1024
WQ, WK, WV = 1536, 1792, 1920
GQ, GK, GV = 2048, 2304, 2432
D_IN = 2560
D_RET = RET_HEADS * RET_DV
D_ATT = WIN_Q_HEADS * HEAD_DIM
N_KV = WIN_KV_HEADS * HEAD_DIM
P_RQ, P_RK, P_RV, P_RG = 0, 256, 512, 1024
P_WQ, P_GQ, P_WK, P_GK = 1536, 1792, 2048, 2176
PX_W = 2304

TM = 512
RET_C = 256
TQ_W = 512
TQ_G = 256
KC_G = 256
LOG2_E = 1.4426950408889634
assert GLB_KV_HEADS == 2
TM_E = 512
N_STREAMS = 2
VMEM_LIMIT = 56 * 1024 * 1024

F32 = jnp.float32
BF16 = jnp.bfloat16


def _dot(a, b):
    return jnp.dot(a, b, preferred_element_type=F32)


def _dot_nt(a, b):
    return lax.dot_general(a, b, (((1,), (1,)), ((), ())), preferred_element_type=F32)


def _dot_tn(a, b):
    return lax.dot_general(a, b, (((0,), (0,)), ((), ())), preferred_element_type=F32)


def _silu(x):
    return x * jax.nn.sigmoid(x)


def _params(sem):
    return pltpu.CompilerParams(dimension_semantics=sem, vmem_limit_bytes=VMEM_LIMIT)


def _ada_kernel(c_ref, w_ref, b_ref, o_ref):
    s = _silu(c_ref[...]).astype(BF16)
    o_ref[0] = _dot(s, w_ref[0].astype(BF16)) + b_ref[0]


def _ada(cc, ada_w, ada_b):
    depth, d, n = ada_w.shape
    tn = 1536
    rows = cc.shape[0]
    return pl.pallas_call(
        _ada_kernel,
        grid=(depth, n // tn),
        in_specs=[pl.BlockSpec((rows, d), lambda l, j: (0, 0)),
                  pl.BlockSpec((1, d, tn), lambda l, j: (l, 0, j)),
                  pl.BlockSpec((1, 1, tn), lambda l, j: (l, 0, j))],
        out_specs=pl.BlockSpec((1, rows, tn), lambda l, j: (l, 0, j)),
        out_shape=jax.ShapeDtypeStruct((depth, rows, n), F32),
        compiler_params=_params(("arbitrary", "arbitrary")),
        name="ada_mod",
    )(cc, ada_w, ada_b.reshape(depth, 1, n))


def _inproj_fused_kernel(x_ref, y0_ref, y1_ref, gate_ref, modp_ref, mod_ref, n1_ref, wf_ref, wvt_ref, cos_ref,
                         sin_ref, gains_ref, hm_ref, xo_ref, o_ref, vt_ref, w_ref):
    g = gate_ref[...]
    y = y0_ref[...].astype(F32) * g[:, 0:1] + y1_ref[...].astype(F32) * g[:, 1:2]
    x = x_ref[...] + modp_ref[0, 5:6, :] * y
    xo_ref[...] = x
    _inproj_body(x, mod_ref, n1_ref, wf_ref, wvt_ref, cos_ref, sin_ref, gains_ref, hm_ref, o_ref, vt_ref, w_ref)


def _inproj_kernel(x_ref, mod_ref, n1_ref, wf_ref, wvt_ref, cos_ref, sin_ref, gains_ref, hm_ref, o_ref, vt_ref,
                   w_ref):
    _inproj_body(x_ref[...], mod_ref, n1_ref, wf_ref, wvt_ref, cos_ref, sin_ref, gains_ref, hm_ref, o_ref, vt_ref,
                 w_ref)


def _inproj_body(x, mod_ref, n1_ref, wf_ref, wvt_ref, cos_ref, sin_ref, gains_ref, hm_ref, o_ref, vt_ref, w_ref):
    @pl.when(pl.program_id(0) == 0)
    def _():
        w_ref[...] = wf_ref[...].astype(BF16)

    ms = jnp.mean(x * x, axis=-1, keepdims=True)
    h = x * lax.rsqrt(ms + EPS) * n1_ref[...]
    h = h * (1.0 + mod_ref[0, 1:2, :]) + mod_ref[0, 0:1, :]
    hb = h.astype(BF16)

    def proj(lo, width):
        return _dot(hb, w_ref[:, lo:lo + width])

    def head_msq(y):
        sq = y * y
        sq_hi = sq.astype(BF16)
        sq_lo = (sq - sq_hi.astype(F32)).astype(BF16)
        hm = hm_ref[0:y.shape[1], 0:y.shape[1]]
        return _dot(sq_hi, hm) + _dot(sq_lo, hm)

    def qk_finish(y, msq, gain_row, scale):
        width = y.shape[1]
        yn = y * lax.rsqrt(msq + EPS) * gains_ref[gain_row:gain_row + 1, 0:width]
        nxt = pltpu.roll(yn, width - HEAD_DIM // 4, 1)
        prv = pltpu.roll(yn, HEAD_DIM // 4, 1)
        lane = lax.broadcasted_iota(jnp.int32, yn.shape, 1)
        partner = jnp.where((lane % (HEAD_DIM // 2)) < HEAD_DIM // 4, nxt, prv)
        yr = yn * cos_ref[:, 0:width] + partner * sin_ref[:, 0:width]
        return (yr * scale).astype(BF16)

    q_scale = HEAD_DIM ** -0.5 * LOG2_E
    qk_segs = ((WQ, D_ATT, 0, q_scale, P_WQ), (GQ, D_ATT, 2, q_scale, P_GQ),
               (WK, N_KV, 1, 1.0, P_WK), (GK, N_KV, 3, 1.0, P_GK))
    plain_segs = ((RQ, RK - RQ, 1.0, P_RQ), (RK, RV - RK, RET_DK ** -0.5, P_RK),
                  (RV, RG - RV, 1.0, P_RV), (RG, WQ - RG, 1.0, P_RG))
    ys = [proj(lo, width) for lo, width, _, _, _ in qk_segs]
    stats = [head_msq(y) for y in ys]
    vt_ref[...] = _dot_nt(wvt_ref[...], hb).astype(BF16)
    for (lo, width, scale, dst), (_, qwidth, gain_row, qscale, qdst), y, msq in zip(plain_segs, qk_segs, ys, stats):
        p = proj(lo, width)
        o_ref[:, dst:dst + width] = (p if scale == 1.0 else p * scale).astype(BF16)
        o_ref[:, qdst:qdst + qwidth] = qk_finish(y, msq, gain_row, qscale)


def _inproj(xa, mod_l, n1, w_in, layer, cos_t, sin_t, gains, hm, nb, n_x_tiles, tiles_per_seq, pending=None):
    ta, d = xa.shape
    n_tiles = ta // TM
    w_vt = jnp.concatenate([w_in[layer, :, WV:WV + N_KV], w_in[layer, :, GV:GV + N_KV]], axis=1).T.astype(BF16)
    n_gv = 2 * N_KV
    once = pl.Buffered(1)

    def mod_idx(i):
        return (jnp.where(i < n_x_tiles, i // tiles_per_seq, nb), 0, 0)

    def rope_idx(i):
        return (jnp.where(i < n_x_tiles, i % tiles_per_seq, tiles_per_seq), 0)

    row = lambda i: (i, 0)
    in_specs = [pl.BlockSpec((1, 6, d), mod_idx),
                pl.BlockSpec((1, d), lambda i: (0, 0)),
                pl.BlockSpec((None, d, D_IN), lambda i: (layer, 0, 0), pipeline_mode=once),
                pl.BlockSpec((n_gv, d), lambda i: (0, 0), pipeline_mode=once),
                pl.BlockSpec((TM, D_ATT), rope_idx),
                pl.BlockSpec((TM, D_ATT), rope_idx),
                pl.BlockSpec((8, D_ATT), lambda i: (0, 0)),
                pl.BlockSpec((D_ATT, D_ATT), lambda i: (0, 0))]
    args = [mod_l, n1, w_in, w_vt, cos_t, sin_t, gains, hm]
    out_specs = [pl.BlockSpec((TM, PX_W), row), pl.BlockSpec((n_gv, TM), lambda i: (0, i))]
    out_shape = [jax.ShapeDtypeStruct((ta, PX_W), BF16), jax.ShapeDtypeStruct((n_gv, ta), BF16)]
    if pending is None:
        kern, aliases = _inproj_kernel, {}
        in_specs = [pl.BlockSpec((TM, d), row)] + in_specs
        args = [xa] + args
    else:
        y0, y1, gate, mod_prev = pending
        kern, aliases = _inproj_fused_kernel, {0: 0}
        in_specs = [pl.BlockSpec((TM, d), row), pl.BlockSpec((TM, d), row), pl.BlockSpec((TM, d), row),
                    pl.BlockSpec((TM, TOP_K), row), pl.BlockSpec((1, 6, d), mod_idx)] + in_specs
        args = [xa, y0, y1, gate, mod_prev] + args
        out_specs = [pl.BlockSpec((TM, d), row)] + out_specs
        out_shape = [jax.ShapeDtypeStruct((ta, d), F32)] + out_shape
    res = pl.pallas_call(
        kern,
        grid=(n_tiles,),
        in_specs=in_specs,
        out_specs=out_specs,
        out_shape=out_shape,
        scratch_shapes=[pltpu.VMEM((d, D_IN), BF16)],
        input_output_aliases=aliases,
        compiler_params=_params(("arbitrary",)),
        name="in_proj",
    )(*args)
    return (xa,) + tuple(res) if pending is None else tuple(res)


def _ret_kernel(gc_ref, qf_ref, kf_ref, vf_ref, qb_ref, kb_ref, vb_ref, dmat_ref, xi_ref, zeta_ref,
                of_ref, ob_ref, s_ref):
    @pl.when(pl.program_id(1) == 0)
    def _():
        s_ref[...] = jnp.zeros_like(s_ref)

    dirs = ((qf_ref, kf_ref, vf_ref, of_ref), (qb_ref, kb_ref, vb_ref, ob_ref))
    for d, (q_ref, k_ref, v_ref, o_ref) in enumerate(dirs):
        q = q_ref[...]
        k = k_ref[...]
        v = v_ref[...]
        kz = (k.astype(F32) * zeta_ref[d]).astype(BF16)
        outs = []
        for h in range(RET_HEADS):
            i = d * RET_HEADS + h
            qh = q[:, h * RET_DK:(h + 1) * RET_DK]
            kh = k[:, h * RET_DK:(h + 1) * RET_DK]
            vh = v[:, h * RET_DV:(h + 1) * RET_DV]
            att = _dot_nt(qh, kh) * dmat_ref[i]
            state = s_ref[i]
            outs.append(_dot(att.astype(BF16), vh) + _dot(qh, state.astype(BF16)) * xi_ref[i])
            s_ref[i] = gc_ref[i] * state + _dot_tn(kz[:, h * RET_DK:(h + 1) * RET_DK], vh)
        o_ref[...] = jnp.concatenate(outs, axis=-1).astype(BF16)


def _retention(px, gc, dmat, xi, zeta, nb, seq, n_ctx):
    ta = px.shape[0]
    n_x = seq // RET_C
    n_c = n_ctx // RET_C
    ctx_base = nb * n_x
    steps = n_c + n_x

    def row_f(b, c):
        return jnp.where(c < n_c, ctx_base + b * n_c + c, b * n_x + (c - n_c))

    def row_b(b, c):
        return jnp.where(c < n_c, ctx_base + b * n_c + (n_c - 1 - c), b * n_x + (steps - 1 - c))

    def spec(width, col, row):
        return pl.BlockSpec((RET_C, width), lambda b, c: (row(b, c), col))

    const3 = lambda b, c: (0, 0, 0)
    return pl.pallas_call(
        _ret_kernel,
        grid=(nb, steps),
        in_specs=[pl.BlockSpec(memory_space=pltpu.SMEM),
                  spec(256, P_RQ // 256, row_f), spec(256, P_RK // 256, row_f), spec(D_RET, P_RV // D_RET, row_f),
                  spec(256, P_RQ // 256, row_b), spec(256, P_RK // 256, row_b), spec(D_RET, P_RV // D_RET, row_b),
                  pl.BlockSpec((2 * RET_HEADS, RET_C, RET_C), const3),
                  pl.BlockSpec((2 * RET_HEADS, RET_C, RET_DV), const3),
                  pl.BlockSpec((2, RET_C, RET_HEADS * RET_DK), const3)],
        out_specs=[spec(D_RET, 0, row_f), spec(D_RET, 0, row_b)],
        out_shape=[jax.ShapeDtypeStruct((ta, D_RET), BF16)] * 2,
        scratch_shapes=[pltpu.VMEM((2 * RET_HEADS, RET_DK, RET_DV), F32)],
        compiler_params=_params(("parallel", "arbitrary")),
        name="retention",
    )(gc, px, px, px, px, px, px, dmat, xi, zeta)


def _ret_tables(decay_logit):
    log_g = jax.nn.log_sigmoid(decay_logit.astype(F32)).reshape(2 * RET_HEADS)
    idx = jnp.arange(RET_C, dtype=F32)
    diff = idx[:, None] - idx[None, :]
    lg = log_g[:, None, None]
    d_fwd = jnp.where(diff >= 0, jnp.exp(jnp.maximum(diff, 0.0) * lg), 0.0)
    d_bwd = jnp.where(diff <= 0, jnp.exp(jnp.maximum(-diff, 0.0) * lg), 0.0)
    is_bwd = (jnp.arange(2 * RET_HEADS) >= RET_HEADS)[:, None, None]
    dmat = jnp.where(is_bwd, d_bwd, d_fwd)
    pos = jnp.where(is_bwd[:, :, 0], RET_C - 1.0 - idx[None, :], idx[None, :])
    xi = jnp.exp((pos + 1.0) * log_g[:, None])
    zeta = jnp.exp((RET_C - 1.0 - pos) * log_g[:, None])
    gc = jnp.exp(RET_C * log_g)
    xi = jnp.broadcast_to(xi[:, :, None], (2 * RET_HEADS, RET_C, RET_DV))
    zeta = jnp.repeat(zeta.reshape(2, RET_HEADS, RET_C).transpose(0, 2, 1), RET_DK, axis=-1)
    return gc, dmat, xi, zeta


def _pad_heads(q, n_kv):
    rows, width = q.shape
    group = width // HEAD_DIM // n_kv
    zeros = jnp.zeros((rows, HEAD_DIM), BF16)
    out = []
    for h in range(n_kv):
        for g in range(group):
            qh = q[:, (h * group + g) * HEAD_DIM:(h * group + g + 1) * HEAD_DIM]
            out.append(jnp.concatenate([qh if hh == h else zeros for hh in range(n_kv)], axis=1))
    return jnp.concatenate(out, axis=0)


def _fill_values(v_ext, lo, vt):
    n = vt.shape[1]
    for h in range(v_ext.shape[0]):
        v_ext[h, 0:HEAD_DIM, lo:lo + n] = vt[h * HEAD_DIM:(h + 1) * HEAD_DIM, :]


def _store_heads(o_ref, col, h, group, oe, den):
    n = oe.shape[1] // group
    o = oe[0:HEAD_DIM] / den
    for g in range(group):
        r0 = (h * group + g) * HEAD_DIM
        o_ref[r0:r0 + HEAD_DIM, col:col + n] = o[:, g * n:(g + 1) * n].astype(BF16)


def _win_kernel(q_ref, k_ref, kc_ref, vt_ref, vtc_ref, sink_ref, tri_ref, o_ref, k_pad, v_ext, *, seq, n_ctx):
    i = pl.program_id(1)
    group = WIN_Q_HEADS // WIN_KV_HEADS
    sub = WINDOW
    w = group * sub
    span = 3 * WINDOW
    c0 = seq + 2 * WINDOW
    n_sub = q_ref.shape[0] // sub

    @pl.when(i == 0)
    def _():
        zk = jnp.zeros((WINDOW, N_KV), BF16)
        k_pad[0:WINDOW, :] = zk
        k_pad[WINDOW:WINDOW + seq, :] = k_ref[...]
        k_pad[WINDOW + seq:c0, :] = zk
        k_pad[c0:c0 + n_ctx, :] = kc_ref[...]
        v_ext[:, 0:HEAD_DIM, 0:WINDOW] = jnp.zeros((WIN_KV_HEADS, HEAD_DIM, WINDOW), BF16)
        v_ext[:, 0:HEAD_DIM, WINDOW + seq:c0] = jnp.zeros((WIN_KV_HEADS, HEAD_DIM, WINDOW), BF16)
        _fill_values(v_ext, WINDOW, vt_ref[...])
        _fill_values(v_ext, c0, vtc_ref[...])
        v_ext[:, HEAD_DIM:2 * HEAD_DIM, :] = jnp.ones((WIN_KV_HEADS, HEAD_DIM, c0 + n_ctx), BF16)

    sink = sink_ref[...]
    kc = k_pad[c0:c0 + n_ctx, :]
    start = i * q_ref.shape[0]

    def scores(a):
        qs = pl.multiple_of(start + a * sub, sub)
        q4 = _pad_heads(q_ref[a * sub:(a + 1) * sub, :], WIN_KV_HEADS)
        return a, qs, _dot_nt(k_pad[pl.ds(qs, span), :], q4), _dot_nt(kc, q4)

    def softmax(a, qs, st, sc):
        lo_edge = jnp.where(qs == 0, NEG_INF, 0.0)
        hi_edge = jnp.where(qs + sub == seq, NEG_INF, 0.0)
        s0 = st[0:WINDOW] + (tri_ref[0] + lo_edge)
        s1 = st[WINDOW:2 * WINDOW]
        s2 = st[2 * WINDOW:span] + (tri_ref[1] + hi_edge)
        m = jnp.maximum(jnp.maximum(jnp.max(s0, axis=0, keepdims=True), jnp.max(s1, axis=0, keepdims=True)),
                        jnp.maximum(jnp.max(s2, axis=0, keepdims=True), jnp.max(sc, axis=0, keepdims=True)))
        m = jnp.maximum(m, sink)
        pw = jnp.concatenate([jnp.exp2(s0 - m), jnp.exp2(s1 - m), jnp.exp2(s2 - m)], axis=0).astype(BF16)
        return a, qs, m, pw, jnp.exp2(sc - m).astype(BF16)

    def values(a, qs, m, pw, pc):
        for h in range(WIN_KV_HEADS):
            cs = slice(h * w, (h + 1) * w)
            oe = _dot(v_ext[h, :, pl.ds(qs, span)], pw[:, cs]) + _dot(v_ext[h, :, c0:c0 + n_ctx], pc[:, cs])
            den = oe[HEAD_DIM:HEAD_DIM + 1] + jnp.exp2(sink[:, cs] - m[:, cs])
            _store_heads(o_ref, a * sub, h, group, oe, den)

    nxt = scores(0)
    pend = None
    for a in range(n_sub):
        cur = nxt
        if a + 1 < n_sub:
            nxt = scores(a + 1)
        if pend is not None:
            values(*pend)
        pend = softmax(*cur)
    values(*pend)


def _window(px, vt, sink_row, tri, nb, seq, n_ctx, with_ctx_cols):
    ta = px.shape[0] if with_ctx_cols else nb * seq
    assert seq % TQ_W == 0 and TQ_W % WINDOW == 0
    n_q = seq // TQ_W
    ctx_rows = (nb * seq) // n_ctx
    n_keys = seq + 2 * WINDOW + n_ctx
    return pl.pallas_call(
        functools.partial(_win_kernel, seq=seq, n_ctx=n_ctx),
        grid=(nb, n_q),
        in_specs=[pl.BlockSpec((TQ_W, D_ATT), lambda b, i: (b * n_q + i, P_WQ // D_ATT)),
                  pl.BlockSpec((seq, N_KV), lambda b, i: (b, P_WK // N_KV)),
                  pl.BlockSpec((n_ctx, N_KV), lambda b, i: (ctx_rows + b, P_WK // N_KV)),
                  pl.BlockSpec((N_KV, seq), lambda b, i: (0, b)),
                  pl.BlockSpec((N_KV, n_ctx), lambda b, i: (0, ctx_rows + b)),
                  pl.BlockSpec((1, WIN_Q_HEADS * WINDOW), lambda b, i: (0, 0)),
                  pl.BlockSpec((2, WINDOW, WIN_Q_HEADS * WINDOW), lambda b, i: (0, 0, 0))],
        out_specs=pl.BlockSpec((D_ATT, TQ_W), lambda b, i: (0, b * n_q + i)),
        out_shape=jax.ShapeDtypeStruct((D_ATT, ta), BF16),
        scratch_shapes=[pltpu.VMEM((n_keys, N_KV), BF16),
                        pltpu.VMEM((WIN_KV_HEADS, 2 * HEAD_DIM, n_keys), BF16)],
        compiler_params=_params(("parallel", "arbitrary")),
        name="window_attn",
    )(px, px, px, vt, vt, sink_row, tri)


def _ctx_kernel(qw_ref, kw_ref, qg_ref, kg_ref, vt_ref, sink_ref, yw_in, yg_in, yw_ref, yg_ref):
    del yw_in, yg_in
    n_ctx = qw_ref.shape[0]
    sub = WINDOW
    ones = jnp.ones((HEAD_DIM, n_ctx), BF16)
    for q_ref, k_ref, row0, o_ref, sink in ((qw_ref, kw_ref, 0, yw_ref, sink_ref[...]),
                                            (qg_ref, kg_ref, N_KV, yg_ref, None)):
        n_kv = k_ref.shape[1] // HEAD_DIM
        group = q_ref.shape[1] // HEAD_DIM // n_kv
        w = group * sub
        k = k_ref[...]
        for a in range(n_ctx // sub):
            sc = _dot_nt(k, _pad_heads(q_ref[a * sub:(a + 1) * sub, :], n_kv))
            m = jnp.max(sc, axis=0, keepdims=True)
            if sink is not None:
                m = jnp.maximum(m, sink)
            p = jnp.exp2(sc - m).astype(BF16)
            for h in range(n_kv):
                cs = slice(h * w, (h + 1) * w)
                ve = jnp.concatenate([vt_ref[row0 + h * HEAD_DIM:row0 + (h + 1) * HEAD_DIM, :], ones], axis=0)
                oe = _dot(ve, p[:, cs])
                den = oe[HEAD_DIM:HEAD_DIM + 1]
                if sink is not None:
                    den = den + jnp.exp2(sink[:, cs] - m[:, cs])
                _store_heads(o_ref, a * sub, h, group, oe, den)


def _ctx_attention(px, vt, sink_row, ywt, ygt, nb, seq, n_ctx):
    ctx_rows = (nb * seq) // n_ctx
    row = lambda col: (lambda b: (ctx_rows + b, col))
    any_spec = pl.BlockSpec(memory_space=pl.ANY)
    out_spec = pl.BlockSpec((D_ATT, n_ctx), lambda b: (0, ctx_rows + b))
    return pl.pallas_call(
        _ctx_kernel,
        grid=(nb,),
        in_specs=[pl.BlockSpec((n_ctx, D_ATT), row(P_WQ // D_ATT)),
                  pl.BlockSpec((n_ctx, N_KV), row(P_WK // N_KV)),
                  pl.BlockSpec((n_ctx, D_ATT), row(P_GQ // D_ATT)),
                  pl.BlockSpec((n_ctx, N_KV), row(P_GK // N_KV)),
                  pl.BlockSpec((2 * N_KV, n_ctx), lambda b: (0, ctx_rows + b)),
                  pl.BlockSpec((1, WIN_Q_HEADS * WINDOW), lambda b: (0, 0)),
                  any_spec, any_spec],
        out_specs=[out_spec, out_spec],
        out_shape=[jax.ShapeDtypeStruct(ywt.shape, BF16), jax.ShapeDtypeStruct(ygt.shape, BF16)],
        input_output_aliases={6: 0, 7: 1},
        compiler_params=_params(("parallel",)),
        name="ctx_attn",
    )(px, px, px, px, vt, sink_row, ywt, ygt)


def _glb_kernel(q_ref, k_ref, kc_ref, vt_ref, vtc_ref, o_ref, k_all, v_ext, *, seq, n_ctx):
    i = pl.program_id(1)
    group = GLB_Q_HEADS // GLB_KV_HEADS
    tq = q_ref.shape[0]
    w = group * tq

    @pl.when(i == 0)
    def _():
        k_all[0:seq, :] = k_ref[...]
        k_all[seq:seq + n_ctx, :] = kc_ref[...]
        _fill_values(v_ext, 0, vt_ref[...])
        _fill_values(v_ext, seq, vtc_ref[...])
        v_ext[:, HEAD_DIM:2 * HEAD_DIM, :] = jnp.ones((GLB_KV_HEADS, HEAD_DIM, seq + n_ctx), BF16)

    def attend(chunks):
        q4 = _pad_heads(q_ref[...], GLB_KV_HEADS)

        def scores(chunk):
            lo, nk = chunk
            return _dot_nt(k_all[lo:lo + nk, :], q4)

        m = None
        acc = [None] * GLB_KV_HEADS

        def accumulate(pend):
            pt, alpha, (lo, nk) = pend
            for h in range(GLB_KV_HEADS):
                pv = _dot(v_ext[h, :, lo:lo + nk], pt[:, h * w:(h + 1) * w])
                acc[h] = pv if alpha is None else acc[h] * alpha[:, h * w:(h + 1) * w] + pv

        st_next = scores(chunks[0])
        pend = None
        for ci, chunk in enumerate(chunks):
            st = st_next
            if ci + 1 < len(chunks):
                st_next = scores(chunks[ci + 1])
            if pend is not None:
                accumulate(pend)
            cm = jnp.max(st, axis=0, keepdims=True)
            m_new = cm if m is None else jnp.maximum(m, cm)
            pt = jnp.exp2(st - m_new).astype(BF16)
            alpha = None if m is None else jnp.exp2(m - m_new)
            pend = (pt, alpha, chunk)
            m = m_new
        accumulate(pend)
        for h in range(GLB_KV_HEADS):
            _store_heads(o_ref, 0, h, group, acc[h], acc[h][HEAD_DIM:HEAD_DIM + 1])

    attend([(c * KC_G, KC_G) for c in range(seq // KC_G)] + [(seq, n_ctx)])


def _global(px, vt, nb, seq, n_ctx, with_ctx_cols):
    ta = px.shape[0] if with_ctx_cols else nb * seq
    assert seq % KC_G == 0 and seq % TQ_G == 0
    n_q = seq // TQ_G
    ctx_rows = (nb * seq) // n_ctx
    return pl.pallas_call(
        functools.partial(_glb_kernel, seq=seq, n_ctx=n_ctx),
        grid=(nb, n_q),
        in_specs=[pl.BlockSpec((TQ_G, D_ATT), lambda b, i: (b * n_q + i, P_GQ // D_ATT)),
                  pl.BlockSpec((seq, N_KV), lambda b, i: (b, P_GK // N_KV)),
                  pl.BlockSpec((n_ctx, N_KV), lambda b, i: (ctx_rows + b, P_GK // N_KV)),
                  pl.BlockSpec((N_KV, seq), lambda b, i: (1, b)),
                  pl.BlockSpec((N_KV, n_ctx), lambda b, i: (1, ctx_rows + b))],
        out_specs=pl.BlockSpec((D_ATT, TQ_G), lambda b, i: (0, b * n_q + i)),
        out_shape=jax.ShapeDtypeStruct((D_ATT, ta), BF16),
        scratch_shapes=[pltpu.VMEM((seq + n_ctx, N_KV), BF16),
                        pltpu.VMEM((GLB_KV_HEADS, 2 * HEAD_DIM, seq + n_ctx), BF16)],
        compiler_params=_params(("parallel", "arbitrary")),
        name="global_attn",
    )(px, px, px, vt, vt)


def _route_rows(logits, bias):
    sig = jax.nn.sigmoid(logits)
    biased = sig + bias
    b_rows = [biased[e:e + 1, :] for e in range(N_EXPERTS)]
    s_rows = [sig[e:e + 1, :] for e in range(N_EXPERTS)]
    n_loc = EXPERTS_PER_GROUP

    best_score, grp = None, None
    for g in range(N_GROUPS):
        a = b_rows[g * n_loc:(g + 1) * n_loc]
        top2 = None
        for i in range(n_loc):
            for j in range(i + 1, n_loc):
                pair = a[i] + a[j]
                top2 = pair if top2 is None else jnp.maximum(top2, pair)
        if g == 0:
            best_score, grp = top2, jnp.zeros(top2.shape, jnp.int32)
        else:
            upd = top2 > best_score
            grp = jnp.where(upd, g, grp)
            best_score = jnp.where(upd, top2, best_score)

    def pick(rows, i):
        out = rows[i]
        for g in range(1, N_GROUPS):
            out = jnp.where(grp == g, rows[g * n_loc + i], out)
        return out

    cand = [pick(b_rows, i) for i in range(n_loc)]
    cand_s = [pick(s_rows, i) for i in range(n_loc)]
    m1, l1, w1 = cand[0], jnp.zeros(grp.shape, jnp.int32), cand_s[0]
    for i in range(1, n_loc):
        upd = cand[i] > m1
        m1 = jnp.where(upd, cand[i], m1)
        l1 = jnp.where(upd, i, l1)
        w1 = jnp.where(upd, cand_s[i], w1)
    m2, l2, w2 = None, None, None
    for i in range(n_loc):
        rest = jnp.where(l1 == i, -jnp.inf, cand[i])
        if i == 0:
            m2, l2, w2 = rest, jnp.zeros(grp.shape, jnp.int32), cand_s[0]
        else:
            upd = rest > m2
            m2 = jnp.where(upd, rest, m2)
            l2 = jnp.where(upd, i, l2)
            w2 = jnp.where(upd, cand_s[i], w2)
    tot = w1 + w2
    e_idx = jnp.concatenate([grp * n_loc + l1, grp * n_loc + l2], axis=0)
    gate = jnp.concatenate([w1 / tot, w2 / tot], axis=0)
    return e_idx, gate


def _outproj_kernel(of_ref, ob_ref, g_ref, ywt_ref, ygt_ref, x_ref, mod_ref, n2_ref, wf_ref, wr_ref, br_ref,
                    xo_ref, h_ref, e_ref, gate_ref, w_ref):
    @pl.when(pl.program_id(0) == 0)
    def _():
        w_ref[...] = wf_ref[...].astype(BF16)

    def ret_out(r):
        o = of_ref[r, :].astype(F32) + ob_ref[r, :].astype(F32)
        normed = []
        for h in range(RET_HEADS):
            oh = o[:, h * RET_DV:(h + 1) * RET_DV]
            mu = jnp.mean(oh, axis=-1, keepdims=True)
            var = jnp.mean(jnp.square(oh - mu), axis=-1, keepdims=True)
            normed.append((oh - mu) * lax.rsqrt(var + EPS))
        return (_silu(g_ref[r, :].astype(F32)) * jnp.concatenate(normed, axis=-1)).astype(BF16)

    def project(r, yr):
        acc = _dot_tn(ywt_ref[:, r], w_ref[D_RET:D_RET + D_ATT, :])
        acc += _dot_tn(ygt_ref[:, r], w_ref[D_RET + D_ATT:D_RET + 2 * D_ATT, :])
        return acc + _dot(yr, w_ref[0:D_RET, :])

    def residual_norm(r, acc):
        x = x_ref[r, :] + mod_ref[0, 2:3, :] * acc
        xo_ref[r, :] = x
        ms = jnp.mean(x * x, axis=-1, keepdims=True)
        h2 = x * lax.rsqrt(ms + EPS) * n2_ref[...]
        h2 = (h2 * (1.0 + mod_ref[0, 4:5, :]) + mod_ref[0, 3:4, :]).astype(BF16)
        h_ref[r, :] = h2
        return h2

    def route(r, h2):
        e_idx, gate = _route_rows(_dot_nt(wr_ref[...], h2), br_ref[...])
        e_ref[:, r] = e_idx
        gate_ref[:, r] = gate

    half = x_ref.shape[0] // 2
    ra, rb = slice(0, half), slice(half, 2 * half)
    yr_a = ret_out(ra)
    acc_a = project(ra, yr_a)
    yr_b = ret_out(rb)
    h2_a = residual_norm(ra, acc_a)
    acc_b = project(rb, yr_b)
    route(ra, h2_a)
    h2_b = residual_norm(rb, acc_b)
    route(rb, h2_b)


def _outproj(o_f, o_b, px, yw, yg, xa, mod_l, n2, w_out, layer, wr_t, b_r, nb, n_x_tiles, tiles_per_seq, n_tiles):
    ta, d = xa.shape
    once = pl.Buffered(1)

    def mod_idx(i):
        return (jnp.where(i < n_x_tiles, i // tiles_per_seq, nb), 0, 0)

    row = lambda i: (i, 0)
    return pl.pallas_call(
        _outproj_kernel,
        grid=(n_tiles,),
        in_specs=[pl.BlockSpec((TM, D_RET), row),
                  pl.BlockSpec((TM, D_RET), row),
                  pl.BlockSpec((TM, D_RET), lambda i: (i, P_RG // D_RET)),
                  pl.BlockSpec((D_ATT, TM), lambda i: (0, i)),
                  pl.BlockSpec((D_ATT, TM), lambda i: (0, i)),
                  pl.BlockSpec((TM, d), row),
                  pl.BlockSpec((1, 6, d), mod_idx),
                  pl.BlockSpec((1, d), lambda i: (0, 0)),
                  pl.BlockSpec((None, d, d), lambda i: (layer, 0, 0), pipeline_mode=once),
                  pl.BlockSpec((N_EXPERTS, d), lambda i: (0, 0)),
                  pl.BlockSpec((N_EXPERTS, 1), lambda i: (0, 0))],
        out_specs=[pl.BlockSpec((TM, d), row),
                   pl.BlockSpec((TM, d), row),
                   pl.BlockSpec((TOP_K, TM), lambda i: (0, i)),
                   pl.BlockSpec((TOP_K, TM), lambda i: (0, i))],
        out_shape=[jax.ShapeDtypeStruct((ta, d), F32),
                   jax.ShapeDtypeStruct((n_tiles * TM, d), BF16),
                   jax.ShapeDtypeStruct((TOP_K, n_tiles * TM), jnp.int32),
                   jax.ShapeDtypeStruct((TOP_K, n_tiles * TM), F32)],
        scratch_shapes=[pltpu.VMEM((d, d), BF16)],
        input_output_aliases={5: 0},
        compiler_params=_params(("arbitrary",)),
        name="out_proj",
    )(o_f, o_b, px, yw, yg, xa, mod_l, n2, w_out, wr_t, b_r)


def _moe_kernel(be_ref, first_ref, nu_ref, x_ref, wgu_f_ref, wd_f_ref, o_ref, wgu_ref, wd_ref):
    i = pl.program_id(0)

    @pl.when(jnp.logical_and(i < nu_ref[0], first_ref[i] == 1))
    def _():
        wgu_ref[...] = wgu_f_ref[...].astype(BF16)
        wd_ref[...] = wd_f_ref[...].astype(BF16)

    @pl.when(i < nu_ref[0])
    def _():
        f = wd_ref.shape[0]
        au = _dot(x_ref[...], wgu_ref[...])
        mid = (_silu(au[:, :f]) * au[:, f:]).astype(BF16)
        o_ref[...] = _dot(mid, wd_ref[...]).astype(BF16)

    @pl.when(i >= nu_ref[0])
    def _():
        o_ref[...] = jnp.zeros_like(o_ref)


def _moe_ffn(blk_e, first, n_used, xs, w_gu, w_down, layer):
    rows, d = xs.shape
    f2 = w_gu.shape[3]
    f = w_down.shape[2]
    grid_spec = pltpu.PrefetchScalarGridSpec(
        num_scalar_prefetch=3,
        grid=(rows // TM_E,),
        in_specs=[pl.BlockSpec((TM_E, d), lambda i, be, fi, nu: (i, 0)),
                  pl.BlockSpec((None, None, d, f2), lambda i, be, fi, nu: (layer, be[i], 0, 0)),
                  pl.BlockSpec((None, None, f, d), lambda i, be, fi, nu: (layer, be[i], 0, 0))],
        out_specs=pl.BlockSpec((TM_E, d), lambda i, be, fi, nu: (i, 0)),
        scratch_shapes=[pltpu.VMEM((d, f2), BF16), pltpu.VMEM((f, d), BF16)],
    )
    return pl.pallas_call(
        _moe_kernel,
        grid_spec=grid_spec,
        out_shape=jax.ShapeDtypeStruct((rows, d), BF16),
        compiler_params=_params(("arbitrary",)),
        name="moe_ffn",
    )(blk_e, first, n_used, xs, w_gu, w_down)


def _moe_plan(e_idx_t, n_tok):
    n_asg = n_tok * TOP_K
    flat_e = e_idx_t[:, :n_tok].reshape(-1)
    onehot = (flat_e[:, None] == jnp.arange(N_EXPERTS, dtype=jnp.int32)[None, :]).astype(jnp.int32)
    csum = jnp.cumsum(onehot, axis=0)
    counts = csum[-1]
    rank = jnp.sum(csum * onehot, axis=1) - 1
    padded = (counts + TM_E - 1) // TM_E * TM_E
    pad_end = jnp.cumsum(padded)
    pad_start = pad_end - padded
    cnt_start = jnp.cumsum(counts) - counts
    pos = (jnp.sum(pad_start[None, :] * onehot, axis=1) + rank).reshape(TOP_K, n_tok)
    n_blocks = (n_asg + N_EXPERTS * (TM_E - 1) + TM_E - 1) // TM_E
    blk_start = jnp.arange(n_blocks, dtype=jnp.int32) * TM_E
    blk_e = jnp.minimum(jnp.sum(blk_start[:, None] >= pad_end[None, :], axis=1), N_EXPERTS - 1).astype(jnp.int32)
    first = jnp.concatenate([jnp.ones((1,), jnp.int32), (blk_e[1:] != blk_e[:-1]).astype(jnp.int32)])
    n_used = (pad_end[-1] // TM_E).astype(jnp.int32).reshape(1)
    order = jnp.argsort(flat_e)
    e_row = jnp.repeat(blk_e, TM_E)
    p = jnp.arange(n_blocks * TM_E, dtype=jnp.int32)
    r = p - pad_start[e_row]
    src = jnp.where(r < counts[e_row], order[jnp.clip(cnt_start[e_row] + r, 0, n_asg - 1)], p) % n_tok
    return (blk_e, first, n_used), src, pos


def _combine_kernel(x_ref, y0_ref, y1_ref, gate_ref, mod_ref, o_ref):
    g = gate_ref[...]
    y = y0_ref[...].astype(F32) * g[:, 0:1] + y1_ref[...].astype(F32) * g[:, 1:2]
    o_ref[...] = x_ref[...] + mod_ref[0, 5:6, :] * y


def _combine_final_kernel(x_ref, y0_ref, y1_ref, gate_ref, mod_ref, prev_ref, o_ref):
    del prev_ref
    _combine_kernel(x_ref, y0_ref, y1_ref, gate_ref, mod_ref, o_ref)


def _combine_final(xa, y0, y1, gate, mod_l, tiles_per_seq, n_tiles, prev, tile_off, total_rows):
    d = xa.shape[1]
    row = lambda i: (i, 0)
    ins = [pl.BlockSpec((TM, d), row), pl.BlockSpec((TM, d), row), pl.BlockSpec((TM, d), row),
           pl.BlockSpec((TM, TOP_K), row), pl.BlockSpec((1, 6, d), lambda i: (i // tiles_per_seq, 0, 0))]
    args = [xa, y0, y1, gate, mod_l]
    if prev is not None:
        ins.append(pl.BlockSpec(memory_space=pl.ANY))
        args.append(prev)
    return pl.pallas_call(
        _combine_kernel if prev is None else _combine_final_kernel,
        grid=(n_tiles,),
        in_specs=ins,
        out_specs=pl.BlockSpec((TM, d), lambda i: (tile_off + i, 0)),
        out_shape=jax.ShapeDtypeStruct((total_rows, d), F32),
        input_output_aliases={} if prev is None else {5: 0},
        compiler_params=_params(("parallel",)),
        name="moe_combine_out",
    )(*args)


def _rope_tables(seq, dtype):
    rows = seq // GRID_W
    row = jnp.repeat(jnp.arange(rows), GRID_W).astype(jnp.float32)
    col = (jnp.arange(rows * GRID_W) % GRID_W).astype(jnp.float32)
    half = HEAD_DIM // 2
    inv = jnp.power(ROPE_BASE, -jnp.arange(0, half, 2, dtype=jnp.float32) / half)
    ang_r, ang_c = row[:, None] * inv, col[:, None] * inv
    cos_r, cos_c = jnp.cos(ang_r).astype(dtype), jnp.cos(ang_c).astype(dtype)
    sin_r, sin_c = jnp.sin(ang_r).astype(dtype), jnp.sin(ang_c).astype(dtype)
    cos_h = jnp.concatenate([cos_r, cos_r, cos_c, cos_c], -1)
    sin_h = jnp.concatenate([-sin_r, sin_r, -sin_c, sin_c], -1)
    cos_t = jnp.concatenate([jnp.tile(cos_h, (1, D_ATT // HEAD_DIM)), jnp.ones((TM, D_ATT), dtype)], 0)
    sin_t = jnp.concatenate([jnp.tile(sin_h, (1, D_ATT // HEAD_DIM)), jnp.zeros((TM, D_ATT), dtype)], 0)
    return cos_t, sin_t


def kernel(x, c, ctx, c_ctx, ada_w, ada_b, norm1, norm2, w_in, w_out, ret_decay, win_qk_gain, win_sink,
           glb_qk_gain, w_router, b_router, w_gate_up, w_down):
    nb_all, seq, d = x.shape
    n_ctx = ctx.shape[1]
    depth = ada_w.shape[0]
    n_streams = N_STREAMS if nb_all % N_STREAMS == 0 and (nb_all // N_STREAMS * n_ctx) % TM == 0 else 1
    nb = nb_all // n_streams
    n_x = nb * seq
    ta = n_x + nb * n_ctx
    assert seq % TM == 0 and (nb * n_ctx) % TM == 0 and seq % GRID_W == 0
    tiles_per_seq = seq // TM
    n_x_tiles = n_x // TM
    n_tiles = ta // TM

    mod_rows = (nb_all + 1 + 7) // 8 * 8
    cc = jnp.zeros((mod_rows, d), F32).at[:nb_all].set(c).at[nb_all].set(c_ctx)
    mod_all = _ada(cc, ada_w, ada_b).reshape(depth, mod_rows, 6, d)

    cos_t, sin_t = _rope_tables(seq, F32)
    head_of = jnp.arange(D_ATT) // HEAD_DIM
    hm = jnp.where(head_of[:, None] == head_of[None, :], 1.0 / HEAD_DIM, 0.0).astype(BF16)
    wr_t = w_router.T.astype(BF16)
    b_r = b_router.astype(F32)[:, None]
    key_j = jnp.arange(WINDOW)[:, None]
    qry_i = jnp.tile(jnp.arange(WINDOW), WIN_Q_HEADS)[None, :]
    tri = jnp.stack([jnp.where(key_j >= qry_i, 0.0, NEG_INF), jnp.where(key_j <= qry_i, 0.0, NEG_INF)]).astype(F32)

    layer_tabs = []
    for l in range(depth):
        gains = jnp.zeros((8, D_ATT), F32)
        gains = gains.at[0].set(jnp.tile(win_qk_gain[l, 0], WIN_Q_HEADS))
        gains = gains.at[1, :WIN_KV_HEADS * HEAD_DIM].set(jnp.tile(win_qk_gain[l, 1], WIN_KV_HEADS))
        gains = gains.at[2].set(jnp.tile(glb_qk_gain[l, 0], GLB_Q_HEADS))
        gains = gains.at[3, :GLB_KV_HEADS * HEAD_DIM].set(jnp.tile(glb_qk_gain[l, 1], GLB_KV_HEADS))
        sink_row = jnp.repeat(win_sink[l].astype(F32) * LOG2_E, WINDOW)[None, :]
        layer_tabs.append((gains, sink_row, _ret_tables(ret_decay[l])))

    def mix(l, s, xa, pending):
        last = l == depth - 1
        gains, sink_row, (gc, dmat, xi, zeta) = layer_tabs[l]
        xa, px, vt = _inproj(xa, mods[s][l], norm1[l][None, :], w_in, l, cos_t, sin_t, gains, hm,
                             nb, n_x_tiles, tiles_per_seq, pending)
        o_f, o_b = _retention(px, gc, dmat, xi, zeta, nb, seq, n_ctx)
        yw = _window(px, vt, sink_row, tri, nb, seq, n_ctx, not last)
        yg = _global(px, vt, nb, seq, n_ctx, not last)
        if not last:
            yw, yg = _ctx_attention(px, vt, sink_row, yw, yg, nb, seq, n_ctx)
        return _outproj(o_f, o_b, px, yw, yg, xa, mods[s][l], norm2[l][None, :], w_out, l,
                        wr_t, b_r, nb, n_x_tiles, tiles_per_seq, n_x_tiles if last else n_tiles)

    def tie(a, b):
        if n_streams == 1:
            return a, b
        return lax.optimization_barrier((a, b))

    mods, xas = [], []
    for s in range(n_streams):
        b0 = s * nb
        mods.append(jnp.concatenate([mod_all[:, b0:b0 + nb], mod_all[:, nb_all:nb_all + 1]], axis=1))
        xas.append(jnp.concatenate([x[b0:b0 + nb].reshape(n_x, d), ctx[b0:b0 + nb].reshape(nb * n_ctx, d)], 0))

    out = None
    pending = [None] * n_streams
    carry = None
    s_last = n_streams - 1

    def gather_back(yp, pos):
        return yp[pos[0]], yp[pos[1]]

    for l in range(depth):
        last = l == depth - 1
        n_tok = n_x if last else ta
        states, srcs, plans, xss = [], [], [], []
        for s in range(n_streams):
            xin = xas[s]
            due = None
            if s > 0:
                xin, srcs[s - 1] = tie(xin, srcs[s - 1])
                xss.append(states[s - 1][1][srcs[s - 1]])
            elif carry is not None:
                due, yp_c, pos_c, gate_c, mod_c = carry
                xin, yp_c = tie(xin, yp_c)
                pending[due] = gather_back(yp_c, pos_c) + (gate_c, mod_c)
                carry = None
            state = list(mix(l, s, xin, pending[s]))
            if s > 0:
                state[2], xss[s - 1] = tie(state[2], xss[s - 1])
            elif due is not None:
                y0d, y1d, gate_d, mod_d = pending[due]
                state[2], (y0d, y1d) = tie(state[2], (y0d, y1d))
                pending[due] = (y0d, y1d, gate_d, mod_d)
            states.append(state)
            tables, src, pos = _moe_plan(state[2], n_tok)
            plans.append((tables, pos))
            srcs.append(src)
        yp_prev = None
        for s in range(n_streams):
            xa, h2, _, gate_t = states[s]
            tables, pos = plans[s]
            xs = xss[s] if s < s_last else h2[srcs[s]]
            if s < s_last:
                xs, srcs[s + 1] = tie(xs, srcs[s + 1])
            if s > 0:
                xs, yp_prev = tie(xs, yp_prev)
                ps, ppos, pgate, pmod, pxa = prev_info
                done = gather_back(yp_prev, ppos) + (pgate, pmod)
                if last:
                    out = _combine_final(pxa, *done, tiles_per_seq, n_x_tiles, out, ps * n_x_tiles, n_streams * n_x)
                else:
                    pending[ps] = done
            yp_prev = _moe_ffn(*tables, xs, w_gate_up, w_down, l)
            prev_info = (s, pos, gate_t.T, mods[s][l], xa)
            xas[s] = xa
        ps, ppos, pgate, pmod, pxa = prev_info
        if last:
            out = _combine_final(pxa, *gather_back(yp_prev, ppos), pgate, pmod, tiles_per_seq, n_x_tiles, out,
                                 ps * n_x_tiles, n_streams * n_x)
        elif n_streams == 1:
            pending[ps] = gather_back(yp_prev, ppos) + (pgate, pmod)
        else:
            carry = (ps, yp_prev, ppos, pgate, pmod)
    return out.reshape(nb_all, seq, d)
```

```python
import functools

import jax
import jax.numpy as jnp
from jax import lax
from jax.experimental import pallas as pl
from jax.experimental.pallas import tpu as pltpu

GRID_W = 64
HEAD_DIM = 64
RET_HEADS = 4
RET_DK = 64
RET_DV = 128
WIN_Q_HEADS = 4
WIN_KV_HEADS = 2
WINDOW = 128
GLB_Q_HEADS = 4
GLB_KV_HEADS = 2
ROPE_BASE = 10000.0
N_EXPERTS = 16
N_GROUPS = 4
EXPERTS_PER_GROUP = N_EXPERTS // N_GROUPS
TOP_K = 2
EPS = 1e-6
NEG_INF = -1e30

RQ, RK, RV, RG = 0, 256, 512, 1024
WQ, WK, WV = 1536, 1792, 1920
GQ, GK, GV = 2048, 2304, 2432
D_IN = 2560
D_RET = RET_HEADS * RET_DV
D_ATT = WIN_Q_HEADS * HEAD_DIM
N_KV = WIN_KV_HEADS * HEAD_DIM
P_RQ, P_RK, P_RV, P_RG = 0, 256, 512, 1024
P_WQ, P_GQ, P_WK, P_GK = 1536, 1792, 2048, 2176
PX_W = 2304

TM = 512
RET_C = 256
TQ_W = 512
TQ_G = 256
KC_G = 256
LOG2_E = 1.4426950408889634
assert GLB_KV_HEADS == 2
TM_E = 512
N_STREAMS = 2
VMEM_LIMIT = 56 * 1024 * 1024

F32 = jnp.float32
BF16 = jnp.bfloat16


def _dot(a, b):
    return jnp.dot(a, b, preferred_element_type=F32)


def _dot_nt(a, b):
    return lax.dot_general(a, b, (((1,), (1,)), ((), ())), preferred_element_type=F32)


def _dot_tn(a, b):
    return lax.dot_general(a, b, (((0,), (0,)), ((), ())), preferred_element_type=F32)


def _silu(x):
    return x * jax.nn.sigmoid(x)


def _params(sem):
    return pltpu.CompilerParams(dimension_semantics=sem, vmem_limit_bytes=VMEM_LIMIT)


def _ada_kernel(c_ref, w_ref, b_ref, o_ref):
    s = _silu(c_ref[...]).astype(BF16)
    o_ref[0] = _dot(s, w_ref[0].astype(BF16)) + b_ref[0]


def _ada(cc, ada_w, ada_b):
    depth, d, n = ada_w.shape
    tn = 1536
    rows = cc.shape[0]
    return pl.pallas_call(
        _ada_kernel,
        grid=(depth, n // tn),
        in_specs=[pl.BlockSpec((rows, d), lambda l, j: (0, 0)),
                  pl.BlockSpec((1, d, tn), lambda l, j: (l, 0, j)),
                  pl.BlockSpec((1, 1, tn), lambda l, j: (l, 0, j))],
        out_specs=pl.BlockSpec((1, rows, tn), lambda l, j: (l, 0, j)),
        out_shape=jax.ShapeDtypeStruct((depth, rows, n), F32),
        compiler_params=_params(("arbitrary", "arbitrary")),
        name="ada_mod",
    )(cc, ada_w, ada_b.reshape(depth, 1, n))


def _inproj_fused_kernel(x_ref, y0_ref, y1_ref, gate_ref, modp_ref, mod_ref, n1_ref, wf_ref, wvt_ref, cos_ref,
                         sin_ref, gains_ref, hm_ref, xo_ref, o_ref, vt_ref, w_ref):
    g = gate_ref[...]
    y = y0_ref[...].astype(F32) * g[:, 0:1] + y1_ref[...].astype(F32) * g[:, 1:2]
    x = x_ref[...] + modp_ref[0, 5:6, :] * y
    xo_ref[...] = x
    _inproj_body(x, mod_ref, n1_ref, wf_ref, wvt_ref, cos_ref, sin_ref, gains_ref, hm_ref, o_ref, vt_ref, w_ref)


def _inproj_kernel(x_ref, mod_ref, n1_ref, wf_ref, wvt_ref, cos_ref, sin_ref, gains_ref, hm_ref, o_ref, vt_ref,
                   w_ref):
    _inproj_body(x_ref[...], mod_ref, n1_ref, wf_ref, wvt_ref, cos_ref, sin_ref, gains_ref, hm_ref, o_ref, vt_ref,
                 w_ref)


def _inproj_body(x, mod_ref, n1_ref, wf_ref, wvt_ref, cos_ref, sin_ref, gains_ref, hm_ref, o_ref, vt_ref, w_ref):
    @pl.when(pl.program_id(0) == 0)
    def _():
        w_ref[...] = wf_ref[...].astype(BF16)

    ms = jnp.mean(x * x, axis=-1, keepdims=True)
    h = x * lax.rsqrt(ms + EPS) * n1_ref[...]
    h = h * (1.0 + mod_ref[0, 1:2, :]) + mod_ref[0, 0:1, :]
    hb = h.astype(BF16)

    def proj(lo, width):
        return _dot(hb, w_ref[:, lo:lo + width])

    def head_msq(y):
        sq = y * y
        sq_hi = sq.astype(BF16)
        sq_lo = (sq - sq_hi.astype(F32)).astype(BF16)
        hm = hm_ref[0:y.shape[1], 0:y.shape[1]]
        return _dot(sq_hi, hm) + _dot(sq_lo, hm)

    def qk_finish(y, msq, gain_row, scale):
        width = y.shape[1]
        yn = y * lax.rsqrt(msq + EPS) * gains_ref[gain_row:gain_row + 1, 0:width]
        nxt = pltpu.roll(yn, width - HEAD_DIM // 4, 1)
        prv = pltpu.roll(yn, HEAD_DIM // 4, 1)
        lane = lax.broadcasted_iota(jnp.int32, yn.shape, 1)
        partner = jnp.where((lane % (HEAD_DIM // 2)) < HEAD_DIM // 4, nxt, prv)
        yr = yn * cos_ref[:, 0:width] + partner * sin_ref[:, 0:width]
        return (yr * scale).astype(BF16)

    q_scale = HEAD_DIM ** -0.5 * LOG2_E
    qk_segs = ((WQ, D_ATT, 0, q_scale, P_WQ), (GQ, D_ATT, 2, q_scale, P_GQ),
               (WK, N_KV, 1, 1.0, P_WK), (GK, N_KV, 3, 1.0, P_GK))
    plain_segs = ((RQ, RK - RQ, 1.0, P_RQ), (RK, RV - RK, RET_DK ** -0.5, P_RK),
                  (RV, RG - RV, 1.0, P_RV), (RG, WQ - RG, 1.0, P_RG))
    ys = [proj(lo, width) for lo, width, _, _, _ in qk_segs]
    stats = [head_msq(y) for y in ys]
    vt_ref[...] = _dot_nt(wvt_ref[...], hb).astype(BF16)
    for (lo, width, scale, dst), (_, qwidth, gain_row, qscale, qdst), y, msq in zip(plain_segs, qk_segs, ys, stats):
        p = proj(lo, width)
        o_ref[:, dst:dst + width] = (p if scale == 1.0 else p * scale).astype(BF16)
        o_ref[:, qdst:qdst + qwidth] = qk_finish(y, msq, gain_row, qscale)


def _inproj(xa, mod_l, n1, w_in, layer, cos_t, sin_t, gains, hm, nb, n_x_tiles, tiles_per_seq, pending=None):
    ta, d = xa.shape
    n_tiles = ta // TM
    w_vt = jnp.concatenate([w_in[layer, :, WV:WV + N_KV], w_in[layer, :, GV:GV + N_KV]], axis=1).T.astype(BF16)
    n_gv = 2 * N_KV
    once = pl.Buffered(1)

    def mod_idx(i):
        return (jnp.where(i < n_x_tiles, i // tiles_per_seq, nb), 0, 0)

    def rope_idx(i):
        return (jnp.where(i < n_x_tiles, i % tiles_per_seq, tiles_per_seq), 0)

    row = lambda i: (i, 0)
    in_specs = [pl.BlockSpec((1, 6, d), mod_idx),
                pl.BlockSpec((1, d), lambda i: (0, 0)),
                pl.BlockSpec((None, d, D_IN), lambda i: (layer, 0, 0), pipeline_mode=once),
                pl.BlockSpec((n_gv, d), lambda i: (0, 0), pipeline_mode=once),
                pl.BlockSpec((TM, D_ATT), rope_idx),
                pl.BlockSpec((TM, D_ATT), rope_idx),
                pl.BlockSpec((8, D_ATT), lambda i: (0, 0)),
                pl.BlockSpec((D_ATT, D_ATT), lambda i: (0, 0))]
    args = [mod_l, n1, w_in, w_vt, cos_t, sin_t, gains, hm]
    out_specs = [pl.BlockSpec((TM, PX_W), row), pl.BlockSpec((n_gv, TM), lambda i: (0, i))]
    out_shape = [jax.ShapeDtypeStruct((ta, PX_W), BF16), jax.ShapeDtypeStruct((n_gv, ta), BF16)]
    if pending is None:
        kern, aliases = _inproj_kernel, {}
        in_specs = [pl.BlockSpec((TM, d), row)] + in_specs
        args = [xa] + args
    else:
        y0, y1, gate, mod_prev = pending
        kern, aliases = _inproj_fused_kernel, {0: 0}
        in_specs = [pl.BlockSpec((TM, d), row), pl.BlockSpec((TM, d), row), pl.BlockSpec((TM, d), row),
                    pl.BlockSpec((TM, TOP_K), row), pl.BlockSpec((1, 6, d), mod_idx)] + in_specs
        args = [xa, y0, y1, gate, mod_prev] + args
        out_specs = [pl.BlockSpec((TM, d), row)] + out_specs
        out_shape = [jax.ShapeDtypeStruct((ta, d), F32)] + out_shape
    res = pl.pallas_call(
        kern,
        grid=(n_tiles,),
        in_specs=in_specs,
        out_specs=out_specs,
        out_shape=out_shape,
        scratch_shapes=[pltpu.VMEM((d, D_IN), BF16)],
        input_output_aliases=aliases,
        compiler_params=_params(("arbitrary",)),
        name="in_proj",
    )(*args)
    return (xa,) + tuple(res) if pending is None else tuple(res)


def _ret_kernel(gc_ref, qf_ref, kf_ref, vf_ref, qb_ref, kb_ref, vb_ref, dmat_ref, xi_ref, zeta_ref,
                of_ref, ob_ref, s_ref):
    @pl.when(pl.program_id(1) == 0)
    def _():
        s_ref[...] = jnp.zeros_like(s_ref)

    dirs = ((qf_ref, kf_ref, vf_ref, of_ref), (qb_ref, kb_ref, vb_ref, ob_ref))
    for d, (q_ref, k_ref, v_ref, o_ref) in enumerate(dirs):
        q = q_ref[...]
        k = k_ref[...]
        v = v_ref[...]
        kz = (k.astype(F32) * zeta_ref[d]).astype(BF16)
        outs = []
        for h in range(RET_HEADS):
            i = d * RET_HEADS + h
            qh = q[:, h * RET_DK:(h + 1) * RET_DK]
            kh = k[:, h * RET_DK:(h + 1) * RET_DK]
            vh = v[:, h * RET_DV:(h + 1) * RET_DV]
            att = _dot_nt(qh, kh) * dmat_ref[i]
            state = s_ref[i]
            outs.append(_dot(att.astype(BF16), vh) + _dot(qh, state.astype(BF16)) * xi_ref[i])
            s_ref[i] = gc_ref[i] * state + _dot_tn(kz[:, h * RET_DK:(h + 1) * RET_DK], vh)
        o_ref[...] = jnp.concatenate(outs, axis=-1).astype(BF16)


def _retention(px, gc, dmat, xi, zeta, nb, seq, n_ctx):
    ta = px.shape[0]
    n_x = seq // RET_C
    n_c = n_ctx // RET_C
    ctx_base = nb * n_x
    steps = n_c + n_x

    def row_f(b, c):
        return jnp.where(c < n_c, ctx_base + b * n_c + c, b * n_x + (c - n_c))

    def row_b(b, c):
        return jnp.where(c < n_c, ctx_base + b * n_c + (n_c - 1 - c), b * n_x + (steps - 1 - c))

    def spec(width, col, row):
        return pl.BlockSpec((RET_C, width), lambda b, c: (row(b, c), col))

    const3 = lambda b, c: (0, 0, 0)
    return pl.pallas_call(
        _ret_kernel,
        grid=(nb, steps),
        in_specs=[pl.BlockSpec(memory_space=pltpu.SMEM),
                  spec(256, P_RQ // 256, row_f), spec(256, P_RK // 256, row_f), spec(D_RET, P_RV // D_RET, row_f),
                  spec(256, P_RQ // 256, row_b), spec(256, P_RK // 256, row_b), spec(D_RET, P_RV // D_RET, row_b),
                  pl.BlockSpec((2 * RET_HEADS, RET_C, RET_C), const3),
                  pl.BlockSpec((2 * RET_HEADS, RET_C, RET_DV), const3),
                  pl.BlockSpec((2, RET_C, RET_HEADS * RET_DK), const3)],
        out_specs=[spec(D_RET, 0, row_f), spec(D_RET, 0, row_b)],
        out_shape=[jax.ShapeDtypeStruct((ta, D_RET), BF16)] * 2,
        scratch_shapes=[pltpu.VMEM((2 * RET_HEADS, RET_DK, RET_DV), F32)],
        compiler_params=_params(("parallel", "arbitrary")),
        name="retention",
    )(gc, px, px, px, px, px, px, dmat, xi, zeta)


def _ret_tables(decay_logit):
    log_g = jax.nn.log_sigmoid(decay_logit.astype(F32)).reshape(2 * RET_HEADS)
    idx = jnp.arange(RET_C, dtype=F32)
    diff = idx[:, None] - idx[None, :]
    lg = log_g[:, None, None]
    d_fwd = jnp.where(diff >= 0, jnp.exp(jnp.maximum(diff, 0.0) * lg), 0.0)
    d_bwd = jnp.where(diff <= 0, jnp.exp(jnp.maximum(-diff, 0.0) * lg), 0.0)
    is_bwd = (jnp.arange(2 * RET_HEADS) >= RET_HEADS)[:, None, None]
    dmat = jnp.where(is_bwd, d_bwd, d_fwd)
    pos = jnp.where(is_bwd[:, :, 0], RET_C - 1.0 - idx[None, :], idx[None, :])
    xi = jnp.exp((pos + 1.0) * log_g[:, None])
    zeta = jnp.exp((RET_C - 1.0 - pos) * log_g[:, None])
    gc = jnp.exp(RET_C * log_g)
    xi = jnp.broadcast_to(xi[:, :, None], (2 * RET_HEADS, RET_C, RET_DV))
    zeta = jnp.repeat(zeta.reshape(2, RET_HEADS, RET_C).transpose(0, 2, 1), RET_DK, axis=-1)
    return gc, dmat, xi, zeta


def _pad_heads(q, n_kv):
    rows, width = q.shape
    group = width // HEAD_DIM // n_kv
    zeros = jnp.zeros((rows, HEAD_DIM), BF16)
    out = []
    for h in range(n_kv):
        for g in range(group):
            qh = q[:, (h * group + g) * HEAD_DIM:(h * group + g + 1) * HEAD_DIM]
            out.append(jnp.concatenate([qh if hh == h else zeros for hh in range(n_kv)], axis=1))
    return jnp.concatenate(out, axis=0)


def _fill_values(v_ext, lo, vt):
    n = vt.shape[1]
    for h in range(v_ext.shape[0]):
        v_ext[h, 0:HEAD_DIM, lo:lo + n] = vt[h * HEAD_DIM:(h + 1) * HEAD_DIM, :]


def _store_heads(o_ref, col, h, group, oe, den):
    n = oe.shape[1] // group
    o = oe[0:HEAD_DIM] / den
    for g in range(group):
        r0 = (h * group + g) * HEAD_DIM
        o_ref[r0:r0 + HEAD_DIM, col:col + n] = o[:, g * n:(g + 1) * n].astype(BF16)


def _win_kernel(q_ref, k_ref, kc_ref, vt_ref, vtc_ref, sink_ref, tri_ref, o_ref, k_pad, v_ext, *, seq, n_ctx):
    i = pl.program_id(1)
    group = WIN_Q_HEADS // WIN_KV_HEADS
    sub = WINDOW
    w = group * sub
    span = 3 * WINDOW
    c0 = seq + 2 * WINDOW
    n_sub = q_ref.shape[0] // sub

    @pl.when(i == 0)
    def _():
        zk = jnp.zeros((WINDOW, N_KV), BF16)
        k_pad[0:WINDOW, :] = zk
        k_pad[WINDOW:WINDOW + seq, :] = k_ref[...]
        k_pad[WINDOW + seq:c0, :] = zk
        k_pad[c0:c0 + n_ctx, :] = kc_ref[...]
        v_ext[:, 0:HEAD_DIM, 0:WINDOW] = jnp.zeros((WIN_KV_HEADS, HEAD_DIM, WINDOW), BF16)
        v_ext[:, 0:HEAD_DIM, WINDOW + seq:c0] = jnp.zeros((WIN_KV_HEADS, HEAD_DIM, WINDOW), BF16)
        _fill_values(v_ext, WINDOW, vt_ref[...])
        _fill_values(v_ext, c0, vtc_ref[...])
        v_ext[:, HEAD_DIM:2 * HEAD_DIM, :] = jnp.ones((WIN_KV_HEADS, HEAD_DIM, c0 + n_ctx), BF16)

    sink = sink_ref[...]
    kc = k_pad[c0:c0 + n_ctx, :]
    start = i * q_ref.shape[0]

    def scores(a):
        qs = pl.multiple_of(start + a * sub, sub)
        q4 = _pad_heads(q_ref[a * sub:(a + 1) * sub, :], WIN_KV_HEADS)
        return a, qs, _dot_nt(k_pad[pl.ds(qs, span), :], q4), _dot_nt(kc, q4)

    def softmax(a, qs, st, sc):
        lo_edge = jnp.where(qs == 0, NEG_INF, 0.0)
        hi_edge = jnp.where(qs + sub == seq, NEG_INF, 0.0)
        s0 = st[0:WINDOW] + (tri_ref[0] + lo_edge)
        s1 = st[WINDOW:2 * WINDOW]
        s2 = st[2 * WINDOW:span] + (tri_ref[1] + hi_edge)
        m = jnp.maximum(jnp.maximum(jnp.max(s0, axis=0, keepdims=True), jnp.max(s1, axis=0, keepdims=True)),
                        jnp.maximum(jnp.max(s2, axis=0, keepdims=True), jnp.max(sc, axis=0, keepdims=True)))
        m = jnp.maximum(m, sink)
        pw = jnp.concatenate([jnp.exp2(s0 - m), jnp.exp2(s1 - m), jnp.exp2(s2 - m)], axis=0).astype(BF16)
        return a, qs, m, pw, jnp.exp2(sc - m).astype(BF16)

    def values(a, qs, m, pw, pc):
        for h in range(WIN_KV_HEADS):
            cs = slice(h * w, (h + 1) * w)
            oe = _dot(v_ext[h, :, pl.ds(qs, span)], pw[:, cs]) + _dot(v_ext[h, :, c0:c0 + n_ctx], pc[:, cs])
            den = oe[HEAD_DIM:HEAD_DIM + 1] + jnp.exp2(sink[:, cs] - m[:, cs])
            _store_heads(o_ref, a * sub, h, group, oe, den)

    nxt = scores(0)
    pend = None
    for a in range(n_sub):
        cur = nxt
        if a + 1 < n_sub:
            nxt = scores(a + 1)
        if pend is not None:
            values(*pend)
        pend = softmax(*cur)
    values(*pend)


def _window(px, vt, sink_row, tri, nb, seq, n_ctx, with_ctx_cols):
    ta = px.shape[0] if with_ctx_cols else nb * seq
    assert seq % TQ_W == 0 and TQ_W % WINDOW == 0
    n_q = seq // TQ_W
    ctx_rows = (nb * seq) // n_ctx
    n_keys = seq + 2 * WINDOW + n_ctx
    return pl.pallas_call(
        functools.partial(_win_kernel, seq=seq, n_ctx=n_ctx),
        grid=(nb, n_q),
        in_specs=[pl.BlockSpec((TQ_W, D_ATT), lambda b, i: (b * n_q + i, P_WQ // D_ATT)),
                  pl.BlockSpec((seq, N_KV), lambda b, i: (b, P_WK // N_KV)),
                  pl.BlockSpec((n_ctx, N_KV), lambda b, i: (ctx_rows + b, P_WK // N_KV)),
                  pl.BlockSpec((N_KV, seq), lambda b, i: (0, b)),
                  pl.BlockSpec((N_KV, n_ctx), lambda b, i: (0, ctx_rows + b)),
                  pl.BlockSpec((1, WIN_Q_HEADS * WINDOW), lambda b, i: (0, 0)),
                  pl.BlockSpec((2, WINDOW, WIN_Q_HEADS * WINDOW), lambda b, i: (0, 0, 0))],
        out_specs=pl.BlockSpec((D_ATT, TQ_W), lambda b, i: (0, b * n_q + i)),
        out_shape=jax.ShapeDtypeStruct((D_ATT, ta), BF16),
        scratch_shapes=[pltpu.VMEM((n_keys, N_KV), BF16),
                        pltpu.VMEM((WIN_KV_HEADS, 2 * HEAD_DIM, n_keys), BF16)],
        compiler_params=_params(("parallel", "arbitrary")),
        name="window_attn",
    )(px, px, px, vt, vt, sink_row, tri)


def _ctx_kernel(qw_ref, kw_ref, qg_ref, kg_ref, vt_ref, sink_ref, yw_in, yg_in, yw_ref, yg_ref):
    del yw_in, yg_in
    n_ctx = qw_ref.shape[0]
    sub = WINDOW
    ones = jnp.ones((HEAD_DIM, n_ctx), BF16)
    for q_ref, k_ref, row0, o_ref, sink in ((qw_ref, kw_ref, 0, yw_ref, sink_ref[...]),
                                            (qg_ref, kg_ref, N_KV, yg_ref, None)):
        n_kv = k_ref.shape[1] // HEAD_DIM
        group = q_ref.shape[1] // HEAD_DIM // n_kv
        w = group * sub
        k = k_ref[...]
        for a in range(n_ctx // sub):
            sc = _dot_nt(k, _pad_heads(q_ref[a * sub:(a + 1) * sub, :], n_kv))
            m = jnp.max(sc, axis=0, keepdims=True)
            if sink is not None:
                m = jnp.maximum(m, sink)
            p = jnp.exp2(sc - m).astype(BF16)
            for h in range(n_kv):
                cs = slice(h * w, (h + 1) * w)
                ve = jnp.concatenate([vt_ref[row0 + h * HEAD_DIM:row0 + (h + 1) * HEAD_DIM, :], ones], axis=0)
                oe = _dot(ve, p[:, cs])
                den = oe[HEAD_DIM:HEAD_DIM + 1]
                if sink is not None:
                    den = den + jnp.exp2(sink[:, cs] - m[:, cs])
                _store_heads(o_ref, a * sub, h, group, oe, den)


def _ctx_attention(px, vt, sink_row, ywt, ygt, nb, seq, n_ctx):
    ctx_rows = (nb * seq) // n_ctx
    row = lambda col: (lambda b: (ctx_rows + b, col))
    any_spec = pl.BlockSpec(memory_space=pl.ANY)
    out_spec = pl.BlockSpec((D_ATT, n_ctx), lambda b: (0, ctx_rows + b))
    return pl.pallas_call(
        _ctx_kernel,
        grid=(nb,),
        in_specs=[pl.BlockSpec((n_ctx, D_ATT), row(P_WQ // D_ATT)),
                  pl.BlockSpec((n_ctx, N_KV), row(P_WK // N_KV)),
                  pl.BlockSpec((n_ctx, D_ATT), row(P_GQ // D_ATT)),
                  pl.BlockSpec((n_ctx, N_KV), row(P_GK // N_KV)),
                  pl.BlockSpec((2 * N_KV, n_ctx), lambda b: (0, ctx_rows + b)),
                  pl.BlockSpec((1, WIN_Q_HEADS * WINDOW), lambda b: (0, 0)),
                  any_spec, any_spec],
        out_specs=[out_spec, out_spec],
        out_shape=[jax.ShapeDtypeStruct(ywt.shape, BF16), jax.ShapeDtypeStruct(ygt.shape, BF16)],
        input_output_aliases={6: 0, 7: 1},
        compiler_params=_params(("parallel",)),
        name="ctx_attn",
    )(px, px, px, px, vt, sink_row, ywt, ygt)


def _glb_kernel(q_ref, k_ref, kc_ref, vt_ref, vtc_ref, o_ref, k_all, v_ext, *, seq, n_ctx):
    i = pl.program_id(1)
    group = GLB_Q_HEADS // GLB_KV_HEADS
    tq = q_ref.shape[0]
    w = group * tq

    @pl.when(i == 0)
    def _():
        k_all[0:seq, :] = k_ref[...]
        k_all[seq:seq + n_ctx, :] = kc_ref[...]
        _fill_values(v_ext, 0, vt_ref[...])
        _fill_values(v_ext, seq, vtc_ref[...])
        v_ext[:, HEAD_DIM:2 * HEAD_DIM, :] = jnp.ones((GLB_KV_HEADS, HEAD_DIM, seq + n_ctx), BF16)

    def attend(chunks):
        q4 = _pad_heads(q_ref[...], GLB_KV_HEADS)

        def scores(chunk):
            lo, nk = chunk
            return _dot_nt(k_all[lo:lo + nk, :], q4)

        m = None
        acc = [None] * GLB_KV_HEADS

        def accumulate(pend):
            pt, alpha, (lo, nk) = pend
            for h in range(GLB_KV_HEADS):
                pv = _dot(v_ext[h, :, lo:lo + nk], pt[:, h * w:(h + 1) * w])
                acc[h] = pv if alpha is None else acc[h] * alpha[:, h * w:(h + 1) * w] + pv

        st_next = scores(chunks[0])
        pend = None
        for ci, chunk in enumerate(chunks):
            st = st_next
            if ci + 1 < len(chunks):
                st_next = scores(chunks[ci + 1])
            if pend is not None:
                accumulate(pend)
            cm = jnp.max(st, axis=0, keepdims=True)
            m_new = cm if m is None else jnp.maximum(m, cm)
            pt = jnp.exp2(st - m_new).astype(BF16)
            alpha = None if m is None else jnp.exp2(m - m_new)
            pend = (pt, alpha, chunk)
            m = m_new
        accumulate(pend)
        for h in range(GLB_KV_HEADS):
            _store_heads(o_ref, 0, h, group, acc[h], acc[h][HEAD_DIM:HEAD_DIM + 1])

    attend([(c * KC_G, KC_G) for c in range(seq // KC_G)] + [(seq, n_ctx)])


def _global(px, vt, nb, seq, n_ctx, with_ctx_cols):
    ta = px.shape[0] if with_ctx_cols else nb * seq
    assert seq % KC_G == 0 and seq % TQ_G == 0
    n_q = seq // TQ_G
    ctx_rows = (nb * seq) // n_ctx
    return pl.pallas_call(
        functools.partial(_glb_kernel, seq=seq, n_ctx=n_ctx),
        grid=(nb, n_q),
        in_specs=[pl.BlockSpec((TQ_G, D_ATT), lambda b, i: (b * n_q + i, P_GQ // D_ATT)),
                  pl.BlockSpec((seq, N_KV), lambda b, i: (b, P_GK // N_KV)),
                  pl.BlockSpec((n_ctx, N_KV), lambda b, i: (ctx_rows + b, P_GK // N_KV)),
                  pl.BlockSpec((N_KV, seq), lambda b, i: (1, b)),
                  pl.BlockSpec((N_KV, n_ctx), lambda b, i: (1, ctx_rows + b))],
        out_specs=pl.BlockSpec((D_ATT, TQ_G), lambda b, i: (0, b * n_q + i)),
        out_shape=jax.ShapeDtypeStruct((D_ATT, ta), BF16),
        scratch_shapes=[pltpu.VMEM((seq + n_ctx, N_KV), BF16),
                        pltpu.VMEM((GLB_KV_HEADS, 2 * HEAD_DIM, seq + n_ctx), BF16)],
        compiler_params=_params(("parallel", "arbitrary")),
        name="global_attn",
    )(px, px, px, vt, vt)


def _route_rows(logits, bias):
    sig = jax.nn.sigmoid(logits)
    biased = sig + bias
    b_rows = [biased[e:e + 1, :] for e in range(N_EXPERTS)]
    s_rows = [sig[e:e + 1, :] for e in range(N_EXPERTS)]
    n_loc = EXPERTS_PER_GROUP

    best_score, grp = None, None
    for g in range(N_GROUPS):
        a = b_rows[g * n_loc:(g + 1) * n_loc]
        top2 = None
        for i in range(n_loc):
            for j in range(i + 1, n_loc):
                pair = a[i] + a[j]
                top2 = pair if top2 is None else jnp.maximum(top2, pair)
        if g == 0:
            best_score, grp = top2, jnp.zeros(top2.shape, jnp.int32)
        else:
            upd = top2 > best_score
            grp = jnp.where(upd, g, grp)
            best_score = jnp.where(upd, top2, best_score)

    def pick(rows, i):
        out = rows[i]
        for g in range(1, N_GROUPS):
            out = jnp.where(grp == g, rows[g * n_loc + i], out)
        return out

    cand = [pick(b_rows, i) for i in range(n_loc)]
    cand_s = [pick(s_rows, i) for i in range(n_loc)]
    m1, l1, w1 = cand[0], jnp.zeros(grp.shape, jnp.int32), cand_s[0]
    for i in range(1, n_loc):
        upd = cand[i] > m1
        m1 = jnp.where(upd, cand[i], m1)
        l1 = jnp.where(upd, i, l1)
        w1 = jnp.where(upd, cand_s[i], w1)
    m2, l2, w2 = None, None, None
    for i in range(n_loc):
        rest = jnp.where(l1 == i, -jnp.inf, cand[i])
        if i == 0:
            m2, l2, w2 = rest, jnp.zeros(grp.shape, jnp.int32), cand_s[0]
        else:
            upd = rest > m2
            m2 = jnp.where(upd, rest, m2)
            l2 = jnp.where(upd, i, l2)
            w2 = jnp.where(upd, cand_s[i], w2)
    tot = w1 + w2
    e_idx = jnp.concatenate([grp * n_loc + l1, grp * n_loc + l2], axis=0)
    gate = jnp.concatenate([w1 / tot, w2 / tot], axis=0)
    return e_idx, gate


def _outproj_kernel(of_ref, ob_ref, g_ref, ywt_ref, ygt_ref, x_ref, mod_ref, n2_ref, wf_ref, wr_ref, br_ref,
                    xo_ref, h_ref, e_ref, gate_ref, w_ref):
    @pl.when(pl.program_id(0) == 0)
    def _():
        w_ref[...] = wf_ref[...].astype(BF16)

    def ret_out(r):
        o = of_ref[r, :].astype(F32) + ob_ref[r, :].astype(F32)
        normed = []
        for h in range(RET_HEADS):
            oh = o[:, h * RET_DV:(h + 1) * RET_DV]
            mu = jnp.mean(oh, axis=-1, keepdims=True)
            var = jnp.mean(jnp.square(oh - mu), axis=-1, keepdims=True)
            normed.append((oh - mu) * lax.rsqrt(var + EPS))
        return (_silu(g_ref[r, :].astype(F32)) * jnp.concatenate(normed, axis=-1)).astype(BF16)

    def project(r, yr):
        acc = _dot_tn(ywt_ref[:, r], w_ref[D_RET:D_RET + D_ATT, :])
        acc += _dot_tn(ygt_ref[:, r], w_ref[D_RET + D_ATT:D_RET + 2 * D_ATT, :])
        return acc + _dot(yr, w_ref[0:D_RET, :])

    def residual_norm(r, acc):
        x = x_ref[r, :] + mod_ref[0, 2:3, :] * acc
        xo_ref[r, :] = x
        ms = jnp.mean(x * x, axis=-1, keepdims=True)
        h2 = x * lax.rsqrt(ms + EPS) * n2_ref[...]
        h2 = (h2 * (1.0 + mod_ref[0, 4:5, :]) + mod_ref[0, 3:4, :]).astype(BF16)
        h_ref[r, :] = h2
        return h2

    def route(r, h2):
        e_idx, gate = _route_rows(_dot_nt(wr_ref[...], h2), br_ref[...])
        e_ref[:, r] = e_idx
        gate_ref[:, r] = gate

    half = x_ref.shape[0] // 2
    ra, rb = slice(0, half), slice(half, 2 * half)
    yr_a = ret_out(ra)
    acc_a = project(ra, yr_a)
    yr_b = ret_out(rb)
    h2_a = residual_norm(ra, acc_a)
    acc_b = project(rb, yr_b)
    route(ra, h2_a)
    h2_b = residual_norm(rb, acc_b)
    route(rb, h2_b)


def _outproj(o_f, o_b, px, yw, yg, xa, mod_l, n2, w_out, layer, wr_t, b_r, nb, n_x_tiles, tiles_per_seq, n_tiles):
    ta, d = xa.shape
    once = pl.Buffered(1)

    def mod_idx(i):
        return (jnp.where(i < n_x_tiles, i // tiles_per_seq, nb), 0, 0)

    row = lambda i: (i, 0)
    return pl.pallas_call(
        _outproj_kernel,
        grid=(n_tiles,),
        in_specs=[pl.BlockSpec((TM, D_RET), row),
                  pl.BlockSpec((TM, D_RET), row),
                  pl.BlockSpec((TM, D_RET), lambda i: (i, P_RG // D_RET)),
                  pl.BlockSpec((D_ATT, TM), lambda i: (0, i)),
                  pl.BlockSpec((D_ATT, TM), lambda i: (0, i)),
                  pl.BlockSpec((TM, d), row),
                  pl.BlockSpec((1, 6, d), mod_idx),
                  pl.BlockSpec((1, d), lambda i: (0, 0)),
                  pl.BlockSpec((None, d, d), lambda i: (layer, 0, 0), pipeline_mode=once),
                  pl.BlockSpec((N_EXPERTS, d), lambda i: (0, 0)),
                  pl.BlockSpec((N_EXPERTS, 1), lambda i: (0, 0))],
        out_specs=[pl.BlockSpec((TM, d), row),
                   pl.BlockSpec((TM, d), row),
                   pl.BlockSpec((TOP_K, TM), lambda i: (0, i)),
                   pl.BlockSpec((TOP_K, TM), lambda i: (0, i))],
        out_shape=[jax.ShapeDtypeStruct((ta, d), F32),
                   jax.ShapeDtypeStruct((n_tiles * TM, d), BF16),
                   jax.ShapeDtypeStruct((TOP_K, n_tiles * TM), jnp.int32),
                   jax.ShapeDtypeStruct((TOP_K, n_tiles * TM), F32)],
        scratch_shapes=[pltpu.VMEM((d, d), BF16)],
        input_output_aliases={5: 0},
        compiler_params=_params(("arbitrary",)),
        name="out_proj",
    )(o_f, o_b, px, yw, yg, xa, mod_l, n2, w_out, wr_t, b_r)


def _moe_kernel(be_ref, first_ref, nu_ref, x_ref, wgu_f_ref, wd_f_ref, o_ref, wgu_ref, wd_ref):
    i = pl.program_id(0)

    @pl.when(jnp.logical_and(i < nu_ref[0], first_ref[i] == 1))
    def _():
        wgu_ref[...] = wgu_f_ref[...].astype(BF16)
        wd_ref[...] = wd_f_ref[...].astype(BF16)

    @pl.when(i < nu_ref[0])
    def _():
        f = wd_ref.shape[0]
        au = _dot(x_ref[...], wgu_ref[...])
        mid = (_silu(au[:, :f]) * au[:, f:]).astype(BF16)
        o_ref[...] = _dot(mid, wd_ref[...]).astype(BF16)

    @pl.when(i >= nu_ref[0])
    def _():
        o_ref[...] = jnp.zeros_like(o_ref)


def _moe_ffn(blk_e, first, n_used, xs, w_gu, w_down, layer):
    rows, d = xs.shape
    f2 = w_gu.shape[3]
    f = w_down.shape[2]
    grid_spec = pltpu.PrefetchScalarGridSpec(
        num_scalar_prefetch=3,
        grid=(rows // TM_E,),
        in_specs=[pl.BlockSpec((TM_E, d), lambda i, be, fi, nu: (i, 0)),
                  pl.BlockSpec((None, None, d, f2), lambda i, be, fi, nu: (layer, be[i], 0, 0)),
                  pl.BlockSpec((None, None, f, d), lambda i, be, fi, nu: (layer, be[i], 0, 0))],
        out_specs=pl.BlockSpec((TM_E, d), lambda i, be, fi, nu: (i, 0)),
        scratch_shapes=[pltpu.VMEM((d, f2), BF16), pltpu.VMEM((f, d), BF16)],
    )
    return pl.pallas_call(
        _moe_kernel,
        grid_spec=grid_spec,
        out_shape=jax.ShapeDtypeStruct((rows, d), BF16),
        compiler_params=_params(("arbitrary",)),
        name="moe_ffn",
    )(blk_e, first, n_used, xs, w_gu, w_down)


def _moe_plan(e_idx_t, n_tok):
    n_asg = n_tok * TOP_K
    flat_e = e_idx_t[:, :n_tok].reshape(-1)
    onehot = (flat_e[:, None] == jnp.arange(N_EXPERTS, dtype=jnp.int32)[None, :]).astype(jnp.int32)
    csum = jnp.cumsum(onehot, axis=0)
    counts = csum[-1]
    rank = jnp.sum(csum * onehot, axis=1) - 1
    padded = (counts + TM_E - 1) // TM_E * TM_E
    pad_end = jnp.cumsum(padded)
    pad_start = pad_end - padded
    cnt_start = jnp.cumsum(counts) - counts
    pos = (jnp.sum(pad_start[None, :] * onehot, axis=1) + rank).reshape(TOP_K, n_tok)
    n_blocks = (n_asg + N_EXPERTS * (TM_E - 1) + TM_E - 1) // TM_E
    blk_start = jnp.arange(n_blocks, dtype=jnp.int32) * TM_E
    blk_e = jnp.minimum(jnp.sum(blk_start[:, None] >= pad_end[None, :], axis=1), N_EXPERTS - 1).astype(jnp.int32)
    first = jnp.concatenate([jnp.ones((1,), jnp.int32), (blk_e[1:] != blk_e[:-1]).astype(jnp.int32)])
    n_used = (pad_end[-1] // TM_E).astype(jnp.int32).reshape(1)
    bits = max(1, (n_asg - 1).bit_length())
    order = jnp.sort((flat_e << bits) | jnp.arange(n_asg, dtype=jnp.int32)) & ((1 << bits) - 1)
    e_row = jnp.repeat(blk_e, TM_E)
    p = jnp.arange(n_blocks * TM_E, dtype=jnp.int32)
    r = p - pad_start[e_row]
    src = jnp.where(r < counts[e_row], order[jnp.clip(cnt_start[e_row] + r, 0, n_asg - 1)], p) % n_tok
    return (blk_e, first, n_used), src, pos


def _combine_kernel(x_ref, y0_ref, y1_ref, gate_ref, mod_ref, o_ref):
    g = gate_ref[...]
    y = y0_ref[...].astype(F32) * g[:, 0:1] + y1_ref[...].astype(F32) * g[:, 1:2]
    o_ref[...] = x_ref[...] + mod_ref[0, 5:6, :] * y


def _combine_final_kernel(x_ref, y0_ref, y1_ref, gate_ref, mod_ref, prev_ref, o_ref):
    del prev_ref
    _combine_kernel(x_ref, y0_ref, y1_ref, gate_ref, mod_ref, o_ref)


def _combine_final(xa, y0, y1, gate, mod_l, tiles_per_seq, n_tiles, prev, tile_off, total_rows):
    d = xa.shape[1]
    row = lambda i: (i, 0)
    ins = [pl.BlockSpec((TM, d), row), pl.BlockSpec((TM, d), row), pl.BlockSpec((TM, d), row),
           pl.BlockSpec((TM, TOP_K), row), pl.BlockSpec((1, 6, d), lambda i: (i // tiles_per_seq, 0, 0))]
    args = [xa, y0, y1, gate, mod_l]
    if prev is not None:
        ins.append(pl.BlockSpec(memory_space=pl.ANY))
        args.append(prev)
    return pl.pallas_call(
        _combine_kernel if prev is None else _combine_final_kernel,
        grid=(n_tiles,),
        in_specs=ins,
        out_specs=pl.BlockSpec((TM, d), lambda i: (tile_off + i, 0)),
        out_shape=jax.ShapeDtypeStruct((total_rows, d), F32),
        input_output_aliases={} if prev is None else {5: 0},
        compiler_params=_params(("parallel",)),
        name="moe_combine_out",
    )(*args)


def _rope_tables(seq, dtype):
    rows = seq // GRID_W
    row = jnp.repeat(jnp.arange(rows), GRID_W).astype(jnp.float32)
    col = (jnp.arange(rows * GRID_W) % GRID_W).astype(jnp.float32)
    half = HEAD_DIM // 2
    inv = jnp.power(ROPE_BASE, -jnp.arange(0, half, 2, dtype=jnp.float32) / half)
    ang_r, ang_c = row[:, None] * inv, col[:, None] * inv
    cos_r, cos_c = jnp.cos(ang_r).astype(dtype), jnp.cos(ang_c).astype(dtype)
    sin_r, sin_c = jnp.sin(ang_r).astype(dtype), jnp.sin(ang_c).astype(dtype)
    cos_h = jnp.concatenate([cos_r, cos_r, cos_c, cos_c], -1)
    sin_h = jnp.concatenate([-sin_r, sin_r, -sin_c, sin_c], -1)
    cos_t = jnp.concatenate([jnp.tile(cos_h, (1, D_ATT // HEAD_DIM)), jnp.ones((TM, D_ATT), dtype)], 0)
    sin_t = jnp.concatenate([jnp.tile(sin_h, (1, D_ATT // HEAD_DIM)), jnp.zeros((TM, D_ATT), dtype)], 0)
    return cos_t, sin_t


def kernel(x, c, ctx, c_ctx, ada_w, ada_b, norm1, norm2, w_in, w_out, ret_decay, win_qk_gain, win_sink,
           glb_qk_gain, w_router, b_router, w_gate_up, w_down):
    nb_all, seq, d = x.shape
    n_ctx = ctx.shape[1]
    depth = ada_w.shape[0]
    n_streams = N_STREAMS if nb_all % N_STREAMS == 0 and (nb_all // N_STREAMS * n_ctx) % TM == 0 else 1
    nb = nb_all // n_streams
    n_x = nb * seq
    ta = n_x + nb * n_ctx
    assert seq % TM == 0 and (nb * n_ctx) % TM == 0 and seq % GRID_W == 0
    tiles_per_seq = seq // TM
    n_x_tiles = n_x // TM
    n_tiles = ta // TM

    mod_rows = (nb_all + 1 + 7) // 8 * 8
    cc = jnp.zeros((mod_rows, d), F32).at[:nb_all].set(c).at[nb_all].set(c_ctx)
    mod_all = _ada(cc, ada_w, ada_b).reshape(depth, mod_rows, 6, d)

    cos_t, sin_t = _rope_tables(seq, F32)
    head_of = jnp.arange(D_ATT) // HEAD_DIM
    hm = jnp.where(head_of[:, None] == head_of[None, :], 1.0 / HEAD_DIM, 0.0).astype(BF16)
    wr_t = w_router.T.astype(BF16)
    b_r = b_router.astype(F32)[:, None]
    key_j = jnp.arange(WINDOW)[:, None]
    qry_i = jnp.tile(jnp.arange(WINDOW), WIN_Q_HEADS)[None, :]
    tri = jnp.stack([jnp.where(key_j >= qry_i, 0.0, NEG_INF), jnp.where(key_j <= qry_i, 0.0, NEG_INF)]).astype(F32)

    layer_tabs = []
    for l in range(depth):
        gains = jnp.zeros((8, D_ATT), F32)
        gains = gains.at[0].set(jnp.tile(win_qk_gain[l, 0], WIN_Q_HEADS))
        gains = gains.at[1, :WIN_KV_HEADS * HEAD_DIM].set(jnp.tile(win_qk_gain[l, 1], WIN_KV_HEADS))
        gains = gains.at[2].set(jnp.tile(glb_qk_gain[l, 0], GLB_Q_HEADS))
        gains = gains.at[3, :GLB_KV_HEADS * HEAD_DIM].set(jnp.tile(glb_qk_gain[l, 1], GLB_KV_HEADS))
        sink_row = jnp.repeat(win_sink[l].astype(F32) * LOG2_E, WINDOW)[None, :]
        layer_tabs.append((gains, sink_row, _ret_tables(ret_decay[l])))

    def mix(l, s, xa, pending, release=None):
        last = l == depth - 1
        gains, sink_row, (gc, dmat, xi, zeta) = layer_tabs[l]
        xa, px, vt = _inproj(xa, mods[s][l], norm1[l][None, :], w_in, l, cos_t, sin_t, gains, hm,
                             nb, n_x_tiles, tiles_per_seq, pending)
        if release is not None:
            px = release(px)
        o_f, o_b = _retention(px, gc, dmat, xi, zeta, nb, seq, n_ctx)
        yw = _window(px, vt, sink_row, tri, nb, seq, n_ctx, not last)
        yg = _global(px, vt, nb, seq, n_ctx, not last)
        if not last:
            yw, yg = _ctx_attention(px, vt, sink_row, yw, yg, nb, seq, n_ctx)
        return _outproj(o_f, o_b, px, yw, yg, xa, mods[s][l], norm2[l][None, :], w_out, l,
                        wr_t, b_r, nb, n_x_tiles, tiles_per_seq, n_x_tiles if last else n_tiles)

    def tie(a, b):
        if n_streams == 1:
            return a, b
        return lax.optimization_barrier((a, b))

    mods, xas = [], []
    for s in range(n_streams):
        b0 = s * nb
        mods.append(jnp.concatenate([mod_all[:, b0:b0 + nb], mod_all[:, nb_all:nb_all + 1]], axis=1))
        xas.append(jnp.concatenate([x[b0:b0 + nb].reshape(n_x, d), ctx[b0:b0 + nb].reshape(nb * n_ctx, d)], 0))

    out = None
    pending = [None] * n_streams
    carry = None
    s_last = n_streams - 1

    def gather_back(yp, pos):
        return yp[pos[0]], yp[pos[1]]

    for l in range(depth):
        last = l == depth - 1
        n_tok = n_x if last else ta
        states, srcs, plans, xss = [], [], [], []
        for s in range(n_streams):
            due, release = None, None
            if s > 0:
                def release(px, s=s):
                    px, src_t = tie(px, srcs[s - 1])
                    xss.append(states[s - 1][1][src_t])
                    return px
            elif carry is not None:
                due, yp_c, pos_c, gate_c, mod_c = carry
                carry = None

                def release(px, due=due, yp_c=yp_c, pos_c=pos_c, gate_c=gate_c, mod_c=mod_c):
                    px, yp_t = tie(px, yp_c)
                    pending[due] = gather_back(yp_t, pos_c) + (gate_c, mod_c)
                    return px
            state = list(mix(l, s, xas[s], pending[s], release))
            if s > 0:
                state[2], xss[s - 1] = tie(state[2], xss[s - 1])
            elif due is not None:
                y0d, y1d, gate_d, mod_d = pending[due]
                state[2], (y0d, y1d) = tie(state[2], (y0d, y1d))
                pending[due] = (y0d, y1d, gate_d, mod_d)
            states.append(state)
            tables, src, pos = _moe_plan(state[2], n_tok)
            plans.append((tables, pos))
            srcs.append(src)
        yp_prev = None
        for s in range(n_streams):
            xa, h2, _, gate_t = states[s]
            tables, pos = plans[s]
            xs = xss[s] if s < s_last else h2[srcs[s]]
            if s < s_last:
                xs, srcs[s + 1] = tie(xs, srcs[s + 1])
            if s > 0:
                xs, yp_prev = tie(xs, yp_prev)
                ps, ppos, pgate, pmod, pxa = prev_info
                done = gather_back(yp_prev, ppos) + (pgate, pmod)
                if last:
                    out = _combine_final(pxa, *done, tiles_per_seq, n_x_tiles, out, ps * n_x_tiles, n_streams * n_x)
                else:
                    pending[ps] = done
            yp_prev = _moe_ffn(*tables, xs, w_gate_up, w_down, l)
            prev_info = (s, pos, gate_t.T, mods[s][l], xa)
            xas[s] = xa
        ps, ppos, pgate, pmod, pxa = prev_info
        if last:
            out = _combine_final(pxa, *gather_back(yp_prev, ppos), pgate, pmod, tiles_per_seq, n_x_tiles, out,
                                 ps * n_x_tiles, n_streams * n_x)
        elif n_streams == 1:
            pending[ps] = gather_back(yp_prev, ppos) + (pgate, pmod)
        else:
            carry = (ps, yp_prev, ppos, pgate, pmod)
    return out.reshape(nb_all, seq, d)
```

```python
import functools

import jax
import jax.numpy as jnp
from jax import lax
from jax.experimental import pallas as pl
from jax.experimental.pallas import tpu as pltpu

GRID_W = 64
HEAD_DIM = 64
RET_HEADS = 4
RET_DK = 64
RET_DV = 128
WIN_Q_HEADS = 4
WIN_KV_HEADS = 2
WINDOW = 128
GLB_Q_HEADS = 4
GLB_KV_HEADS = 2
ROPE_BASE = 10000.0
N_EXPERTS = 16
N_GROUPS = 4
EXPERTS_PER_GROUP = N_EXPERTS // N_GROUPS
TOP_K = 2
EPS = 1e-6
NEG_INF = -1e30

RQ, RK, RV, RG = 0, 256, 512, 1024
WQ, WK, WV = 1536, 1792, 1920
GQ, GK, GV = 2048, 2304, 2432
D_IN = 2560
D_RET = RET_HEADS * RET_DV
D_ATT = WIN_Q_HEADS * HEAD_DIM
N_KV = WIN_KV_HEADS * HEAD_DIM
P_RQ, P_RK, P_RV, P_RG = 0, 256, 512, 1024
P_WQ, P_GQ, P_WK, P_GK = 1536, 1792, 2048, 2176
PX_W = 2304

TM = 512
RET_C = 256
TQ_W = 512
TQ_G = 256
KC_G = 256
LOG2_E = 1.4426950408889634
assert GLB_KV_HEADS == 2
TM_E = 512
N_STREAMS = 2
VMEM_LIMIT = 56 * 1024 * 1024

F32 = jnp.float32
BF16 = jnp.bfloat16


def _dot(a, b):
    return jnp.dot(a, b, preferred_element_type=F32)


def _dot_nt(a, b):
    return lax.dot_general(a, b, (((1,), (1,)), ((), ())), preferred_element_type=F32)


def _dot_tn(a, b):
    return lax.dot_general(a, b, (((0,), (0,)), ((), ())), preferred_element_type=F32)


def _silu(x):
    return x * jax.nn.sigmoid(x)


def _params(sem):
    return pltpu.CompilerParams(dimension_semantics=sem, vmem_limit_bytes=VMEM_LIMIT)


def _ada_kernel(c_ref, w_ref, b_ref, o_ref):
    s = _silu(c_ref[...]).astype(BF16)
    o_ref[0] = _dot(s, w_ref[0].astype(BF16)) + b_ref[0]


def _ada(cc, ada_w, ada_b):
    depth, d, n = ada_w.shape
    tn = 1536
    rows = cc.shape[0]
    return pl.pallas_call(
        _ada_kernel,
        grid=(depth, n // tn),
        in_specs=[pl.BlockSpec((rows, d), lambda l, j: (0, 0)),
                  pl.BlockSpec((1, d, tn), lambda l, j: (l, 0, j)),
                  pl.BlockSpec((1, 1, tn), lambda l, j: (l, 0, j))],
        out_specs=pl.BlockSpec((1, rows, tn), lambda l, j: (l, 0, j)),
        out_shape=jax.ShapeDtypeStruct((depth, rows, n), F32),
        compiler_params=_params(("arbitrary", "arbitrary")),
        name="ada_mod",
    )(cc, ada_w, ada_b.reshape(depth, 1, n))


def _inproj_fused_kernel(x_ref, y0_ref, y1_ref, gate_ref, modp_ref, mod_ref, n1_ref, wf_ref, wvt_ref, cos_ref,
                         sin_ref, gains_ref, hm_ref, xo_ref, o_ref, vt_ref, w_ref):
    g = gate_ref[...]
    y = y0_ref[...].astype(F32) * g[:, 0:1] + y1_ref[...].astype(F32) * g[:, 1:2]
    x = x_ref[...] + modp_ref[0, 5:6, :] * y
    xo_ref[...] = x
    _inproj_body(x, mod_ref, n1_ref, wf_ref, wvt_ref, cos_ref, sin_ref, gains_ref, hm_ref, o_ref, vt_ref, w_ref)


def _inproj_kernel(x_ref, mod_ref, n1_ref, wf_ref, wvt_ref, cos_ref, sin_ref, gains_ref, hm_ref, o_ref, vt_ref,
                   w_ref):
    _inproj_body(x_ref[...], mod_ref, n1_ref, wf_ref, wvt_ref, cos_ref, sin_ref, gains_ref, hm_ref, o_ref, vt_ref,
                 w_ref)


def _inproj_body(x, mod_ref, n1_ref, wf_ref, wvt_ref, cos_ref, sin_ref, gains_ref, hm_ref, o_ref, vt_ref, w_ref):
    @pl.when(pl.program_id(0) == 0)
    def _():
        w_ref[...] = wf_ref[...].astype(BF16)

    ms = jnp.mean(x * x, axis=-1, keepdims=True)
    h = x * lax.rsqrt(ms + EPS) * n1_ref[...]
    h = h * (1.0 + mod_ref[0, 1:2, :]) + mod_ref[0, 0:1, :]
    hb = h.astype(BF16)

    def proj(lo, width):
        return _dot(hb, w_ref[:, lo:lo + width])

    def head_msq(y):
        sq = y * y
        sq_hi = sq.astype(BF16)
        sq_lo = (sq - sq_hi.astype(F32)).astype(BF16)
        hm = hm_ref[0:y.shape[1], 0:y.shape[1]]
        return _dot(sq_hi, hm) + _dot(sq_lo, hm)

    def qk_finish(y, msq, gain_row, scale):
        width = y.shape[1]
        yn = y * lax.rsqrt(msq + EPS) * gains_ref[gain_row:gain_row + 1, 0:width]
        nxt = pltpu.roll(yn, width - HEAD_DIM // 4, 1)
        prv = pltpu.roll(yn, HEAD_DIM // 4, 1)
        lane = lax.broadcasted_iota(jnp.int32, yn.shape, 1)
        partner = jnp.where((lane % (HEAD_DIM // 2)) < HEAD_DIM // 4, nxt, prv)
        yr = yn * cos_ref[:, 0:width] + partner * sin_ref[:, 0:width]
        return (yr * scale).astype(BF16)

    q_scale = HEAD_DIM ** -0.5 * LOG2_E
    qk_segs = ((WQ, D_ATT, 0, q_scale, P_WQ), (GQ, D_ATT, 2, q_scale, P_GQ),
               (WK, N_KV, 1, 1.0, P_WK), (GK, N_KV, 3, 1.0, P_GK))
    plain_segs = ((RQ, RK - RQ, 1.0, P_RQ), (RK, RV - RK, RET_DK ** -0.5, P_RK),
                  (RV, RG - RV, 1.0, P_RV), (RG, WQ - RG, 1.0, P_RG))
    ys = [proj(lo, width) for lo, width, _, _, _ in qk_segs]
    stats = [head_msq(y) for y in ys]
    vt_ref[...] = _dot_nt(wvt_ref[...], hb).astype(BF16)
    for (lo, width, scale, dst), (_, qwidth, gain_row, qscale, qdst), y, msq in zip(plain_segs, qk_segs, ys, stats):
        p = proj(lo, width)
        o_ref[:, dst:dst + width] = (p if scale == 1.0 else p * scale).astype(BF16)
        o_ref[:, qdst:qdst + qwidth] = qk_finish(y, msq, gain_row, qscale)


def _inproj(xa, mod_l, n1, w_in, layer, cos_t, sin_t, gains, hm, nb, n_x_tiles, tiles_per_seq, pending=None):
    ta, d = xa.shape
    n_tiles = ta // TM
    w_vt = jnp.concatenate([w_in[layer, :, WV:WV + N_KV], w_in[layer, :, GV:GV + N_KV]], axis=1).T.astype(BF16)
    n_gv = 2 * N_KV
    once = pl.Buffered(1)

    def mod_idx(i):
        return (jnp.where(i < n_x_tiles, i // tiles_per_seq, nb), 0, 0)

    def rope_idx(i):
        return (jnp.where(i < n_x_tiles, i % tiles_per_seq, tiles_per_seq), 0)

    row = lambda i: (i, 0)
    in_specs = [pl.BlockSpec((1, 6, d), mod_idx),
                pl.BlockSpec((1, d), lambda i: (0, 0)),
                pl.BlockSpec((None, d, D_IN), lambda i: (layer, 0, 0), pipeline_mode=once),
                pl.BlockSpec((n_gv, d), lambda i: (0, 0), pipeline_mode=once),
                pl.BlockSpec((TM, D_ATT), rope_idx),
                pl.BlockSpec((TM, D_ATT), rope_idx),
                pl.BlockSpec((8, D_ATT), lambda i: (0, 0)),
                pl.BlockSpec((D_ATT, D_ATT), lambda i: (0, 0))]
    args = [mod_l, n1, w_in, w_vt, cos_t, sin_t, gains, hm]
    out_specs = [pl.BlockSpec((TM, PX_W), row), pl.BlockSpec((n_gv, TM), lambda i: (0, i))]
    out_shape = [jax.ShapeDtypeStruct((ta, PX_W), BF16), jax.ShapeDtypeStruct((n_gv, ta), BF16)]
    if pending is None:
        kern, aliases = _inproj_kernel, {}
        in_specs = [pl.BlockSpec((TM, d), row)] + in_specs
        args = [xa] + args
    else:
        y0, y1, gate, mod_prev = pending
        kern, aliases = _inproj_fused_kernel, {0: 0}
        in_specs = [pl.BlockSpec((TM, d), row), pl.BlockSpec((TM, d), row), pl.BlockSpec((TM, d), row),
                    pl.BlockSpec((TM, TOP_K), row), pl.BlockSpec((1, 6, d), mod_idx)] + in_specs
        args = [xa, y0, y1, gate, mod_prev] + args
        out_specs = [pl.BlockSpec((TM, d), row)] + out_specs
        out_shape = [jax.ShapeDtypeStruct((ta, d), F32)] + out_shape
    res = pl.pallas_call(
        kern,
        grid=(n_tiles,),
        in_specs=in_specs,
        out_specs=out_specs,
        out_shape=out_shape,
        scratch_shapes=[pltpu.VMEM((d, D_IN), BF16)],
        input_output_aliases=aliases,
        compiler_params=_params(("arbitrary",)),
        name="in_proj",
    )(*args)
    return (xa,) + tuple(res) if pending is None else tuple(res)


def _ret_kernel(gc_ref, qf_ref, kf_ref, vf_ref, qb_ref, kb_ref, vb_ref, dmat_ref, xi_ref, zeta_ref,
                of_ref, ob_ref, s_ref):
    @pl.when(pl.program_id(1) == 0)
    def _():
        s_ref[...] = jnp.zeros_like(s_ref)

    dirs = ((qf_ref, kf_ref, vf_ref, of_ref), (qb_ref, kb_ref, vb_ref, ob_ref))
    for d, (q_ref, k_ref, v_ref, o_ref) in enumerate(dirs):
        q = q_ref[...]
        k = k_ref[...]
        v = v_ref[...]
        kz = (k.astype(F32) * zeta_ref[d]).astype(BF16)
        outs = []
        for h in range(RET_HEADS):
            i = d * RET_HEADS + h
            qh = q[:, h * RET_DK:(h + 1) * RET_DK]
            kh = k[:, h * RET_DK:(h + 1) * RET_DK]
            vh = v[:, h * RET_DV:(h + 1) * RET_DV]
            att = _dot_nt(qh, kh) * dmat_ref[i]
            state = s_ref[i]
            outs.append(_dot(att.astype(BF16), vh) + _dot(qh, state.astype(BF16)) * xi_ref[i])
            s_ref[i] = gc_ref[i] * state + _dot_tn(kz[:, h * RET_DK:(h + 1) * RET_DK], vh)
        o_ref[...] = jnp.concatenate(outs, axis=-1).astype(BF16)


def _retention(px, gc, dmat, xi, zeta, nb, seq, n_ctx):
    ta = px.shape[0]
    n_x = seq // RET_C
    n_c = n_ctx // RET_C
    ctx_base = nb * n_x
    steps = n_c + n_x

    def row_f(b, c):
        return jnp.where(c < n_c, ctx_base + b * n_c + c, b * n_x + (c - n_c))

    def row_b(b, c):
        return jnp.where(c < n_c, ctx_base + b * n_c + (n_c - 1 - c), b * n_x + (steps - 1 - c))

    def spec(width, col, row):
        return pl.BlockSpec((RET_C, width), lambda b, c: (row(b, c), col))

    const3 = lambda b, c: (0, 0, 0)
    return pl.pallas_call(
        _ret_kernel,
        grid=(nb, steps),
        in_specs=[pl.BlockSpec(memory_space=pltpu.SMEM),
                  spec(256, P_RQ // 256, row_f), spec(256, P_RK // 256, row_f), spec(D_RET, P_RV // D_RET, row_f),
                  spec(256, P_RQ // 256, row_b), spec(256, P_RK // 256, row_b), spec(D_RET, P_RV // D_RET, row_b),
                  pl.BlockSpec((2 * RET_HEADS, RET_C, RET_C), const3),
                  pl.BlockSpec((2 * RET_HEADS, RET_C, RET_DV), const3),
                  pl.BlockSpec((2, RET_C, RET_HEADS * RET_DK), const3)],
        out_specs=[spec(D_RET, 0, row_f), spec(D_RET, 0, row_b)],
        out_shape=[jax.ShapeDtypeStruct((ta, D_RET), BF16)] * 2,
        scratch_shapes=[pltpu.VMEM((2 * RET_HEADS, RET_DK, RET_DV), F32)],
        compiler_params=_params(("parallel", "arbitrary")),
        name="retention",
    )(gc, px, px, px, px, px, px, dmat, xi, zeta)


def _ret_tables(decay_logit):
    log_g = jax.nn.log_sigmoid(decay_logit.astype(F32)).reshape(2 * RET_HEADS)
    idx = jnp.arange(RET_C, dtype=F32)
    diff = idx[:, None] - idx[None, :]
    lg = log_g[:, None, None]
    d_fwd = jnp.where(diff >= 0, jnp.exp(jnp.maximum(diff, 0.0) * lg), 0.0)
    d_bwd = jnp.where(diff <= 0, jnp.exp(jnp.maximum(-diff, 0.0) * lg), 0.0)
    is_bwd = (jnp.arange(2 * RET_HEADS) >= RET_HEADS)[:, None, None]
    dmat = jnp.where(is_bwd, d_bwd, d_fwd)
    pos = jnp.where(is_bwd[:, :, 0], RET_C - 1.0 - idx[None, :], idx[None, :])
    xi = jnp.exp((pos + 1.0) * log_g[:, None])
    zeta = jnp.exp((RET_C - 1.0 - pos) * log_g[:, None])
    gc = jnp.exp(RET_C * log_g)
    xi = jnp.broadcast_to(xi[:, :, None], (2 * RET_HEADS, RET_C, RET_DV))
    zeta = jnp.repeat(zeta.reshape(2, RET_HEADS, RET_C).transpose(0, 2, 1), RET_DK, axis=-1)
    return gc, dmat, xi, zeta


def _pad_heads(q, n_kv):
    rows, width = q.shape
    group = width // HEAD_DIM // n_kv
    zeros = jnp.zeros((rows, HEAD_DIM), BF16)
    out = []
    for h in range(n_kv):
        for g in range(group):
            qh = q[:, (h * group + g) * HEAD_DIM:(h * group + g + 1) * HEAD_DIM]
            out.append(jnp.concatenate([qh if hh == h else zeros for hh in range(n_kv)], axis=1))
    return jnp.concatenate(out, axis=0)


def _fill_values(v_ext, lo, vt):
    n = vt.shape[1]
    for h in range(v_ext.shape[0]):
        v_ext[h, 0:HEAD_DIM, lo:lo + n] = vt[h * HEAD_DIM:(h + 1) * HEAD_DIM, :]


def _store_heads(o_ref, col, h, group, oe, den):
    n = oe.shape[1] // group
    o = oe[0:HEAD_DIM] / den
    for g in range(group):
        r0 = (h * group + g) * HEAD_DIM
        o_ref[r0:r0 + HEAD_DIM, col:col + n] = o[:, g * n:(g + 1) * n].astype(BF16)


def _win_kernel(q_ref, k_ref, kc_ref, vt_ref, vtc_ref, sink_ref, tri_ref, o_ref, k_pad, v_ext, *, seq, n_ctx):
    i = pl.program_id(1)
    group = WIN_Q_HEADS // WIN_KV_HEADS
    sub = WINDOW
    w = group * sub
    span = 3 * WINDOW
    c0 = seq + 2 * WINDOW
    n_sub = q_ref.shape[0] // sub

    @pl.when(i == 0)
    def _():
        zk = jnp.zeros((WINDOW, N_KV), BF16)
        k_pad[0:WINDOW, :] = zk
        k_pad[WINDOW:WINDOW + seq, :] = k_ref[...]
        k_pad[WINDOW + seq:c0, :] = zk
        k_pad[c0:c0 + n_ctx, :] = kc_ref[...]
        v_ext[:, 0:HEAD_DIM, 0:WINDOW] = jnp.zeros((WIN_KV_HEADS, HEAD_DIM, WINDOW), BF16)
        v_ext[:, 0:HEAD_DIM, WINDOW + seq:c0] = jnp.zeros((WIN_KV_HEADS, HEAD_DIM, WINDOW), BF16)
        _fill_values(v_ext, WINDOW, vt_ref[...])
        _fill_values(v_ext, c0, vtc_ref[...])
        v_ext[:, HEAD_DIM:2 * HEAD_DIM, :] = jnp.ones((WIN_KV_HEADS, HEAD_DIM, c0 + n_ctx), BF16)

    sink = sink_ref[...]
    kc = k_pad[c0:c0 + n_ctx, :]
    start = i * q_ref.shape[0]

    def scores(a):
        qs = pl.multiple_of(start + a * sub, sub)
        q4 = _pad_heads(q_ref[a * sub:(a + 1) * sub, :], WIN_KV_HEADS)
        return a, qs, _dot_nt(k_pad[pl.ds(qs, span), :], q4), _dot_nt(kc, q4)

    def softmax(a, qs, st, sc):
        lo_edge = jnp.where(qs == 0, NEG_INF, 0.0)
        hi_edge = jnp.where(qs + sub == seq, NEG_INF, 0.0)
        s0 = st[0:WINDOW] + (tri_ref[0] + lo_edge)
        s1 = st[WINDOW:2 * WINDOW]
        s2 = st[2 * WINDOW:span] + (tri_ref[1] + hi_edge)
        m = jnp.maximum(jnp.maximum(jnp.max(s0, axis=0, keepdims=True), jnp.max(s1, axis=0, keepdims=True)),
                        jnp.maximum(jnp.max(s2, axis=0, keepdims=True), jnp.max(sc, axis=0, keepdims=True)))
        m = jnp.maximum(m, sink)
        pw = jnp.concatenate([jnp.exp2(s0 - m), jnp.exp2(s1 - m), jnp.exp2(s2 - m)], axis=0).astype(BF16)
        return a, qs, m, pw, jnp.exp2(sc - m).astype(BF16)

    def values(a, qs, m, pw, pc):
        for h in range(WIN_KV_HEADS):
            cs = slice(h * w, (h + 1) * w)
            oe = _dot(v_ext[h, :, pl.ds(qs, span)], pw[:, cs]) + _dot(v_ext[h, :, c0:c0 + n_ctx], pc[:, cs])
            den = oe[HEAD_DIM:HEAD_DIM + 1] + jnp.exp2(sink[:, cs] - m[:, cs])
            _store_heads(o_ref, a * sub, h, group, oe, den)

    nxt = scores(0)
    pend = None
    for a in range(n_sub):
        cur = nxt
        if a + 1 < n_sub:
            nxt = scores(a + 1)
        if pend is not None:
            values(*pend)
        pend = softmax(*cur)
    values(*pend)


def _window(px, vt, sink_row, tri, nb, seq, n_ctx, with_ctx_cols):
    ta = px.shape[0] if with_ctx_cols else nb * seq
    assert seq % TQ_W == 0 and TQ_W % WINDOW == 0
    n_q = seq // TQ_W
    ctx_rows = (nb * seq) // n_ctx
    n_keys = seq + 2 * WINDOW + n_ctx
    return pl.pallas_call(
        functools.partial(_win_kernel, seq=seq, n_ctx=n_ctx),
        grid=(nb, n_q),
        in_specs=[pl.BlockSpec((TQ_W, D_ATT), lambda b, i: (b * n_q + i, P_WQ // D_ATT)),
                  pl.BlockSpec((seq, N_KV), lambda b, i: (b, P_WK // N_KV)),
                  pl.BlockSpec((n_ctx, N_KV), lambda b, i: (ctx_rows + b, P_WK // N_KV)),
                  pl.BlockSpec((N_KV, seq), lambda b, i: (0, b)),
                  pl.BlockSpec((N_KV, n_ctx), lambda b, i: (0, ctx_rows + b)),
                  pl.BlockSpec((1, WIN_Q_HEADS * WINDOW), lambda b, i: (0, 0)),
                  pl.BlockSpec((2, WINDOW, WIN_Q_HEADS * WINDOW), lambda b, i: (0, 0, 0))],
        out_specs=pl.BlockSpec((D_ATT, TQ_W), lambda b, i: (0, b * n_q + i)),
        out_shape=jax.ShapeDtypeStruct((D_ATT, ta), BF16),
        scratch_shapes=[pltpu.VMEM((n_keys, N_KV), BF16),
                        pltpu.VMEM((WIN_KV_HEADS, 2 * HEAD_DIM, n_keys), BF16)],
        compiler_params=_params(("parallel", "arbitrary")),
        name="window_attn",
    )(px, px, px, vt, vt, sink_row, tri)


def _ctx_kernel(qw_ref, kw_ref, qg_ref, kg_ref, vt_ref, sink_ref, yw_in, yg_in, yw_ref, yg_ref):
    del yw_in, yg_in
    n_ctx = qw_ref.shape[0]
    sub = WINDOW
    ones = jnp.ones((HEAD_DIM, n_ctx), BF16)
    for q_ref, k_ref, row0, o_ref, sink in ((qw_ref, kw_ref, 0, yw_ref, sink_ref[...]),
                                            (qg_ref, kg_ref, N_KV, yg_ref, None)):
        n_kv = k_ref.shape[1] // HEAD_DIM
        group = q_ref.shape[1] // HEAD_DIM // n_kv
        w = group * sub
        k = k_ref[...]
        for a in range(n_ctx // sub):
            sc = _dot_nt(k, _pad_heads(q_ref[a * sub:(a + 1) * sub, :], n_kv))
            m = jnp.max(sc, axis=0, keepdims=True)
            if sink is not None:
                m = jnp.maximum(m, sink)
            p = jnp.exp2(sc - m).astype(BF16)
            for h in range(n_kv):
                cs = slice(h * w, (h + 1) * w)
                ve = jnp.concatenate([vt_ref[row0 + h * HEAD_DIM:row0 + (h + 1) * HEAD_DIM, :], ones], axis=0)
                oe = _dot(ve, p[:, cs])
                den = oe[HEAD_DIM:HEAD_DIM + 1]
                if sink is not None:
                    den = den + jnp.exp2(sink[:, cs] - m[:, cs])
                _store_heads(o_ref, a * sub, h, group, oe, den)


def _ctx_attention(px, vt, sink_row, ywt, ygt, nb, seq, n_ctx):
    ctx_rows = (nb * seq) // n_ctx
    row = lambda col: (lambda b: (ctx_rows + b, col))
    any_spec = pl.BlockSpec(memory_space=pl.ANY)
    out_spec = pl.BlockSpec((D_ATT, n_ctx), lambda b: (0, ctx_rows + b))
    return pl.pallas_call(
        _ctx_kernel,
        grid=(nb,),
        in_specs=[pl.BlockSpec((n_ctx, D_ATT), row(P_WQ // D_ATT)),
                  pl.BlockSpec((n_ctx, N_KV), row(P_WK // N_KV)),
                  pl.BlockSpec((n_ctx, D_ATT), row(P_GQ // D_ATT)),
                  pl.BlockSpec((n_ctx, N_KV), row(P_GK // N_KV)),
                  pl.BlockSpec((2 * N_KV, n_ctx), lambda b: (0, ctx_rows + b)),
                  pl.BlockSpec((1, WIN_Q_HEADS * WINDOW), lambda b: (0, 0)),
                  any_spec, any_spec],
        out_specs=[out_spec, out_spec],
        out_shape=[jax.ShapeDtypeStruct(ywt.shape, BF16), jax.ShapeDtypeStruct(ygt.shape, BF16)],
        input_output_aliases={6: 0, 7: 1},
        compiler_params=_params(("parallel",)),
        name="ctx_attn",
    )(px, px, px, px, vt, sink_row, ywt, ygt)


def _glb_kernel(q_ref, k_ref, kc_ref, vt_ref, vtc_ref, o_ref, k_all, v_ext, *, seq, n_ctx):
    i = pl.program_id(1)
    group = GLB_Q_HEADS // GLB_KV_HEADS
    tq = q_ref.shape[0]
    w = group * tq

    @pl.when(i == 0)
    def _():
        k_all[0:seq, :] = k_ref[...]
        k_all[seq:seq + n_ctx, :] = kc_ref[...]
        _fill_values(v_ext, 0, vt_ref[...])
        _fill_values(v_ext, seq, vtc_ref[...])
        v_ext[:, HEAD_DIM:2 * HEAD_DIM, :] = jnp.ones((GLB_KV_HEADS, HEAD_DIM, seq + n_ctx), BF16)

    def attend(chunks):
        q4 = _pad_heads(q_ref[...], GLB_KV_HEADS)

        def scores(chunk):
            lo, nk = chunk
            return _dot_nt(k_all[lo:lo + nk, :], q4)

        m = None
        acc = [None] * GLB_KV_HEADS

        def accumulate(pend):
            pt, alpha, (lo, nk) = pend
            for h in range(GLB_KV_HEADS):
                pv = _dot(v_ext[h, :, lo:lo + nk], pt[:, h * w:(h + 1) * w])
                acc[h] = pv if alpha is None else acc[h] * alpha[:, h * w:(h + 1) * w] + pv

        st_next = scores(chunks[0])
        pend = None
        for ci, chunk in enumerate(chunks):
            st = st_next
            if ci + 1 < len(chunks):
                st_next = scores(chunks[ci + 1])
            if pend is not None:
                accumulate(pend)
            cm = jnp.max(st, axis=0, keepdims=True)
            m_new = cm if m is None else jnp.maximum(m, cm)
            pt = jnp.exp2(st - m_new).astype(BF16)
            alpha = None if m is None else jnp.exp2(m - m_new)
            pend = (pt, alpha, chunk)
            m = m_new
        accumulate(pend)
        for h in range(GLB_KV_HEADS):
            _store_heads(o_ref, 0, h, group, acc[h], acc[h][HEAD_DIM:HEAD_DIM + 1])

    attend([(c * KC_G, KC_G) for c in range(seq // KC_G)] + [(seq, n_ctx)])


def _global(px, vt, nb, seq, n_ctx, with_ctx_cols):
    ta = px.shape[0] if with_ctx_cols else nb * seq
    assert seq % KC_G == 0 and seq % TQ_G == 0
    n_q = seq // TQ_G
    ctx_rows = (nb * seq) // n_ctx
    return pl.pallas_call(
        functools.partial(_glb_kernel, seq=seq, n_ctx=n_ctx),
        grid=(nb, n_q),
        in_specs=[pl.BlockSpec((TQ_G, D_ATT), lambda b, i: (b * n_q + i, P_GQ // D_ATT)),
                  pl.BlockSpec((seq, N_KV), lambda b, i: (b, P_GK // N_KV)),
                  pl.BlockSpec((n_ctx, N_KV), lambda b, i: (ctx_rows + b, P_GK // N_KV)),
                  pl.BlockSpec((N_KV, seq), lambda b, i: (1, b)),
                  pl.BlockSpec((N_KV, n_ctx), lambda b, i: (1, ctx_rows + b))],
        out_specs=pl.BlockSpec((D_ATT, TQ_G), lambda b, i: (0, b * n_q + i)),
        out_shape=jax.ShapeDtypeStruct((D_ATT, ta), BF16),
        scratch_shapes=[pltpu.VMEM((seq + n_ctx, N_KV), BF16),
                        pltpu.VMEM((GLB_KV_HEADS, 2 * HEAD_DIM, seq + n_ctx), BF16)],
        compiler_params=_params(("parallel", "arbitrary")),
        name="global_attn",
    )(px, px, px, vt, vt)


def _route_rows(logits, bias):
    sig = jax.nn.sigmoid(logits)
    biased = sig + bias
    b_rows = [biased[e:e + 1, :] for e in range(N_EXPERTS)]
    s_rows = [sig[e:e + 1, :] for e in range(N_EXPERTS)]
    n_loc = EXPERTS_PER_GROUP

    best_score, grp = None, None
    for g in range(N_GROUPS):
        a = b_rows[g * n_loc:(g + 1) * n_loc]
        top2 = None
        for i in range(n_loc):
            for j in range(i + 1, n_loc):
                pair = a[i] + a[j]
                top2 = pair if top2 is None else jnp.maximum(top2, pair)
        if g == 0:
            best_score, grp = top2, jnp.zeros(top2.shape, jnp.int32)
        else:
            upd = top2 > best_score
            grp = jnp.where(upd, g, grp)
            best_score = jnp.where(upd, top2, best_score)

    def pick(rows, i):
        out = rows[i]
        for g in range(1, N_GROUPS):
            out = jnp.where(grp == g, rows[g * n_loc + i], out)
        return out

    cand = [pick(b_rows, i) for i in range(n_loc)]
    cand_s = [pick(s_rows, i) for i in range(n_loc)]
    m1, l1, w1 = cand[0], jnp.zeros(grp.shape, jnp.int32), cand_s[0]
    for i in range(1, n_loc):
        upd = cand[i] > m1
        m1 = jnp.where(upd, cand[i], m1)
        l1 = jnp.where(upd, i, l1)
        w1 = jnp.where(upd, cand_s[i], w1)
    m2, l2, w2 = None, None, None
    for i in range(n_loc):
        rest = jnp.where(l1 == i, -jnp.inf, cand[i])
        if i == 0:
            m2, l2, w2 = rest, jnp.zeros(grp.shape, jnp.int32), cand_s[0]
        else:
            upd = rest > m2
            m2 = jnp.where(upd, rest, m2)
            l2 = jnp.where(upd, i, l2)
            w2 = jnp.where(upd, cand_s[i], w2)
    tot = w1 + w2
    e_idx = jnp.concatenate([grp * n_loc + l1, grp * n_loc + l2], axis=0)
    gate = jnp.concatenate([w1 / tot, w2 / tot], axis=0)
    return e_idx, gate


def _outproj_kernel(of_ref, ob_ref, g_ref, ywt_ref, ygt_ref, x_ref, mod_ref, n2_ref, wf_ref, wr_ref, br_ref,
                    xo_ref, h_ref, e_ref, gate_ref, w_ref):
    @pl.when(pl.program_id(0) == 0)
    def _():
        w_ref[...] = wf_ref[...].astype(BF16)

    def ret_out(r):
        o = of_ref[r, :].astype(F32) + ob_ref[r, :].astype(F32)
        normed = []
        for h in range(RET_HEADS):
            oh = o[:, h * RET_DV:(h + 1) * RET_DV]
            mu = jnp.mean(oh, axis=-1, keepdims=True)
            var = jnp.mean(jnp.square(oh - mu), axis=-1, keepdims=True)
            normed.append((oh - mu) * lax.rsqrt(var + EPS))
        return (_silu(g_ref[r, :].astype(F32)) * jnp.concatenate(normed, axis=-1)).astype(BF16)

    def project(r, yr):
        acc = _dot_tn(ywt_ref[:, r], w_ref[D_RET:D_RET + D_ATT, :])
        acc += _dot_tn(ygt_ref[:, r], w_ref[D_RET + D_ATT:D_RET + 2 * D_ATT, :])
        return acc + _dot(yr, w_ref[0:D_RET, :])

    def residual_norm(r, acc):
        x = x_ref[r, :] + mod_ref[0, 2:3, :] * acc
        xo_ref[r, :] = x
        ms = jnp.mean(x * x, axis=-1, keepdims=True)
        h2 = x * lax.rsqrt(ms + EPS) * n2_ref[...]
        h2 = (h2 * (1.0 + mod_ref[0, 4:5, :]) + mod_ref[0, 3:4, :]).astype(BF16)
        h_ref[r, :] = h2
        return h2

    def route(r, h2):
        e_idx, gate = _route_rows(_dot_nt(wr_ref[...], h2), br_ref[...])
        e_ref[:, r] = e_idx
        gate_ref[:, r] = gate

    half = x_ref.shape[0] // 2
    ra, rb = slice(0, half), slice(half, 2 * half)
    yr_a = ret_out(ra)
    acc_a = project(ra, yr_a)
    yr_b = ret_out(rb)
    h2_a = residual_norm(ra, acc_a)
    acc_b = project(rb, yr_b)
    route(ra, h2_a)
    h2_b = residual_norm(rb, acc_b)
    route(rb, h2_b)


def _outproj(o_f, o_b, px, yw, yg, xa, mod_l, n2, w_out, layer, wr_t, b_r, nb, n_x_tiles, tiles_per_seq, n_tiles):
    ta, d = xa.shape
    once = pl.Buffered(1)

    def mod_idx(i):
        return (jnp.where(i < n_x_tiles, i // tiles_per_seq, nb), 0, 0)

    row = lambda i: (i, 0)
    return pl.pallas_call(
        _outproj_kernel,
        grid=(n_tiles,),
        in_specs=[pl.BlockSpec((TM, D_RET), row),
                  pl.BlockSpec((TM, D_RET), row),
                  pl.BlockSpec((TM, D_RET), lambda i: (i, P_RG // D_RET)),
                  pl.BlockSpec((D_ATT, TM), lambda i: (0, i)),
                  pl.BlockSpec((D_ATT, TM), lambda i: (0, i)),
                  pl.BlockSpec((TM, d), row),
                  pl.BlockSpec((1, 6, d), mod_idx),
                  pl.BlockSpec((1, d), lambda i: (0, 0)),
                  pl.BlockSpec((None, d, d), lambda i: (layer, 0, 0), pipeline_mode=once),
                  pl.BlockSpec((N_EXPERTS, d), lambda i: (0, 0)),
                  pl.BlockSpec((N_EXPERTS, 1), lambda i: (0, 0))],
        out_specs=[pl.BlockSpec((TM, d), row),
                   pl.BlockSpec((TM, d), row),
                   pl.BlockSpec((TOP_K, TM), lambda i: (0, i)),
                   pl.BlockSpec((TOP_K, TM), lambda i: (0, i))],
        out_shape=[jax.ShapeDtypeStruct((ta, d), F32),
                   jax.ShapeDtypeStruct((n_tiles * TM, d), BF16),
                   jax.ShapeDtypeStruct((TOP_K, n_tiles * TM), jnp.int32),
                   jax.ShapeDtypeStruct((TOP_K, n_tiles * TM), F32)],
        scratch_shapes=[pltpu.VMEM((d, d), BF16)],
        input_output_aliases={5: 0},
        compiler_params=_params(("arbitrary",)),
        name="out_proj",
    )(o_f, o_b, px, yw, yg, xa, mod_l, n2, w_out, wr_t, b_r)


def _moe_kernel(be_ref, first_ref, nu_ref, x_ref, wgu_f_ref, wd_f_ref, o_ref, wgu_ref, wd_ref):
    i = pl.program_id(0)

    @pl.when(jnp.logical_and(i < nu_ref[0], first_ref[i] == 1))
    def _():
        wgu_ref[...] = wgu_f_ref[...].astype(BF16)
        wd_ref[...] = wd_f_ref[...].astype(BF16)

    @pl.when(i < nu_ref[0])
    def _():
        f = wd_ref.shape[0]
        au = _dot(x_ref[...], wgu_ref[...])
        mid = (_silu(au[:, :f]) * au[:, f:]).astype(BF16)
        o_ref[...] = _dot(mid, wd_ref[...]).astype(BF16)

    @pl.when(i >= nu_ref[0])
    def _():
        o_ref[...] = jnp.zeros_like(o_ref)


def _moe_ffn(blk_e, first, n_used, xs, w_gu, w_down, layer):
    rows, d = xs.shape
    f2 = w_gu.shape[3]
    f = w_down.shape[2]
    grid_spec = pltpu.PrefetchScalarGridSpec(
        num_scalar_prefetch=3,
        grid=(rows // TM_E,),
        in_specs=[pl.BlockSpec((TM_E, d), lambda i, be, fi, nu: (i, 0)),
                  pl.BlockSpec((None, None, d, f2), lambda i, be, fi, nu: (layer, be[i], 0, 0)),
                  pl.BlockSpec((None, None, f, d), lambda i, be, fi, nu: (layer, be[i], 0, 0))],
        out_specs=pl.BlockSpec((TM_E, d), lambda i, be, fi, nu: (i, 0)),
        scratch_shapes=[pltpu.VMEM((d, f2), BF16), pltpu.VMEM((f, d), BF16)],
    )
    return pl.pallas_call(
        _moe_kernel,
        grid_spec=grid_spec,
        out_shape=jax.ShapeDtypeStruct((rows, d), BF16),
        compiler_params=_params(("arbitrary",)),
        name="moe_ffn",
    )(blk_e, first, n_used, xs, w_gu, w_down)


def _moe_plan(e_idx_t, n_tok):
    n_asg = n_tok * TOP_K
    flat_e = e_idx_t[:, :n_tok].reshape(-1)
    onehot = (flat_e[:, None] == jnp.arange(N_EXPERTS, dtype=jnp.int32)[None, :]).astype(jnp.int32)
    csum = jnp.cumsum(onehot, axis=0)
    counts = csum[-1]
    rank = jnp.sum(csum * onehot, axis=1) - 1
    padded = (counts + TM_E - 1) // TM_E * TM_E
    pad_end = jnp.cumsum(padded)
    pad_start = pad_end - padded
    cnt_start = jnp.cumsum(counts) - counts
    pos = (jnp.sum(pad_start[None, :] * onehot, axis=1) + rank).reshape(TOP_K, n_tok)
    n_blocks = (n_asg + N_EXPERTS * (TM_E - 1) + TM_E - 1) // TM_E
    blk_start = jnp.arange(n_blocks, dtype=jnp.int32) * TM_E
    blk_e = jnp.minimum(jnp.sum(blk_start[:, None] >= pad_end[None, :], axis=1), N_EXPERTS - 1).astype(jnp.int32)
    first = jnp.concatenate([jnp.ones((1,), jnp.int32), (blk_e[1:] != blk_e[:-1]).astype(jnp.int32)])
    n_used = (pad_end[-1] // TM_E).astype(jnp.int32).reshape(1)
    bits = max(1, (n_asg - 1).bit_length())
    order = jnp.sort((flat_e << bits) | jnp.arange(n_asg, dtype=jnp.int32)) & ((1 << bits) - 1)
    blk_hot = (blk_e[:, None] == jnp.arange(N_EXPERTS, dtype=jnp.int32)[None, :]).astype(jnp.int32)
    blk_pad_start = jnp.sum(blk_hot * pad_start[None, :], axis=1)[:, None]
    blk_counts = jnp.sum(blk_hot * counts[None, :], axis=1)[:, None]
    blk_cnt_start = jnp.sum(blk_hot * cnt_start[None, :], axis=1)[:, None]
    p = blk_start[:, None] + jnp.arange(TM_E, dtype=jnp.int32)[None, :]
    r = p - blk_pad_start
    idx = jnp.clip(blk_cnt_start + r, 0, n_asg - 1).reshape(-1)
    src = jnp.where((r < blk_counts).reshape(-1), order[idx], p.reshape(-1)) % n_tok
    return (blk_e, first, n_used), src, pos


def _combine_kernel(x_ref, y0_ref, y1_ref, gate_ref, mod_ref, o_ref):
    g = gate_ref[...]
    y = y0_ref[...].astype(F32) * g[:, 0:1] + y1_ref[...].astype(F32) * g[:, 1:2]
    o_ref[...] = x_ref[...] + mod_ref[0, 5:6, :] * y


def _combine_final_kernel(x_ref, y0_ref, y1_ref, gate_ref, mod_ref, prev_ref, o_ref):
    del prev_ref
    _combine_kernel(x_ref, y0_ref, y1_ref, gate_ref, mod_ref, o_ref)


def _combine_final(xa, y0, y1, gate, mod_l, tiles_per_seq, n_tiles, prev, tile_off, total_rows):
    d = xa.shape[1]
    row = lambda i: (i, 0)
    ins = [pl.BlockSpec((TM, d), row), pl.BlockSpec((TM, d), row), pl.BlockSpec((TM, d), row),
           pl.BlockSpec((TM, TOP_K), row), pl.BlockSpec((1, 6, d), lambda i: (i // tiles_per_seq, 0, 0))]
    args = [xa, y0, y1, gate, mod_l]
    if prev is not None:
        ins.append(pl.BlockSpec(memory_space=pl.ANY))
        args.append(prev)
    return pl.pallas_call(
        _combine_kernel if prev is None else _combine_final_kernel,
        grid=(n_tiles,),
        in_specs=ins,
        out_specs=pl.BlockSpec((TM, d), lambda i: (tile_off + i, 0)),
        out_shape=jax.ShapeDtypeStruct((total_rows, d), F32),
        input_output_aliases={} if prev is None else {5: 0},
        compiler_params=_params(("parallel",)),
        name="moe_combine_out",
    )(*args)


def _rope_tables(seq, dtype):
    rows = seq // GRID_W
    row = jnp.repeat(jnp.arange(rows), GRID_W).astype(jnp.float32)
    col = (jnp.arange(rows * GRID_W) % GRID_W).astype(jnp.float32)
    half = HEAD_DIM // 2
    inv = jnp.power(ROPE_BASE, -jnp.arange(0, half, 2, dtype=jnp.float32) / half)
    ang_r, ang_c = row[:, None] * inv, col[:, None] * inv
    cos_r, cos_c = jnp.cos(ang_r).astype(dtype), jnp.cos(ang_c).astype(dtype)
    sin_r, sin_c = jnp.sin(ang_r).astype(dtype), jnp.sin(ang_c).astype(dtype)
    cos_h = jnp.concatenate([cos_r, cos_r, cos_c, cos_c], -1)
    sin_h = jnp.concatenate([-sin_r, sin_r, -sin_c, sin_c], -1)
    cos_t = jnp.concatenate([jnp.tile(cos_h, (1, D_ATT // HEAD_DIM)), jnp.ones((TM, D_ATT), dtype)], 0)
    sin_t = jnp.concatenate([jnp.tile(sin_h, (1, D_ATT // HEAD_DIM)), jnp.zeros((TM, D_ATT), dtype)], 0)
    return cos_t, sin_t


def kernel(x, c, ctx, c_ctx, ada_w, ada_b, norm1, norm2, w_in, w_out, ret_decay, win_qk_gain, win_sink,
           glb_qk_gain, w_router, b_router, w_gate_up, w_down):
    nb_all, seq, d = x.shape
    n_ctx = ctx.shape[1]
    depth = ada_w.shape[0]
    n_streams = N_STREAMS if nb_all % N_STREAMS == 0 and (nb_all // N_STREAMS * n_ctx) % TM == 0 else 1
    nb = nb_all // n_streams
    n_x = nb * seq
    ta = n_x + nb * n_ctx
    assert seq % TM == 0 and (nb * n_ctx) % TM == 0 and seq % GRID_W == 0
    tiles_per_seq = seq // TM
    n_x_tiles = n_x // TM
    n_tiles = ta // TM

    mod_rows = (nb_all + 1 + 7) // 8 * 8
    cc = jnp.zeros((mod_rows, d), F32).at[:nb_all].set(c).at[nb_all].set(c_ctx)
    mod_all = _ada(cc, ada_w, ada_b).reshape(depth, mod_rows, 6, d)

    cos_t, sin_t = _rope_tables(seq, F32)
    head_of = jnp.arange(D_ATT) // HEAD_DIM
    hm = jnp.where(head_of[:, None] == head_of[None, :], 1.0 / HEAD_DIM, 0.0).astype(BF16)
    wr_t = w_router.T.astype(BF16)
    b_r = b_router.astype(F32)[:, None]
    key_j = jnp.arange(WINDOW)[:, None]
    qry_i = jnp.tile(jnp.arange(WINDOW), WIN_Q_HEADS)[None, :]
    tri = jnp.stack([jnp.where(key_j >= qry_i, 0.0, NEG_INF), jnp.where(key_j <= qry_i, 0.0, NEG_INF)]).astype(F32)

    layer_tabs = []
    for l in range(depth):
        gains = jnp.zeros((8, D_ATT), F32)
        gains = gains.at[0].set(jnp.tile(win_qk_gain[l, 0], WIN_Q_HEADS))
        gains = gains.at[1, :WIN_KV_HEADS * HEAD_DIM].set(jnp.tile(win_qk_gain[l, 1], WIN_KV_HEADS))
        gains = gains.at[2].set(jnp.tile(glb_qk_gain[l, 0], GLB_Q_HEADS))
        gains = gains.at[3, :GLB_KV_HEADS * HEAD_DIM].set(jnp.tile(glb_qk_gain[l, 1], GLB_KV_HEADS))
        sink_row = jnp.repeat(win_sink[l].astype(F32) * LOG2_E, WINDOW)[None, :]
        layer_tabs.append((gains, sink_row, _ret_tables(ret_decay[l])))

    def mix(l, s, xa, pending, release=None):
        last = l == depth - 1
        gains, sink_row, (gc, dmat, xi, zeta) = layer_tabs[l]
        xa, px, vt = _inproj(xa, mods[s][l], norm1[l][None, :], w_in, l, cos_t, sin_t, gains, hm,
                             nb, n_x_tiles, tiles_per_seq, pending)
        if release is not None:
            px = release(px)
        o_f, o_b = _retention(px, gc, dmat, xi, zeta, nb, seq, n_ctx)
        yw = _window(px, vt, sink_row, tri, nb, seq, n_ctx, not last)
        yg = _global(px, vt, nb, seq, n_ctx, not last)
        if not last:
            yw, yg = _ctx_attention(px, vt, sink_row, yw, yg, nb, seq, n_ctx)
        return _outproj(o_f, o_b, px, yw, yg, xa, mods[s][l], norm2[l][None, :], w_out, l,
                        wr_t, b_r, nb, n_x_tiles, tiles_per_seq, n_x_tiles if last else n_tiles)

    def tie(a, b):
        if n_streams == 1:
            return a, b
        return lax.optimization_barrier((a, b))

    mods, xas = [], []
    for s in range(n_streams):
        b0 = s * nb
        mods.append(jnp.concatenate([mod_all[:, b0:b0 + nb], mod_all[:, nb_all:nb_all + 1]], axis=1))
        xas.append(jnp.concatenate([x[b0:b0 + nb].reshape(n_x, d), ctx[b0:b0 + nb].reshape(nb * n_ctx, d)], 0))

    out = None
    pending = [None] * n_streams
    carry = None
    s_last = n_streams - 1

    def gather_back(yp, pos):
        return yp[pos[0]], yp[pos[1]]

    for l in range(depth):
        last = l == depth - 1
        n_tok = n_x if last else ta
        states, srcs, plans, xss = [], [], [], []
        for s in range(n_streams):
            due, release = None, None
            if s > 0:
                def release(px, s=s):
                    px, src_t = tie(px, srcs[s - 1])
                    xss.append(states[s - 1][1][src_t])
                    return px
            elif carry is not None:
                due, yp_c, pos_c, gate_c, mod_c = carry
                carry = None

                def release(px, due=due, yp_c=yp_c, pos_c=pos_c, gate_c=gate_c, mod_c=mod_c):
                    px, yp_t = tie(px, yp_c)
                    pending[due] = gather_back(yp_t, pos_c) + (gate_c, mod_c)
                    return px
            state = list(mix(l, s, xas[s], pending[s], release))
            if s > 0:
                state[2], xss[s - 1] = tie(state[2], xss[s - 1])
            elif due is not None:
                y0d, y1d, gate_d, mod_d = pending[due]
                state[2], (y0d, y1d) = tie(state[2], (y0d, y1d))
                pending[due] = (y0d, y1d, gate_d, mod_d)
            states.append(state)
            tables, src, pos = _moe_plan(state[2], n_tok)
            plans.append((tables, pos))
            srcs.append(src)
        yp_prev = None
        for s in range(n_streams):
            xa, h2, _, gate_t = states[s]
            tables, pos = plans[s]
            xs = xss[s] if s < s_last else h2[srcs[s]]
            if s < s_last:
                xs, srcs[s + 1] = tie(xs, srcs[s + 1])
            if s > 0:
                xs, yp_prev = tie(xs, yp_prev)
                ps, ppos, pgate, pmod, pxa = prev_info
                done = gather_back(yp_prev, ppos) + (pgate, pmod)
                if last:
                    out = _combine_final(pxa, *done, tiles_per_seq, n_x_tiles, out, ps * n_x_tiles, n_streams * n_x)
                else:
                    pending[ps] = done
            yp_prev = _moe_ffn(*tables, xs, w_gate_up, w_down, l)
            prev_info = (s, pos, gate_t.T, mods[s][l], xa)
            xas[s] = xa
        ps, ppos, pgate, pmod, pxa = prev_info
        if last:
            out = _combine_final(pxa, *gather_back(yp_prev, ppos), pgate, pmod, tiles_per_seq, n_x_tiles, out,
                                 ps * n_x_tiles, n_streams * n_x)
        elif n_streams == 1:
            pending[ps] = gather_back(yp_prev, ppos) + (pgate, pmod)
        else:
            carry = (ps, yp_prev, ppos, pgate, pmod)
    return out.reshape(nb_all, seq, d)
```

```python
import functools

import jax
import jax.numpy as jnp
from jax import lax
from jax.experimental import pallas as pl
from jax.experimental.pallas import tpu as pltpu

GRID_W = 64
HEAD_DIM = 64
RET_HEADS = 4
RET_DK = 64
RET_DV = 128
WIN_Q_HEADS = 4
WIN_KV_HEADS = 2
WINDOW = 128
GLB_Q_HEADS = 4
GLB_KV_HEADS = 2
ROPE_BASE = 10000.0
N_EXPERTS = 16
N_GROUPS = 4
EXPERTS_PER_GROUP = N_EXPERTS // N_GROUPS
TOP_K = 2
EPS = 1e-6
NEG_INF = -1e30

RQ, RK, RV, RG = 0, 256, 512, 1024
WQ, WK, WV = 1536, 1792, 1920
GQ, GK, GV = 2048, 2304, 2432
D_IN = 2560
D_RET = RET_HEADS * RET_DV
D_ATT = WIN_Q_HEADS * HEAD_DIM
N_KV = WIN_KV_HEADS * HEAD_DIM
P_RQ, P_RK, P_RV, P_RG = 0, 256, 512, 1024
P_WQ, P_GQ, P_WK, P_GK = 1536, 1792, 2048, 2176
PX_W = 2304

TM = 512
RET_C = 256
TQ_W = 1024
TQ_G = 256
KC_G = 256
LOG2_E = 1.4426950408889634
assert GLB_KV_HEADS == 2
TM_E = 512
N_STREAMS = 2
VMEM_LIMIT = 56 * 1024 * 1024

F32 = jnp.float32
BF16 = jnp.bfloat16


def _dot(a, b):
    return jnp.dot(a, b, preferred_element_type=F32)


def _dot_nt(a, b):
    return lax.dot_general(a, b, (((1,), (1,)), ((), ())), preferred_element_type=F32)


def _dot_tn(a, b):
    return lax.dot_general(a, b, (((0,), (0,)), ((), ())), preferred_element_type=F32)


def _silu(x):
    return x * jax.nn.sigmoid(x)


def _params(sem):
    return pltpu.CompilerParams(dimension_semantics=sem, vmem_limit_bytes=VMEM_LIMIT)


def _ada_kernel(c_ref, w_ref, b_ref, o_ref):
    s = _silu(c_ref[...]).astype(BF16)
    o_ref[0] = _dot(s, w_ref[0].astype(BF16)) + b_ref[0]


def _ada(cc, ada_w, ada_b):
    depth, d, n = ada_w.shape
    tn = 1536
    rows = cc.shape[0]
    return pl.pallas_call(
        _ada_kernel,
        grid=(depth, n // tn),
        in_specs=[pl.BlockSpec((rows, d), lambda l, j: (0, 0)),
                  pl.BlockSpec((1, d, tn), lambda l, j: (l, 0, j)),
                  pl.BlockSpec((1, 1, tn), lambda l, j: (l, 0, j))],
        out_specs=pl.BlockSpec((1, rows, tn), lambda l, j: (l, 0, j)),
        out_shape=jax.ShapeDtypeStruct((depth, rows, n), F32),
        compiler_params=_params(("arbitrary", "arbitrary")),
        name="ada_mod",
    )(cc, ada_w, ada_b.reshape(depth, 1, n))


def _inproj_fused_kernel(x_ref, y0_ref, y1_ref, gate_ref, modp_ref, mod_ref, n1_ref, wf_ref, wvt_ref, cos_ref,
                         sin_ref, gains_ref, hm_ref, xo_ref, o_ref, vt_ref, w_ref):
    g = gate_ref[...]
    y = y0_ref[...].astype(F32) * g[:, 0:1] + y1_ref[...].astype(F32) * g[:, 1:2]
    x = x_ref[...] + modp_ref[0, 5:6, :] * y
    xo_ref[...] = x
    _inproj_body(x, mod_ref, n1_ref, wf_ref, wvt_ref, cos_ref, sin_ref, gains_ref, hm_ref, o_ref, vt_ref, w_ref)


def _inproj_kernel(x_ref, mod_ref, n1_ref, wf_ref, wvt_ref, cos_ref, sin_ref, gains_ref, hm_ref, o_ref, vt_ref,
                   w_ref):
    _inproj_body(x_ref[...], mod_ref, n1_ref, wf_ref, wvt_ref, cos_ref, sin_ref, gains_ref, hm_ref, o_ref, vt_ref,
                 w_ref)


def _inproj_body(x, mod_ref, n1_ref, wf_ref, wvt_ref, cos_ref, sin_ref, gains_ref, hm_ref, o_ref, vt_ref, w_ref):
    @pl.when(pl.program_id(0) == 0)
    def _():
        w_ref[...] = wf_ref[...].astype(BF16)

    ms = jnp.mean(x * x, axis=-1, keepdims=True)
    h = x * lax.rsqrt(ms + EPS) * n1_ref[...]
    h = h * (1.0 + mod_ref[0, 1:2, :]) + mod_ref[0, 0:1, :]
    hb = h.astype(BF16)

    def proj(lo, width):
        return _dot(hb, w_ref[:, lo:lo + width])

    def head_msq(y):
        sq = y * y
        sq_hi = sq.astype(BF16)
        sq_lo = (sq - sq_hi.astype(F32)).astype(BF16)
        hm = hm_ref[0:y.shape[1], 0:y.shape[1]]
        return _dot(sq_hi, hm) + _dot(sq_lo, hm)

    def qk_finish(y, msq, gain_row, scale):
        width = y.shape[1]
        yn = y * lax.rsqrt(msq + EPS) * gains_ref[gain_row:gain_row + 1, 0:width]
        nxt = pltpu.roll(yn, width - HEAD_DIM // 4, 1)
        prv = pltpu.roll(yn, HEAD_DIM // 4, 1)
        lane = lax.broadcasted_iota(jnp.int32, yn.shape, 1)
        partner = jnp.where((lane % (HEAD_DIM // 2)) < HEAD_DIM // 4, nxt, prv)
        yr = yn * cos_ref[:, 0:width] + partner * sin_ref[:, 0:width]
        return (yr * scale).astype(BF16)

    q_scale = HEAD_DIM ** -0.5 * LOG2_E
    qk_segs = ((WQ, D_ATT, 0, q_scale, P_WQ), (GQ, D_ATT, 2, q_scale, P_GQ),
               (WK, N_KV, 1, 1.0, P_WK), (GK, N_KV, 3, 1.0, P_GK))
    plain_segs = ((RQ, RK - RQ, 1.0, P_RQ), (RK, RV - RK, RET_DK ** -0.5, P_RK),
                  (RV, RG - RV, 1.0, P_RV), (RG, WQ - RG, 1.0, P_RG))
    ys = [proj(lo, width) for lo, width, _, _, _ in qk_segs]
    stats = [head_msq(y) for y in ys]
    vt_ref[...] = _dot_nt(wvt_ref[...], hb).astype(BF16)
    for (lo, width, scale, dst), (_, qwidth, gain_row, qscale, qdst), y, msq in zip(plain_segs, qk_segs, ys, stats):
        p = proj(lo, width)
        o_ref[:, dst:dst + width] = (p if scale == 1.0 else p * scale).astype(BF16)
        o_ref[:, qdst:qdst + qwidth] = qk_finish(y, msq, gain_row, qscale)


def _inproj(xa, mod_l, n1, w_in, layer, cos_t, sin_t, gains, hm, nb, n_x_tiles, tiles_per_seq, pending=None):
    ta, d = xa.shape
    n_tiles = ta // TM
    w_vt = jnp.concatenate([w_in[layer, :, WV:WV + N_KV], w_in[layer, :, GV:GV + N_KV]], axis=1).T.astype(BF16)
    n_gv = 2 * N_KV
    once = pl.Buffered(1)

    def mod_idx(i):
        return (jnp.where(i < n_x_tiles, i // tiles_per_seq, nb), 0, 0)

    def rope_idx(i):
        return (jnp.where(i < n_x_tiles, i % tiles_per_seq, tiles_per_seq), 0)

    row = lambda i: (i, 0)
    in_specs = [pl.BlockSpec((1, 6, d), mod_idx),
                pl.BlockSpec((1, d), lambda i: (0, 0)),
                pl.BlockSpec((None, d, D_IN), lambda i: (layer, 0, 0), pipeline_mode=once),
                pl.BlockSpec((n_gv, d), lambda i: (0, 0), pipeline_mode=once),
                pl.BlockSpec((TM, D_ATT), rope_idx),
                pl.BlockSpec((TM, D_ATT), rope_idx),
                pl.BlockSpec((8, D_ATT), lambda i: (0, 0)),
                pl.BlockSpec((D_ATT, D_ATT), lambda i: (0, 0))]
    args = [mod_l, n1, w_in, w_vt, cos_t, sin_t, gains, hm]
    out_specs = [pl.BlockSpec((TM, PX_W), row), pl.BlockSpec((n_gv, TM), lambda i: (0, i))]
    out_shape = [jax.ShapeDtypeStruct((ta, PX_W), BF16), jax.ShapeDtypeStruct((n_gv, ta), BF16)]
    if pending is None:
        kern, aliases = _inproj_kernel, {}
        in_specs = [pl.BlockSpec((TM, d), row)] + in_specs
        args = [xa] + args
    else:
        y0, y1, gate, mod_prev = pending
        kern, aliases = _inproj_fused_kernel, {0: 0}
        in_specs = [pl.BlockSpec((TM, d), row), pl.BlockSpec((TM, d), row), pl.BlockSpec((TM, d), row),
                    pl.BlockSpec((TM, TOP_K), row), pl.BlockSpec((1, 6, d), mod_idx)] + in_specs
        args = [xa, y0, y1, gate, mod_prev] + args
        out_specs = [pl.BlockSpec((TM, d), row)] + out_specs
        out_shape = [jax.ShapeDtypeStruct((ta, d), F32)] + out_shape
    res = pl.pallas_call(
        kern,
        grid=(n_tiles,),
        in_specs=in_specs,
        out_specs=out_specs,
        out_shape=out_shape,
        scratch_shapes=[pltpu.VMEM((d, D_IN), BF16)],
        input_output_aliases=aliases,
        compiler_params=_params(("arbitrary",)),
        name="in_proj",
    )(*args)
    return (xa,) + tuple(res) if pending is None else tuple(res)


def _ret_kernel(gc_ref, qf_ref, kf_ref, vf_ref, qb_ref, kb_ref, vb_ref, dmat_ref, xi_ref, zeta_ref,
                of_ref, ob_ref, s_ref):
    @pl.when(pl.program_id(1) == 0)
    def _():
        s_ref[...] = jnp.zeros_like(s_ref)

    dirs = ((qf_ref, kf_ref, vf_ref, of_ref), (qb_ref, kb_ref, vb_ref, ob_ref))
    work = []
    for d, (q_ref, k_ref, v_ref, o_ref) in enumerate(dirs):
        q = q_ref[...]
        k = k_ref[...]
        v = v_ref[...]
        kz = (k.astype(F32) * zeta_ref[d]).astype(BF16)
        for h in range(RET_HEADS):
            i = d * RET_HEADS + h
            qh = q[:, h * RET_DK:(h + 1) * RET_DK]
            kh = k[:, h * RET_DK:(h + 1) * RET_DK]
            vh = v[:, h * RET_DV:(h + 1) * RET_DV]
            state = s_ref[i]
            att = _dot_nt(qh, kh)
            carried = _dot(qh, state.astype(BF16))
            update = _dot_tn(kz[:, h * RET_DK:(h + 1) * RET_DK], vh)
            work.append((i, vh, state, att, carried, update))
    outs = []
    for i, vh, state, att, carried, update in work:
        outs.append(_dot((att * dmat_ref[i]).astype(BF16), vh) + carried * xi_ref[i])
        s_ref[i] = gc_ref[i] * state + update
    for d, (_, _, _, o_ref) in enumerate(dirs):
        o_ref[...] = jnp.concatenate(outs[d * RET_HEADS:(d + 1) * RET_HEADS], axis=-1).astype(BF16)


def _retention(px, gc, dmat, xi, zeta, nb, seq, n_ctx):
    ta = px.shape[0]
    n_x = seq // RET_C
    n_c = n_ctx // RET_C
    ctx_base = nb * n_x
    steps = n_c + n_x

    def row_f(b, c):
        return jnp.where(c < n_c, ctx_base + b * n_c + c, b * n_x + (c - n_c))

    def row_b(b, c):
        return jnp.where(c < n_c, ctx_base + b * n_c + (n_c - 1 - c), b * n_x + (steps - 1 - c))

    def spec(width, col, row):
        return pl.BlockSpec((RET_C, width), lambda b, c: (row(b, c), col))

    const3 = lambda b, c: (0, 0, 0)
    return pl.pallas_call(
        _ret_kernel,
        grid=(nb, steps),
        in_specs=[pl.BlockSpec(memory_space=pltpu.SMEM),
                  spec(256, P_RQ // 256, row_f), spec(256, P_RK // 256, row_f), spec(D_RET, P_RV // D_RET, row_f),
                  spec(256, P_RQ // 256, row_b), spec(256, P_RK // 256, row_b), spec(D_RET, P_RV // D_RET, row_b),
                  pl.BlockSpec((2 * RET_HEADS, RET_C, RET_C), const3),
                  pl.BlockSpec((2 * RET_HEADS, RET_C, RET_DV), const3),
                  pl.BlockSpec((2, RET_C, RET_HEADS * RET_DK), const3)],
        out_specs=[spec(D_RET, 0, row_f), spec(D_RET, 0, row_b)],
        out_shape=[jax.ShapeDtypeStruct((ta, D_RET), BF16)] * 2,
        scratch_shapes=[pltpu.VMEM((2 * RET_HEADS, RET_DK, RET_DV), F32)],
        compiler_params=_params(("parallel", "arbitrary")),
        name="retention",
    )(gc, px, px, px, px, px, px, dmat, xi, zeta)


def _ret_tables(decay_logit):
    log_g = jax.nn.log_sigmoid(decay_logit.astype(F32)).reshape(2 * RET_HEADS)
    idx = jnp.arange(RET_C, dtype=F32)
    diff = idx[:, None] - idx[None, :]
    lg = log_g[:, None, None]
    d_fwd = jnp.where(diff >= 0, jnp.exp(jnp.maximum(diff, 0.0) * lg), 0.0)
    d_bwd = jnp.where(diff <= 0, jnp.exp(jnp.maximum(-diff, 0.0) * lg), 0.0)
    is_bwd = (jnp.arange(2 * RET_HEADS) >= RET_HEADS)[:, None, None]
    dmat = jnp.where(is_bwd, d_bwd, d_fwd)
    pos = jnp.where(is_bwd[:, :, 0], RET_C - 1.0 - idx[None, :], idx[None, :])
    xi = jnp.exp((pos + 1.0) * log_g[:, None])
    zeta = jnp.exp((RET_C - 1.0 - pos) * log_g[:, None])
    gc = jnp.exp(RET_C * log_g)
    xi = jnp.broadcast_to(xi[:, :, None], (2 * RET_HEADS, RET_C, RET_DV))
    zeta = jnp.repeat(zeta.reshape(2, RET_HEADS, RET_C).transpose(0, 2, 1), RET_DK, axis=-1)
    return gc, dmat, xi, zeta


def _pad_heads(q, n_kv):
    rows, width = q.shape
    group = width // HEAD_DIM // n_kv
    zeros = jnp.zeros((rows, HEAD_DIM), BF16)
    out = []
    for h in range(n_kv):
        for g in range(group):
            qh = q[:, (h * group + g) * HEAD_DIM:(h * group + g + 1) * HEAD_DIM]
            out.append(jnp.concatenate([qh if hh == h else zeros for hh in range(n_kv)], axis=1))
    return jnp.concatenate(out, axis=0)


def _fill_values(v_ext, lo, vt):
    n = vt.shape[1]
    for h in range(v_ext.shape[0]):
        v_ext[h, 0:HEAD_DIM, lo:lo + n] = vt[h * HEAD_DIM:(h + 1) * HEAD_DIM, :]


def _store_heads(o_ref, col, h, group, oe, den):
    n = oe.shape[1] // group
    o = oe[0:HEAD_DIM] / den
    for g in range(group):
        r0 = (h * group + g) * HEAD_DIM
        o_ref[r0:r0 + HEAD_DIM, col:col + n] = o[:, g * n:(g + 1) * n].astype(BF16)


def _win_kernel(q_ref, k_ref, kc_ref, vt_ref, vtc_ref, sink_ref, tri_ref, o_ref, k_pad, v_ext, *, seq, n_ctx):
    i = pl.program_id(1)
    group = WIN_Q_HEADS // WIN_KV_HEADS
    sub = WINDOW
    w = group * sub
    span = 3 * WINDOW
    c0 = seq + 2 * WINDOW
    n_sub = q_ref.shape[0] // sub

    @pl.when(i == 0)
    def _():
        zk = jnp.zeros((WINDOW, N_KV), BF16)
        k_pad[0:WINDOW, :] = zk
        k_pad[WINDOW:WINDOW + seq, :] = k_ref[...]
        k_pad[WINDOW + seq:c0, :] = zk
        k_pad[c0:c0 + n_ctx, :] = kc_ref[...]
        v_ext[:, 0:HEAD_DIM, 0:WINDOW] = jnp.zeros((WIN_KV_HEADS, HEAD_DIM, WINDOW), BF16)
        v_ext[:, 0:HEAD_DIM, WINDOW + seq:c0] = jnp.zeros((WIN_KV_HEADS, HEAD_DIM, WINDOW), BF16)
        _fill_values(v_ext, WINDOW, vt_ref[...])
        _fill_values(v_ext, c0, vtc_ref[...])
        v_ext[:, HEAD_DIM:2 * HEAD_DIM, :] = jnp.ones((WIN_KV_HEADS, HEAD_DIM, c0 + n_ctx), BF16)

    sink = sink_ref[...]
    kc = k_pad[c0:c0 + n_ctx, :]
    start = i * q_ref.shape[0]

    def scores(a):
        qs = pl.multiple_of(start + a * sub, sub)
        q4 = _pad_heads(q_ref[a * sub:(a + 1) * sub, :], WIN_KV_HEADS)
        return a, qs, _dot_nt(k_pad[pl.ds(qs, span), :], q4), _dot_nt(kc, q4)

    def softmax(a, qs, st, sc):
        lo_edge = jnp.where(qs == 0, NEG_INF, 0.0)
        hi_edge = jnp.where(qs + sub == seq, NEG_INF, 0.0)
        s0 = st[0:WINDOW] + (tri_ref[0] + lo_edge)
        s1 = st[WINDOW:2 * WINDOW]
        s2 = st[2 * WINDOW:span] + (tri_ref[1] + hi_edge)
        m = jnp.maximum(jnp.maximum(jnp.max(s0, axis=0, keepdims=True), jnp.max(s1, axis=0, keepdims=True)),
                        jnp.maximum(jnp.max(s2, axis=0, keepdims=True), jnp.max(sc, axis=0, keepdims=True)))
        m = jnp.maximum(m, sink)
        pw = jnp.concatenate([jnp.exp2(s0 - m), jnp.exp2(s1 - m), jnp.exp2(s2 - m)], axis=0).astype(BF16)
        return a, qs, m, pw, jnp.exp2(sc - m).astype(BF16)

    def values(a, qs, m, pw, pc):
        for h in range(WIN_KV_HEADS):
            cs = slice(h * w, (h + 1) * w)
            oe = _dot(v_ext[h, :, pl.ds(qs, span)], pw[:, cs]) + _dot(v_ext[h, :, c0:c0 + n_ctx], pc[:, cs])
            den = oe[HEAD_DIM:HEAD_DIM + 1] + jnp.exp2(sink[:, cs] - m[:, cs])
            _store_heads(o_ref, a * sub, h, group, oe, den)

    nxt = scores(0)
    pend = None
    for a in range(n_sub):
        cur = nxt
        if a + 1 < n_sub:
            nxt = scores(a + 1)
        if pend is not None:
            values(*pend)
        pend = softmax(*cur)
    values(*pend)


def _window(px, vt, sink_row, tri, nb, seq, n_ctx, with_ctx_cols):
    ta = px.shape[0] if with_ctx_cols else nb * seq
    assert seq % TQ_W == 0 and TQ_W % WINDOW == 0
    n_q = seq // TQ_W
    ctx_rows = (nb * seq) // n_ctx
    n_keys = seq + 2 * WINDOW + n_ctx
    return pl.pallas_call(
        functools.partial(_win_kernel, seq=seq, n_ctx=n_ctx),
        grid=(nb, n_q),
        in_specs=[pl.BlockSpec((TQ_W, D_ATT), lambda b, i: (b * n_q + i, P_WQ // D_ATT)),
                  pl.BlockSpec((seq, N_KV), lambda b, i: (b, P_WK // N_KV)),
                  pl.BlockSpec((n_ctx, N_KV), lambda b, i: (ctx_rows + b, P_WK // N_KV)),
                  pl.BlockSpec((N_KV, seq), lambda b, i: (0, b)),
                  pl.BlockSpec((N_KV, n_ctx), lambda b, i: (0, ctx_rows + b)),
                  pl.BlockSpec((1, WIN_Q_HEADS * WINDOW), lambda b, i: (0, 0)),
                  pl.BlockSpec((2, WINDOW, WIN_Q_HEADS * WINDOW), lambda b, i: (0, 0, 0))],
        out_specs=pl.BlockSpec((D_ATT, TQ_W), lambda b, i: (0, b * n_q + i)),
        out_shape=jax.ShapeDtypeStruct((D_ATT, ta), BF16),
        scratch_shapes=[pltpu.VMEM((n_keys, N_KV), BF16),
                        pltpu.VMEM((WIN_KV_HEADS, 2 * HEAD_DIM, n_keys), BF16)],
        compiler_params=_params(("parallel", "arbitrary")),
        name="window_attn",
    )(px, px, px, vt, vt, sink_row, tri)


def _ctx_kernel(qw_ref, kw_ref, qg_ref, kg_ref, vt_ref, sink_ref, yw_in, yg_in, yw_ref, yg_ref):
    del yw_in, yg_in
    n_ctx = qw_ref.shape[0]
    sub = WINDOW
    ones = jnp.ones((HEAD_DIM, n_ctx), BF16)
    for q_ref, k_ref, row0, o_ref, sink in ((qw_ref, kw_ref, 0, yw_ref, sink_ref[...]),
                                            (qg_ref, kg_ref, N_KV, yg_ref, None)):
        n_kv = k_ref.shape[1] // HEAD_DIM
        group = q_ref.shape[1] // HEAD_DIM // n_kv
        w = group * sub
        k = k_ref[...]
        for a in range(n_ctx // sub):
            sc = _dot_nt(k, _pad_heads(q_ref[a * sub:(a + 1) * sub, :], n_kv))
            m = jnp.max(sc, axis=0, keepdims=True)
            if sink is not None:
                m = jnp.maximum(m, sink)
            p = jnp.exp2(sc - m).astype(BF16)
            for h in range(n_kv):
                cs = slice(h * w, (h + 1) * w)
                ve = jnp.concatenate([vt_ref[row0 + h * HEAD_DIM:row0 + (h + 1) * HEAD_DIM, :], ones], axis=0)
                oe = _dot(ve, p[:, cs])
                den = oe[HEAD_DIM:HEAD_DIM + 1]
                if sink is not None:
                    den = den + jnp.exp2(sink[:, cs] - m[:, cs])
                _store_heads(o_ref, a * sub, h, group, oe, den)


def _ctx_attention(px, vt, sink_row, ywt, ygt, nb, seq, n_ctx):
    ctx_rows = (nb * seq) // n_ctx
    row = lambda col: (lambda b: (ctx_rows + b, col))
    any_spec = pl.BlockSpec(memory_space=pl.ANY)
    out_spec = pl.BlockSpec((D_ATT, n_ctx), lambda b: (0, ctx_rows + b))
    return pl.pallas_call(
        _ctx_kernel,
        grid=(nb,),
        in_specs=[pl.BlockSpec((n_ctx, D_ATT), row(P_WQ // D_ATT)),
                  pl.BlockSpec((n_ctx, N_KV), row(P_WK // N_KV)),
                  pl.BlockSpec((n_ctx, D_ATT), row(P_GQ // D_ATT)),
                  pl.BlockSpec((n_ctx, N_KV), row(P_GK // N_KV)),
                  pl.BlockSpec((2 * N_KV, n_ctx), lambda b: (0, ctx_rows + b)),
                  pl.BlockSpec((1, WIN_Q_HEADS * WINDOW), lambda b: (0, 0)),
                  any_spec, any_spec],
        out_specs=[out_spec, out_spec],
        out_shape=[jax.ShapeDtypeStruct(ywt.shape, BF16), jax.ShapeDtypeStruct(ygt.shape, BF16)],
        input_output_aliases={6: 0, 7: 1},
        compiler_params=_params(("parallel",)),
        name="ctx_attn",
    )(px, px, px, px, vt, sink_row, ywt, ygt)


def _glb_kernel(q_ref, k_ref, kc_ref, vt_ref, vtc_ref, o_ref, k_all, v_ext, *, seq, n_ctx):
    i = pl.program_id(1)
    group = GLB_Q_HEADS // GLB_KV_HEADS
    tq = q_ref.shape[0]
    w = group * tq

    @pl.when(i == 0)
    def _():
        k_all[0:seq, :] = k_ref[...]
        k_all[seq:seq + n_ctx, :] = kc_ref[...]
        _fill_values(v_ext, 0, vt_ref[...])
        _fill_values(v_ext, seq, vtc_ref[...])
        v_ext[:, HEAD_DIM:2 * HEAD_DIM, :] = jnp.ones((GLB_KV_HEADS, HEAD_DIM, seq + n_ctx), BF16)

    def attend(chunks):
        q4 = _pad_heads(q_ref[...], GLB_KV_HEADS)

        def scores(chunk):
            lo, nk = chunk
            return _dot_nt(k_all[lo:lo + nk, :], q4)

        m = None
        acc = [None] * GLB_KV_HEADS

        def accumulate(pend):
            pt, alpha, (lo, nk) = pend
            for h in range(GLB_KV_HEADS):
                pv = _dot(v_ext[h, :, lo:lo + nk], pt[:, h * w:(h + 1) * w])
                acc[h] = pv if alpha is None else acc[h] * alpha[:, h * w:(h + 1) * w] + pv

        st_next = scores(chunks[0])
        pend = None
        for ci, chunk in enumerate(chunks):
            st = st_next
            if ci + 1 < len(chunks):
                st_next = scores(chunks[ci + 1])
            if pend is not None:
                accumulate(pend)
            cm = jnp.max(st, axis=0, keepdims=True)
            m_new = cm if m is None else jnp.maximum(m, cm)
            pt = jnp.exp2(st - m_new).astype(BF16)
            alpha = None if m is None else jnp.exp2(m - m_new)
            pend = (pt, alpha, chunk)
            m = m_new
        accumulate(pend)
        for h in range(GLB_KV_HEADS):
            _store_heads(o_ref, 0, h, group, acc[h], acc[h][HEAD_DIM:HEAD_DIM + 1])

    attend([(c * KC_G, KC_G) for c in range(seq // KC_G)] + [(seq, n_ctx)])


def _global(px, vt, nb, seq, n_ctx, with_ctx_cols):
    ta = px.shape[0] if with_ctx_cols else nb * seq
    assert seq % KC_G == 0 and seq % TQ_G == 0
    n_q = seq // TQ_G
    ctx_rows = (nb * seq) // n_ctx
    return pl.pallas_call(
        functools.partial(_glb_kernel, seq=seq, n_ctx=n_ctx),
        grid=(nb, n_q),
        in_specs=[pl.BlockSpec((TQ_G, D_ATT), lambda b, i: (b * n_q + i, P_GQ // D_ATT)),
                  pl.BlockSpec((seq, N_KV), lambda b, i: (b, P_GK // N_KV)),
                  pl.BlockSpec((n_ctx, N_KV), lambda b, i: (ctx_rows + b, P_GK // N_KV)),
                  pl.BlockSpec((N_KV, seq), lambda b, i: (1, b)),
                  pl.BlockSpec((N_KV, n_ctx), lambda b, i: (1, ctx_rows + b))],
        out_specs=pl.BlockSpec((D_ATT, TQ_G), lambda b, i: (0, b * n_q + i)),
        out_shape=jax.ShapeDtypeStruct((D_ATT, ta), BF16),
        scratch_shapes=[pltpu.VMEM((seq + n_ctx, N_KV), BF16),
                        pltpu.VMEM((GLB_KV_HEADS, 2 * HEAD_DIM, seq + n_ctx), BF16)],
        compiler_params=_params(("parallel", "arbitrary")),
        name="global_attn",
    )(px, px, px, vt, vt)


def _route_rows(logits, bias):
    sig = jax.nn.sigmoid(logits)
    biased = sig + bias
    b_rows = [biased[e:e + 1, :] for e in range(N_EXPERTS)]
    s_rows = [sig[e:e + 1, :] for e in range(N_EXPERTS)]
    n_loc = EXPERTS_PER_GROUP

    best_score, grp = None, None
    for g in range(N_GROUPS):
        a = b_rows[g * n_loc:(g + 1) * n_loc]
        top2 = None
        for i in range(n_loc):
            for j in range(i + 1, n_loc):
                pair = a[i] + a[j]
                top2 = pair if top2 is None else jnp.maximum(top2, pair)
        if g == 0:
            best_score, grp = top2, jnp.zeros(top2.shape, jnp.int32)
        else:
            upd = top2 > best_score
            grp = jnp.where(upd, g, grp)
            best_score = jnp.where(upd, top2, best_score)

    def pick(rows, i):
        out = rows[i]
        for g in range(1, N_GROUPS):
            out = jnp.where(grp == g, rows[g * n_loc + i], out)
        return out

    cand = [pick(b_rows, i) for i in range(n_loc)]
    cand_s = [pick(s_rows, i) for i in range(n_loc)]
    m1, l1, w1 = cand[0], jnp.zeros(grp.shape, jnp.int32), cand_s[0]
    for i in range(1, n_loc):
        upd = cand[i] > m1
        m1 = jnp.where(upd, cand[i], m1)
        l1 = jnp.where(upd, i, l1)
        w1 = jnp.where(upd, cand_s[i], w1)
    m2, l2, w2 = None, None, None
    for i in range(n_loc):
        rest = jnp.where(l1 == i, -jnp.inf, cand[i])
        if i == 0:
            m2, l2, w2 = rest, jnp.zeros(grp.shape, jnp.int32), cand_s[0]
        else:
            upd = rest > m2
            m2 = jnp.where(upd, rest, m2)
            l2 = jnp.where(upd, i, l2)
            w2 = jnp.where(upd, cand_s[i], w2)
    tot = w1 + w2
    e_idx = jnp.concatenate([grp * n_loc + l1, grp * n_loc + l2], axis=0)
    gate = jnp.concatenate([w1 / tot, w2 / tot], axis=0)
    return e_idx, gate


def _outproj_kernel(of_ref, ob_ref, g_ref, ywt_ref, ygt_ref, x_ref, mod_ref, n2_ref, wf_ref, wr_ref, br_ref,
                    xo_ref, h_ref, e_ref, gate_ref, w_ref):
    @pl.when(pl.program_id(0) == 0)
    def _():
        w_ref[...] = wf_ref[...].astype(BF16)

    def ret_out(r):
        o = of_ref[r, :].astype(F32) + ob_ref[r, :].astype(F32)
        normed = []
        for h in range(RET_HEADS):
            oh = o[:, h * RET_DV:(h + 1) * RET_DV]
            mu = jnp.mean(oh, axis=-1, keepdims=True)
            var = jnp.mean(jnp.square(oh - mu), axis=-1, keepdims=True)
            normed.append((oh - mu) * lax.rsqrt(var + EPS))
        return (_silu(g_ref[r, :].astype(F32)) * jnp.concatenate(normed, axis=-1)).astype(BF16)

    def project(r, yr):
        acc = _dot_tn(ywt_ref[:, r], w_ref[D_RET:D_RET + D_ATT, :])
        acc += _dot_tn(ygt_ref[:, r], w_ref[D_RET + D_ATT:D_RET + 2 * D_ATT, :])
        return acc + _dot(yr, w_ref[0:D_RET, :])

    def residual_norm(r, acc):
        x = x_ref[r, :] + mod_ref[0, 2:3, :] * acc
        xo_ref[r, :] = x
        ms = jnp.mean(x * x, axis=-1, keepdims=True)
        h2 = x * lax.rsqrt(ms + EPS) * n2_ref[...]
        h2 = (h2 * (1.0 + mod_ref[0, 4:5, :]) + mod_ref[0, 3:4, :]).astype(BF16)
        h_ref[r, :] = h2
        return h2

    def route(r, h2):
        e_idx, gate = _route_rows(_dot_nt(wr_ref[...], h2), br_ref[...])
        e_ref[:, r] = e_idx
        gate_ref[:, r] = gate

    half = x_ref.shape[0] // 2
    ra, rb = slice(0, half), slice(half, 2 * half)
    yr_a = ret_out(ra)
    acc_a = project(ra, yr_a)
    yr_b = ret_out(rb)
    h2_a = residual_norm(ra, acc_a)
    acc_b = project(rb, yr_b)
    route(ra, h2_a)
    h2_b = residual_norm(rb, acc_b)
    route(rb, h2_b)


def _outproj(o_f, o_b, px, yw, yg, xa, mod_l, n2, w_out, layer, wr_t, b_r, nb, n_x_tiles, tiles_per_seq, n_tiles):
    ta, d = xa.shape
    once = pl.Buffered(1)

    def mod_idx(i):
        return (jnp.where(i < n_x_tiles, i // tiles_per_seq, nb), 0, 0)

    row = lambda i: (i, 0)
    return pl.pallas_call(
        _outproj_kernel,
        grid=(n_tiles,),
        in_specs=[pl.BlockSpec((TM, D_RET), row),
                  pl.BlockSpec((TM, D_RET), row),
                  pl.BlockSpec((TM, D_RET), lambda i: (i, P_RG // D_RET)),
                  pl.BlockSpec((D_ATT, TM), lambda i: (0, i)),
                  pl.BlockSpec((D_ATT, TM), lambda i: (0, i)),
                  pl.BlockSpec((TM, d), row),
                  pl.BlockSpec((1, 6, d), mod_idx),
                  pl.BlockSpec((1, d), lambda i: (0, 0)),
                  pl.BlockSpec((None, d, d), lambda i: (layer, 0, 0), pipeline_mode=once),
                  pl.BlockSpec((N_EXPERTS, d), lambda i: (0, 0)),
                  pl.BlockSpec((N_EXPERTS, 1), lambda i: (0, 0))],
        out_specs=[pl.BlockSpec((TM, d), row),
                   pl.BlockSpec((TM, d), row),
                   pl.BlockSpec((TOP_K, TM), lambda i: (0, i)),
                   pl.BlockSpec((TOP_K, TM), lambda i: (0, i))],
        out_shape=[jax.ShapeDtypeStruct((ta, d), F32),
                   jax.ShapeDtypeStruct((n_tiles * TM, d), BF16),
                   jax.ShapeDtypeStruct((TOP_K, n_tiles * TM), jnp.int32),
                   jax.ShapeDtypeStruct((TOP_K, n_tiles * TM), F32)],
        scratch_shapes=[pltpu.VMEM((d, d), BF16)],
        input_output_aliases={5: 0},
        compiler_params=_params(("arbitrary",)),
        name="out_proj",
    )(o_f, o_b, px, yw, yg, xa, mod_l, n2, w_out, wr_t, b_r)


def _moe_kernel(be_ref, first_ref, nu_ref, x_ref, wgu_f_ref, wd_f_ref, o_ref, wgu_ref, wd_ref):
    i = pl.program_id(0)

    @pl.when(jnp.logical_and(i < nu_ref[0], first_ref[i] == 1))
    def _():
        wgu_ref[...] = wgu_f_ref[...].astype(BF16)
        wd_ref[...] = wd_f_ref[...].astype(BF16)

    @pl.when(i < nu_ref[0])
    def _():
        f = wd_ref.shape[0]
        au = _dot(x_ref[...], wgu_ref[...])
        mid = (_silu(au[:, :f]) * au[:, f:]).astype(BF16)
        o_ref[...] = _dot(mid, wd_ref[...]).astype(BF16)

    @pl.when(i >= nu_ref[0])
    def _():
        o_ref[...] = jnp.zeros_like(o_ref)


def _moe_ffn(blk_e, first, n_used, xs, w_gu, w_down, layer):
    rows, d = xs.shape
    f2 = w_gu.shape[3]
    f = w_down.shape[2]
    grid_spec = pltpu.PrefetchScalarGridSpec(
        num_scalar_prefetch=3,
        grid=(rows // TM_E,),
        in_specs=[pl.BlockSpec((TM_E, d), lambda i, be, fi, nu: (i, 0)),
                  pl.BlockSpec((None, None, d, f2), lambda i, be, fi, nu: (layer, be[i], 0, 0)),
                  pl.BlockSpec((None, None, f, d), lambda i, be, fi, nu: (layer, be[i], 0, 0))],
        out_specs=pl.BlockSpec((TM_E, d), lambda i, be, fi, nu: (i, 0)),
        scratch_shapes=[pltpu.VMEM((d, f2), BF16), pltpu.VMEM((f, d), BF16)],
    )
    return pl.pallas_call(
        _moe_kernel,
        grid_spec=grid_spec,
        out_shape=jax.ShapeDtypeStruct((rows, d), BF16),
        compiler_params=_params(("arbitrary",)),
        name="moe_ffn",
    )(blk_e, first, n_used, xs, w_gu, w_down)


def _moe_plan(e_idx_t, n_tok):
    n_asg = n_tok * TOP_K
    flat_e = e_idx_t[:, :n_tok].reshape(-1)
    onehot = (flat_e[:, None] == jnp.arange(N_EXPERTS, dtype=jnp.int32)[None, :]).astype(jnp.int32)
    csum = jnp.cumsum(onehot, axis=0)
    counts = csum[-1]
    rank = jnp.sum(csum * onehot, axis=1) - 1
    padded = (counts + TM_E - 1) // TM_E * TM_E
    pad_end = jnp.cumsum(padded)
    pad_start = pad_end - padded
    cnt_start = jnp.cumsum(counts) - counts
    pos = (jnp.sum(pad_start[None, :] * onehot, axis=1) + rank).reshape(TOP_K, n_tok)
    n_blocks = (n_asg + N_EXPERTS * (TM_E - 1) + TM_E - 1) // TM_E
    blk_start = jnp.arange(n_blocks, dtype=jnp.int32) * TM_E
    blk_e = jnp.minimum(jnp.sum(blk_start[:, None] >= pad_end[None, :], axis=1), N_EXPERTS - 1).astype(jnp.int32)
    first = jnp.concatenate([jnp.ones((1,), jnp.int32), (blk_e[1:] != blk_e[:-1]).astype(jnp.int32)])
    n_used = (pad_end[-1] // TM_E).astype(jnp.int32).reshape(1)
    bits = max(1, (n_asg - 1).bit_length())
    order = jnp.sort((flat_e << bits) | jnp.arange(n_asg, dtype=jnp.int32)) & ((1 << bits) - 1)
    e_row = jnp.repeat(blk_e, TM_E)
    p = jnp.arange(n_blocks * TM_E, dtype=jnp.int32)
    r = p - pad_start[e_row]
    src = jnp.where(r < counts[e_row], order[jnp.clip(cnt_start[e_row] + r, 0, n_asg - 1)], p) % n_tok
    return (blk_e, first, n_used), src, pos


def _combine_kernel(x_ref, y0_ref, y1_ref, gate_ref, mod_ref, o_ref):
    g = gate_ref[...]
    y = y0_ref[...].astype(F32) * g[:, 0:1] + y1_ref[...].astype(F32) * g[:, 1:2]
    o_ref[...] = x_ref[...] + mod_ref[0, 5:6, :] * y


def _combine_final_kernel(x_ref, y0_ref, y1_ref, gate_ref, mod_ref, prev_ref, o_ref):
    del prev_ref
    _combine_kernel(x_ref, y0_ref, y1_ref, gate_ref, mod_ref, o_ref)


def _combine_final(xa, y0, y1, gate, mod_l, tiles_per_seq, n_tiles, prev, tile_off, total_rows):
    d = xa.shape[1]
    row = lambda i: (i, 0)
    ins = [pl.BlockSpec((TM, d), row), pl.BlockSpec((TM, d), row), pl.BlockSpec((TM, d), row),
           pl.BlockSpec((TM, TOP_K), row), pl.BlockSpec((1, 6, d), lambda i: (i // tiles_per_seq, 0, 0))]
    args = [xa, y0, y1, gate, mod_l]
    if prev is not None:
        ins.append(pl.BlockSpec(memory_space=pl.ANY))
        args.append(prev)
    return pl.pallas_call(
        _combine_kernel if prev is None else _combine_final_kernel,
        grid=(n_tiles,),
        in_specs=ins,
        out_specs=pl.BlockSpec((TM, d), lambda i: (tile_off + i, 0)),
        out_shape=jax.ShapeDtypeStruct((total_rows, d), F32),
        input_output_aliases={} if prev is None else {5: 0},
        compiler_params=_params(("parallel",)),
        name="moe_combine_out",
    )(*args)


def _rope_tables(seq, dtype):
    rows = seq // GRID_W
    row = jnp.repeat(jnp.arange(rows), GRID_W).astype(jnp.float32)
    col = (jnp.arange(rows * GRID_W) % GRID_W).astype(jnp.float32)
    half = HEAD_DIM // 2
    inv = jnp.power(ROPE_BASE, -jnp.arange(0, half, 2, dtype=jnp.float32) / half)
    ang_r, ang_c = row[:, None] * inv, col[:, None] * inv
    cos_r, cos_c = jnp.cos(ang_r).astype(dtype), jnp.cos(ang_c).astype(dtype)
    sin_r, sin_c = jnp.sin(ang_r).astype(dtype), jnp.sin(ang_c).astype(dtype)
    cos_h = jnp.concatenate([cos_r, cos_r, cos_c, cos_c], -1)
    sin_h = jnp.concatenate([-sin_r, sin_r, -sin_c, sin_c], -1)
    cos_t = jnp.concatenate([jnp.tile(cos_h, (1, D_ATT // HEAD_DIM)), jnp.ones((TM, D_ATT), dtype)], 0)
    sin_t = jnp.concatenate([jnp.tile(sin_h, (1, D_ATT // HEAD_DIM)), jnp.zeros((TM, D_ATT), dtype)], 0)
    return cos_t, sin_t


def kernel(x, c, ctx, c_ctx, ada_w, ada_b, norm1, norm2, w_in, w_out, ret_decay, win_qk_gain, win_sink,
           glb_qk_gain, w_router, b_router, w_gate_up, w_down):
    nb_all, seq, d = x.shape
    n_ctx = ctx.shape[1]
    depth = ada_w.shape[0]
    n_streams = N_STREAMS if nb_all % N_STREAMS == 0 and (nb_all // N_STREAMS * n_ctx) % TM == 0 else 1
    nb = nb_all // n_streams
    n_x = nb * seq
    ta = n_x + nb * n_ctx
    assert seq % TM == 0 and (nb * n_ctx) % TM == 0 and seq % GRID_W == 0
    tiles_per_seq = seq // TM
    n_x_tiles = n_x // TM
    n_tiles = ta // TM

    mod_rows = (nb_all + 1 + 7) // 8 * 8
    cc = jnp.zeros((mod_rows, d), F32).at[:nb_all].set(c).at[nb_all].set(c_ctx)
    mod_all = _ada(cc, ada_w, ada_b).reshape(depth, mod_rows, 6, d)

    cos_t, sin_t = _rope_tables(seq, F32)
    head_of = jnp.arange(D_ATT) // HEAD_DIM
    hm = jnp.where(head_of[:, None] == head_of[None, :], 1.0 / HEAD_DIM, 0.0).astype(BF16)
    wr_t = w_router.T.astype(BF16)
    b_r = b_router.astype(F32)[:, None]
    key_j = jnp.arange(WINDOW)[:, None]
    qry_i = jnp.tile(jnp.arange(WINDOW), WIN_Q_HEADS)[None, :]
    tri = jnp.stack([jnp.where(key_j >= qry_i, 0.0, NEG_INF), jnp.where(key_j <= qry_i, 0.0, NEG_INF)]).astype(F32)

    layer_tabs = []
    for l in range(depth):
        gains = jnp.zeros((8, D_ATT), F32)
        gains = gains.at[0].set(jnp.tile(win_qk_gain[l, 0], WIN_Q_HEADS))
        gains = gains.at[1, :WIN_KV_HEADS * HEAD_DIM].set(jnp.tile(win_qk_gain[l, 1], WIN_KV_HEADS))
        gains = gains.at[2].set(jnp.tile(glb_qk_gain[l, 0], GLB_Q_HEADS))
        gains = gains.at[3, :GLB_KV_HEADS * HEAD_DIM].set(jnp.tile(glb_qk_gain[l, 1], GLB_KV_HEADS))
        sink_row = jnp.repeat(win_sink[l].astype(F32) * LOG2_E, WINDOW)[None, :]
        layer_tabs.append((gains, sink_row, _ret_tables(ret_decay[l])))

    def mix(l, s, xa, pending, release=None):
        last = l == depth - 1
        gains, sink_row, (gc, dmat, xi, zeta) = layer_tabs[l]
        xa, px, vt = _inproj(xa, mods[s][l], norm1[l][None, :], w_in, l, cos_t, sin_t, gains, hm,
                             nb, n_x_tiles, tiles_per_seq, pending)
        if release is not None:
            px = release(px)
        o_f, o_b = _retention(px, gc, dmat, xi, zeta, nb, seq, n_ctx)
        yw = _window(px, vt, sink_row, tri, nb, seq, n_ctx, not last)
        yg = _global(px, vt, nb, seq, n_ctx, not last)
        if not last:
            yw, yg = _ctx_attention(px, vt, sink_row, yw, yg, nb, seq, n_ctx)
        return _outproj(o_f, o_b, px, yw, yg, xa, mods[s][l], norm2[l][None, :], w_out, l,
                        wr_t, b_r, nb, n_x_tiles, tiles_per_seq, n_x_tiles if last else n_tiles)

    def tie(a, b):
        if n_streams == 1:
            return a, b
        return lax.optimization_barrier((a, b))

    mods, xas = [], []
    for s in range(n_streams):
        b0 = s * nb
        mods.append(jnp.concatenate([mod_all[:, b0:b0 + nb], mod_all[:, nb_all:nb_all + 1]], axis=1))
        xas.append(jnp.concatenate([x[b0:b0 + nb].reshape(n_x, d), ctx[b0:b0 + nb].reshape(nb * n_ctx, d)], 0))

    out = None
    pending = [None] * n_streams
    carry = None
    s_last = n_streams - 1

    def gather_back(yp, pos):
        return yp[pos[0]], yp[pos[1]]

    for l in range(depth):
        last = l == depth - 1
        n_tok = n_x if last else ta
        states, srcs, plans, xss = [], [], [], []
        for s in range(n_streams):
            due, release = None, None
            if s > 0:
                def release(px, s=s):
                    px, src_t = tie(px, srcs[s - 1])
                    xss.append(states[s - 1][1][src_t])
                    return px
            elif carry is not None:
                due, yp_c, pos_c, gate_c, mod_c = carry
                carry = None

                def release(px, due=due, yp_c=yp_c, pos_c=pos_c, gate_c=gate_c, mod_c=mod_c):
                    px, yp_t = tie(px, yp_c)
                    pending[due] = gather_back(yp_t, pos_c) + (gate_c, mod_c)
                    return px
            state = list(mix(l, s, xas[s], pending[s], release))
            if s > 0:
                state[2], xss[s - 1] = tie(state[2], xss[s - 1])
            elif due is not None:
                y0d, y1d, gate_d, mod_d = pending[due]
                state[2], (y0d, y1d) = tie(state[2], (y0d, y1d))
                pending[due] = (y0d, y1d, gate_d, mod_d)
            states.append(state)
            tables, src, pos = _moe_plan(state[2], n_tok)
            plans.append((tables, pos))
            srcs.append(src)
        yp_prev = None
        for s in range(n_streams):
            xa, h2, _, gate_t = states[s]
            tables, pos = plans[s]
            xs = xss[s] if s < s_last else h2[srcs[s]]
            if s < s_last:
                xs, srcs[s + 1] = tie(xs, srcs[s + 1])
            if s > 0:
                xs, yp_prev = tie(xs, yp_prev)
                ps, ppos, pgate, pmod, pxa = prev_info
                done = gather_back(yp_prev, ppos) + (pgate, pmod)
                if last:
                    out = _combine_final(pxa, *done, tiles_per_seq, n_x_tiles, out, ps * n_x_tiles, n_streams * n_x)
                else:
                    pending[ps] = done
            yp_prev = _moe_ffn(*tables, xs, w_gate_up, w_down, l)
            prev_info = (s, pos, gate_t.T, mods[s][l], xa)
            xas[s] = xa
        ps, ppos, pgate, pmod, pxa = prev_info
        if last:
            out = _combine_final(pxa, *gather_back(yp_prev, ppos), pgate, pmod, tiles_per_seq, n_x_tiles, out,
                                 ps * n_x_tiles, n_streams * n_x)
        elif n_streams == 1:
            pending[ps] = gather_back(yp_prev, ppos) + (pgate, pmod)
        else:
            carry = (ps, yp_prev, ppos, pgate, pmod)
    return out.reshape(nb_all, seq, d)
```

```python
import functools

import jax
import jax.numpy as jnp
from jax import lax
from jax.experimental import pallas as pl
from jax.experimental.pallas import tpu as pltpu

GRID_W = 64
HEAD_DIM = 64
RET_HEADS = 4
RET_DK = 64
RET_DV = 128
WIN_Q_HEADS = 4
WIN_KV_HEADS = 2
WINDOW = 128
GLB_Q_HEADS = 4
GLB_KV_HEADS = 2
ROPE_BASE = 10000.0
N_EXPERTS = 16
N_GROUPS = 4
EXPERTS_PER_GROUP = N_EXPERTS // N_GROUPS
TOP_K = 2
EPS = 1e-6
NEG_INF = -1e30

RQ, RK, RV, RG = 0, 256, 512, 1024
WQ, WK, WV = 1536, 1792, 1920
GQ, GK, GV = 2048, 2304, 2432
D_IN = 2560
D_RET = RET_HEADS * RET_DV
D_ATT = WIN_Q_HEADS * HEAD_DIM
N_KV = WIN_KV_HEADS * HEAD_DIM
P_RQ, P_RK, P_RV, P_RG = 0, 256, 512, 1024
P_WQ, P_GQ, P_WK, P_GK = 1536, 1792, 2048, 2176
PX_W = 2304

TM = 512
RET_C = 256
TQ_W = 1024
TQ_G = 256
KC_G = 256
LOG2_E = 1.4426950408889634
assert GLB_KV_HEADS == 2
TM_E = 512
N_STREAMS = 2
VMEM_LIMIT = 56 * 1024 * 1024

F32 = jnp.float32
BF16 = jnp.bfloat16


def _dot(a, b):
    return jnp.dot(a, b, preferred_element_type=F32)


def _dot_nt(a, b):
    return lax.dot_general(a, b, (((1,), (1,)), ((), ())), preferred_element_type=F32)


def _dot_tn(a, b):
    return lax.dot_general(a, b, (((0,), (0,)), ((), ())), preferred_element_type=F32)


def _silu(x):
    return x * jax.nn.sigmoid(x)


def _params(sem):
    return pltpu.CompilerParams(dimension_semantics=sem, vmem_limit_bytes=VMEM_LIMIT)


def _ada_kernel(c_ref, w_ref, b_ref, o_ref):
    s = _silu(c_ref[...]).astype(BF16)
    o_ref[0] = _dot(s, w_ref[0].astype(BF16)) + b_ref[0]


def _ada(cc, ada_w, ada_b):
    depth, d, n = ada_w.shape
    tn = 1536
    rows = cc.shape[0]
    return pl.pallas_call(
        _ada_kernel,
        grid=(depth, n // tn),
        in_specs=[pl.BlockSpec((rows, d), lambda l, j: (0, 0)),
                  pl.BlockSpec((1, d, tn), lambda l, j: (l, 0, j)),
                  pl.BlockSpec((1, 1, tn), lambda l, j: (l, 0, j))],
        out_specs=pl.BlockSpec((1, rows, tn), lambda l, j: (l, 0, j)),
        out_shape=jax.ShapeDtypeStruct((depth, rows, n), F32),
        compiler_params=_params(("arbitrary", "arbitrary")),
        name="ada_mod",
    )(cc, ada_w, ada_b.reshape(depth, 1, n))


def _inproj_fused_kernel(x_ref, y0_ref, y1_ref, gate_ref, modp_ref, mod_ref, n1_ref, wf_ref, wvt_ref, cos_ref,
                         sin_ref, gains_ref, hm_ref, xo_ref, o_ref, vt_ref, w_ref):
    g = gate_ref[...]
    y = y0_ref[...].astype(F32) * g[:, 0:1] + y1_ref[...].astype(F32) * g[:, 1:2]
    x = x_ref[...] + modp_ref[0, 5:6, :] * y
    xo_ref[...] = x
    _inproj_body(x, mod_ref, n1_ref, wf_ref, wvt_ref, cos_ref, sin_ref, gains_ref, hm_ref, o_ref, vt_ref, w_ref)


def _inproj_kernel(x_ref, mod_ref, n1_ref, wf_ref, wvt_ref, cos_ref, sin_ref, gains_ref, hm_ref, o_ref, vt_ref,
                   w_ref):
    _inproj_body(x_ref[...], mod_ref, n1_ref, wf_ref, wvt_ref, cos_ref, sin_ref, gains_ref, hm_ref, o_ref, vt_ref,
                 w_ref)


def _inproj_body(x, mod_ref, n1_ref, wf_ref, wvt_ref, cos_ref, sin_ref, gains_ref, hm_ref, o_ref, vt_ref, w_ref):
    @pl.when(pl.program_id(0) == 0)
    def _():
        w_ref[...] = wf_ref[...].astype(BF16)

    ms = jnp.mean(x * x, axis=-1, keepdims=True)
    h = x * lax.rsqrt(ms + EPS) * n1_ref[...]
    h = h * (1.0 + mod_ref[0, 1:2, :]) + mod_ref[0, 0:1, :]
    hb = h.astype(BF16)

    def proj(lo, width):
        return _dot(hb, w_ref[:, lo:lo + width])

    def head_msq(y):
        sq = y * y
        sq_hi = sq.astype(BF16)
        sq_lo = (sq - sq_hi.astype(F32)).astype(BF16)
        hm = hm_ref[0:y.shape[1], 0:y.shape[1]]
        return _dot(sq_hi, hm) + _dot(sq_lo, hm)

    def qk_finish(y, msq, gain_row, scale):
        width = y.shape[1]
        yn = y * lax.rsqrt(msq + EPS) * gains_ref[gain_row:gain_row + 1, 0:width]
        nxt = pltpu.roll(yn, width - HEAD_DIM // 4, 1)
        prv = pltpu.roll(yn, HEAD_DIM // 4, 1)
        lane = lax.broadcasted_iota(jnp.int32, yn.shape, 1)
        partner = jnp.where((lane % (HEAD_DIM // 2)) < HEAD_DIM // 4, nxt, prv)
        yr = yn * cos_ref[:, 0:width] + partner * sin_ref[:, 0:width]
        return (yr * scale).astype(BF16)

    q_scale = HEAD_DIM ** -0.5 * LOG2_E
    qk_segs = ((WQ, D_ATT, 0, q_scale, P_WQ), (GQ, D_ATT, 2, q_scale, P_GQ),
               (WK, N_KV, 1, 1.0, P_WK), (GK, N_KV, 3, 1.0, P_GK))
    plain_segs = ((RQ, RK - RQ, 1.0, P_RQ), (RK, RV - RK, RET_DK ** -0.5, P_RK),
                  (RV, RG - RV, 1.0, P_RV), (RG, WQ - RG, 1.0, P_RG))
    ys = [proj(lo, width) for lo, width, _, _, _ in qk_segs]
    stats = [head_msq(y) for y in ys]
    vt_ref[...] = _dot_nt(wvt_ref[...], hb).astype(BF16)
    for (lo, width, scale, dst), (_, qwidth, gain_row, qscale, qdst), y, msq in zip(plain_segs, qk_segs, ys, stats):
        p = proj(lo, width)
        o_ref[:, dst:dst + width] = (p if scale == 1.0 else p * scale).astype(BF16)
        o_ref[:, qdst:qdst + qwidth] = qk_finish(y, msq, gain_row, qscale)


def _inproj(xa, mod_l, n1, w_in, layer, cos_t, sin_t, gains, hm, nb, n_x_tiles, tiles_per_seq, pending=None):
    ta, d = xa.shape
    n_tiles = ta // TM
    w_vt = jnp.concatenate([w_in[layer, :, WV:WV + N_KV], w_in[layer, :, GV:GV + N_KV]], axis=1).T.astype(BF16)
    n_gv = 2 * N_KV
    once = pl.Buffered(1)

    def mod_idx(i):
        return (jnp.where(i < n_x_tiles, i // tiles_per_seq, nb), 0, 0)

    def rope_idx(i):
        return (jnp.where(i < n_x_tiles, i % tiles_per_seq, tiles_per_seq), 0)

    row = lambda i: (i, 0)
    in_specs = [pl.BlockSpec((1, 6, d), mod_idx),
                pl.BlockSpec((1, d), lambda i: (0, 0)),
                pl.BlockSpec((None, d, D_IN), lambda i: (layer, 0, 0), pipeline_mode=once),
                pl.BlockSpec((n_gv, d), lambda i: (0, 0), pipeline_mode=once),
                pl.BlockSpec((TM, D_ATT), rope_idx),
                pl.BlockSpec((TM, D_ATT), rope_idx),
                pl.BlockSpec((8, D_ATT), lambda i: (0, 0)),
                pl.BlockSpec((D_ATT, D_ATT), lambda i: (0, 0))]
    args = [mod_l, n1, w_in, w_vt, cos_t, sin_t, gains, hm]
    out_specs = [pl.BlockSpec((TM, PX_W), row), pl.BlockSpec((n_gv, TM), lambda i: (0, i))]
    out_shape = [jax.ShapeDtypeStruct((ta, PX_W), BF16), jax.ShapeDtypeStruct((n_gv, ta), BF16)]
    if pending is None:
        kern, aliases = _inproj_kernel, {}
        in_specs = [pl.BlockSpec((TM, d), row)] + in_specs
        args = [xa] + args
    else:
        y0, y1, gate, mod_prev = pending
        kern, aliases = _inproj_fused_kernel, {0: 0}
        in_specs = [pl.BlockSpec((TM, d), row), pl.BlockSpec((TM, d), row), pl.BlockSpec((TM, d), row),
                    pl.BlockSpec((TM, TOP_K), row), pl.BlockSpec((1, 6, d), mod_idx)] + in_specs
        args = [xa, y0, y1, gate, mod_prev] + args
        out_specs = [pl.BlockSpec((TM, d), row)] + out_specs
        out_shape = [jax.ShapeDtypeStruct((ta, d), F32)] + out_shape
    res = pl.pallas_call(
        kern,
        grid=(n_tiles,),
        in_specs=in_specs,
        out_specs=out_specs,
        out_shape=out_shape,
        scratch_shapes=[pltpu.VMEM((d, D_IN), BF16)],
        input_output_aliases=aliases,
        compiler_params=_params(("arbitrary",)),
        name="in_proj",
    )(*args)
    return (xa,) + tuple(res) if pending is None else tuple(res)


def _ret_kernel(gc_ref, qf_ref, kf_ref, vf_ref, qb_ref, kb_ref, vb_ref, dmat_ref, xi_ref, zeta_ref,
                of_ref, ob_ref, s_ref):
    @pl.when(pl.program_id(1) == 0)
    def _():
        s_ref[...] = jnp.zeros_like(s_ref)

    dirs = ((qf_ref, kf_ref, vf_ref, of_ref), (qb_ref, kb_ref, vb_ref, ob_ref))
    work = []
    for d, (q_ref, k_ref, v_ref, o_ref) in enumerate(dirs):
        q = q_ref[...]
        k = k_ref[...]
        v = v_ref[...]
        kz = (k.astype(F32) * zeta_ref[d]).astype(BF16)
        for h in range(RET_HEADS):
            i = d * RET_HEADS + h
            qh = q[:, h * RET_DK:(h + 1) * RET_DK]
            kh = k[:, h * RET_DK:(h + 1) * RET_DK]
            vh = v[:, h * RET_DV:(h + 1) * RET_DV]
            state = s_ref[i]
            att = _dot_nt(qh, kh)
            carried = _dot(qh, state.astype(BF16))
            update = _dot_tn(kz[:, h * RET_DK:(h + 1) * RET_DK], vh)
            work.append((i, vh, state, att, carried, update))
    outs = []
    for i, vh, state, att, carried, update in work:
        outs.append(_dot((att * dmat_ref[i]).astype(BF16), vh) + carried * xi_ref[i])
        s_ref[i] = gc_ref[i] * state + update
    for d, (_, _, _, o_ref) in enumerate(dirs):
        o_ref[...] = jnp.concatenate(outs[d * RET_HEADS:(d + 1) * RET_HEADS], axis=-1).astype(BF16)


def _retention(px, gc, dmat, xi, zeta, nb, seq, n_ctx):
    ta = px.shape[0]
    n_x = seq // RET_C
    n_c = n_ctx // RET_C
    ctx_base = nb * n_x
    steps = n_c + n_x

    def row_f(b, c):
        return jnp.where(c < n_c, ctx_base + b * n_c + c, b * n_x + (c - n_c))

    def row_b(b, c):
        return jnp.where(c < n_c, ctx_base + b * n_c + (n_c - 1 - c), b * n_x + (steps - 1 - c))

    def spec(width, col, row):
        return pl.BlockSpec((RET_C, width), lambda b, c: (row(b, c), col))

    const3 = lambda b, c: (0, 0, 0)
    return pl.pallas_call(
        _ret_kernel,
        grid=(nb, steps),
        in_specs=[pl.BlockSpec(memory_space=pltpu.SMEM),
                  spec(256, P_RQ // 256, row_f), spec(256, P_RK // 256, row_f), spec(D_RET, P_RV // D_RET, row_f),
                  spec(256, P_RQ // 256, row_b), spec(256, P_RK // 256, row_b), spec(D_RET, P_RV // D_RET, row_b),
                  pl.BlockSpec((2 * RET_HEADS, RET_C, RET_C), const3),
                  pl.BlockSpec((2 * RET_HEADS, RET_C, RET_DV), const3),
                  pl.BlockSpec((2, RET_C, RET_HEADS * RET_DK), const3)],
        out_specs=[spec(D_RET, 0, row_f), spec(D_RET, 0, row_b)],
        out_shape=[jax.ShapeDtypeStruct((ta, D_RET), BF16)] * 2,
        scratch_shapes=[pltpu.VMEM((2 * RET_HEADS, RET_DK, RET_DV), F32)],
        compiler_params=_params(("parallel", "arbitrary")),
        name="retention",
    )(gc, px, px, px, px, px, px, dmat, xi, zeta)


def _ret_tables(decay_logit):
    log_g = jax.nn.log_sigmoid(decay_logit.astype(F32)).reshape(2 * RET_HEADS)
    idx = jnp.arange(RET_C, dtype=F32)
    diff = idx[:, None] - idx[None, :]
    lg = log_g[:, None, None]
    d_fwd = jnp.where(diff >= 0, jnp.exp(jnp.maximum(diff, 0.0) * lg), 0.0)
    d_bwd = jnp.where(diff <= 0, jnp.exp(jnp.maximum(-diff, 0.0) * lg), 0.0)
    is_bwd = (jnp.arange(2 * RET_HEADS) >= RET_HEADS)[:, None, None]
    dmat = jnp.where(is_bwd, d_bwd, d_fwd)
    pos = jnp.where(is_bwd[:, :, 0], RET_C - 1.0 - idx[None, :], idx[None, :])
    xi = jnp.exp((pos + 1.0) * log_g[:, None])
    zeta = jnp.exp((RET_C - 1.0 - pos) * log_g[:, None])
    gc = jnp.exp(RET_C * log_g)
    xi = jnp.broadcast_to(xi[:, :, None], (2 * RET_HEADS, RET_C, RET_DV))
    zeta = jnp.repeat(zeta.reshape(2, RET_HEADS, RET_C).transpose(0, 2, 1), RET_DK, axis=-1)
    return gc, dmat, xi, zeta


def _pad_heads(q, n_kv):
    rows, width = q.shape
    group = width // HEAD_DIM // n_kv
    zeros = jnp.zeros((rows, HEAD_DIM), BF16)
    out = []
    for h in range(n_kv):
        for g in range(group):
            qh = q[:, (h * group + g) * HEAD_DIM:(h * group + g + 1) * HEAD_DIM]
            out.append(jnp.concatenate([qh if hh == h else zeros for hh in range(n_kv)], axis=1))
    return jnp.concatenate(out, axis=0)


def _fill_values(v_ext, lo, vt):
    n = vt.shape[1]
    for h in range(v_ext.shape[0]):
        v_ext[h, 0:HEAD_DIM, lo:lo + n] = vt[h * HEAD_DIM:(h + 1) * HEAD_DIM, :]


def _store_heads(o_ref, col, h, group, oe, den):
    n = oe.shape[1] // group
    o = oe[0:HEAD_DIM] / den
    for g in range(group):
        r0 = (h * group + g) * HEAD_DIM
        o_ref[r0:r0 + HEAD_DIM, col:col + n] = o[:, g * n:(g + 1) * n].astype(BF16)


def _win_kernel(q_ref, k_ref, kc_ref, vt_ref, vtc_ref, sink_ref, tri_ref, o_ref, k_pad, v_ext, *, seq, n_ctx):
    i = pl.program_id(1)
    group = WIN_Q_HEADS // WIN_KV_HEADS
    sub = WINDOW
    w = group * sub
    span = 3 * WINDOW
    c0 = seq + 2 * WINDOW
    n_sub = q_ref.shape[0] // sub

    @pl.when(i == 0)
    def _():
        zk = jnp.zeros((WINDOW, N_KV), BF16)
        k_pad[0:WINDOW, :] = zk
        k_pad[WINDOW:WINDOW + seq, :] = k_ref[...]
        k_pad[WINDOW + seq:c0, :] = zk
        k_pad[c0:c0 + n_ctx, :] = kc_ref[...]
        v_ext[:, 0:HEAD_DIM, 0:WINDOW] = jnp.zeros((WIN_KV_HEADS, HEAD_DIM, WINDOW), BF16)
        v_ext[:, 0:HEAD_DIM, WINDOW + seq:c0] = jnp.zeros((WIN_KV_HEADS, HEAD_DIM, WINDOW), BF16)
        _fill_values(v_ext, WINDOW, vt_ref[...])
        _fill_values(v_ext, c0, vtc_ref[...])
        v_ext[:, HEAD_DIM:2 * HEAD_DIM, :] = jnp.ones((WIN_KV_HEADS, HEAD_DIM, c0 + n_ctx), BF16)

    sink = sink_ref[...]
    kc = k_pad[c0:c0 + n_ctx, :]
    start = i * q_ref.shape[0]

    def scores(a):
        qs = pl.multiple_of(start + a * sub, sub)
        q4 = _pad_heads(q_ref[a * sub:(a + 1) * sub, :], WIN_KV_HEADS)
        return a, qs, _dot_nt(k_pad[pl.ds(qs, span), :], q4), _dot_nt(kc, q4)

    def softmax(a, qs, st, sc):
        lo_edge = jnp.where(qs == 0, NEG_INF, 0.0)
        hi_edge = jnp.where(qs + sub == seq, NEG_INF, 0.0)
        s0 = st[0:WINDOW] + (tri_ref[0] + lo_edge)
        s1 = st[WINDOW:2 * WINDOW]
        s2 = st[2 * WINDOW:span] + (tri_ref[1] + hi_edge)
        m = jnp.maximum(jnp.maximum(jnp.max(s0, axis=0, keepdims=True), jnp.max(s1, axis=0, keepdims=True)),
                        jnp.maximum(jnp.max(s2, axis=0, keepdims=True), jnp.max(sc, axis=0, keepdims=True)))
        m = jnp.maximum(m, sink)
        pw = jnp.concatenate([jnp.exp2(s0 - m), jnp.exp2(s1 - m), jnp.exp2(s2 - m)], axis=0).astype(BF16)
        return a, qs, m, pw, jnp.exp2(sc - m).astype(BF16)

    def values(a, qs, m, pw, pc):
        for h in range(WIN_KV_HEADS):
            cs = slice(h * w, (h + 1) * w)
            oe = _dot(v_ext[h, :, pl.ds(qs, span)], pw[:, cs]) + _dot(v_ext[h, :, c0:c0 + n_ctx], pc[:, cs])
            den = oe[HEAD_DIM:HEAD_DIM + 1] + jnp.exp2(sink[:, cs] - m[:, cs])
            _store_heads(o_ref, a * sub, h, group, oe, den)

    nxt = scores(0)
    pend = None
    for a in range(n_sub):
        cur = nxt
        if a + 1 < n_sub:
            nxt = scores(a + 1)
        if pend is not None:
            values(*pend)
        pend = softmax(*cur)
    values(*pend)


def _window(px, vt, sink_row, tri, nb, seq, n_ctx, with_ctx_cols):
    ta = px.shape[0] if with_ctx_cols else nb * seq
    assert seq % TQ_W == 0 and TQ_W % WINDOW == 0
    n_q = seq // TQ_W
    ctx_rows = (nb * seq) // n_ctx
    n_keys = seq + 2 * WINDOW + n_ctx
    return pl.pallas_call(
        functools.partial(_win_kernel, seq=seq, n_ctx=n_ctx),
        grid=(nb, n_q),
        in_specs=[pl.BlockSpec((TQ_W, D_ATT), lambda b, i: (b * n_q + i, P_WQ // D_ATT)),
                  pl.BlockSpec((seq, N_KV), lambda b, i: (b, P_WK // N_KV)),
                  pl.BlockSpec((n_ctx, N_KV), lambda b, i: (ctx_rows + b, P_WK // N_KV)),
                  pl.BlockSpec((N_KV, seq), lambda b, i: (0, b)),
                  pl.BlockSpec((N_KV, n_ctx), lambda b, i: (0, ctx_rows + b)),
                  pl.BlockSpec((1, WIN_Q_HEADS * WINDOW), lambda b, i: (0, 0)),
                  pl.BlockSpec((2, WINDOW, WIN_Q_HEADS * WINDOW), lambda b, i: (0, 0, 0))],
        out_specs=pl.BlockSpec((D_ATT, TQ_W), lambda b, i: (0, b * n_q + i)),
        out_shape=jax.ShapeDtypeStruct((D_ATT, ta), BF16),
        scratch_shapes=[pltpu.VMEM((n_keys, N_KV), BF16),
                        pltpu.VMEM((WIN_KV_HEADS, 2 * HEAD_DIM, n_keys), BF16)],
        compiler_params=_params(("parallel", "arbitrary")),
        name="window_attn",
    )(px, px, px, vt, vt, sink_row, tri)


def _ctx_kernel(qw_ref, kw_ref, qg_ref, kg_ref, vt_ref, sink_ref, yw_in, yg_in, yw_ref, yg_ref):
    del yw_in, yg_in
    n_ctx = qw_ref.shape[0]
    sub = WINDOW
    ones = jnp.ones((HEAD_DIM, n_ctx), BF16)
    for q_ref, k_ref, row0, o_ref, sink in ((qw_ref, kw_ref, 0, yw_ref, sink_ref[...]),
                                            (qg_ref, kg_ref, N_KV, yg_ref, None)):
        n_kv = k_ref.shape[1] // HEAD_DIM
        group = q_ref.shape[1] // HEAD_DIM // n_kv
        w = group * sub
        k = k_ref[...]
        for a in range(n_ctx // sub):
            sc = _dot_nt(k, _pad_heads(q_ref[a * sub:(a + 1) * sub, :], n_kv))
            m = jnp.max(sc, axis=0, keepdims=True)
            if sink is not None:
                m = jnp.maximum(m, sink)
            p = jnp.exp2(sc - m).astype(BF16)
            for h in range(n_kv):
                cs = slice(h * w, (h + 1) * w)
                ve = jnp.concatenate([vt_ref[row0 + h * HEAD_DIM:row0 + (h + 1) * HEAD_DIM, :], ones], axis=0)
                oe = _dot(ve, p[:, cs])
                den = oe[HEAD_DIM:HEAD_DIM + 1]
                if sink is not None:
                    den = den + jnp.exp2(sink[:, cs] - m[:, cs])
                _store_heads(o_ref, a * sub, h, group, oe, den)


def _ctx_attention(px, vt, sink_row, ywt, ygt, nb, seq, n_ctx):
    ctx_rows = (nb * seq) // n_ctx
    row = lambda col: (lambda b: (ctx_rows + b, col))
    any_spec = pl.BlockSpec(memory_space=pl.ANY)
    out_spec = pl.BlockSpec((D_ATT, n_ctx), lambda b: (0, ctx_rows + b))
    return pl.pallas_call(
        _ctx_kernel,
        grid=(nb,),
        in_specs=[pl.BlockSpec((n_ctx, D_ATT), row(P_WQ // D_ATT)),
                  pl.BlockSpec((n_ctx, N_KV), row(P_WK // N_KV)),
                  pl.BlockSpec((n_ctx, D_ATT), row(P_GQ // D_ATT)),
                  pl.BlockSpec((n_ctx, N_KV), row(P_GK // N_KV)),
                  pl.BlockSpec((2 * N_KV, n_ctx), lambda b: (0, ctx_rows + b)),
                  pl.BlockSpec((1, WIN_Q_HEADS * WINDOW), lambda b: (0, 0)),
                  any_spec, any_spec],
        out_specs=[out_spec, out_spec],
        out_shape=[jax.ShapeDtypeStruct(ywt.shape, BF16), jax.ShapeDtypeStruct(ygt.shape, BF16)],
        input_output_aliases={6: 0, 7: 1},
        compiler_params=_params(("parallel",)),
        name="ctx_attn",
    )(px, px, px, px, vt, sink_row, ywt, ygt)


def _glb_kernel(q_ref, k_ref, kc_ref, vt_ref, vtc_ref, o_ref, k_all, v_ext, *, seq, n_ctx):
    i = pl.program_id(1)
    group = GLB_Q_HEADS // GLB_KV_HEADS
    tq = q_ref.shape[0]
    w = group * tq

    @pl.when(i == 0)
    def _():
        k_all[0:seq, :] = k_ref[...]
        k_all[seq:seq + n_ctx, :] = kc_ref[...]
        _fill_values(v_ext, 0, vt_ref[...])
        _fill_values(v_ext, seq, vtc_ref[...])
        v_ext[:, HEAD_DIM:2 * HEAD_DIM, :] = jnp.ones((GLB_KV_HEADS, HEAD_DIM, seq + n_ctx), BF16)

    def attend(chunks):
        q4 = _pad_heads(q_ref[...], GLB_KV_HEADS)

        def scores(chunk):
            lo, nk = chunk
            return _dot_nt(k_all[lo:lo + nk, :], q4)

        m = None
        acc = [None] * GLB_KV_HEADS

        def accumulate(pend):
            pt, alpha, (lo, nk) = pend
            for h in range(GLB_KV_HEADS):
                pv = _dot(v_ext[h, :, lo:lo + nk], pt[:, h * w:(h + 1) * w])
                acc[h] = pv if alpha is None else acc[h] * alpha[:, h * w:(h + 1) * w] + pv

        st_next = scores(chunks[0])
        pend = None
        for ci, chunk in enumerate(chunks):
            st = st_next
            if ci + 1 < len(chunks):
                st_next = scores(chunks[ci + 1])
            if pend is not None:
                accumulate(pend)
            cm = jnp.max(st, axis=0, keepdims=True)
            m_new = cm if m is None else jnp.maximum(m, cm)
            pt = jnp.exp2(st - m_new).astype(BF16)
            alpha = None if m is None else jnp.exp2(m - m_new)
            pend = (pt, alpha, chunk)
            m = m_new
        accumulate(pend)
        for h in range(GLB_KV_HEADS):
            _store_heads(o_ref, 0, h, group, acc[h], acc[h][HEAD_DIM:HEAD_DIM + 1])

    attend([(c * KC_G, KC_G) for c in range(seq // KC_G)] + [(seq, n_ctx)])


def _global(px, vt, nb, seq, n_ctx, with_ctx_cols):
    ta = px.shape[0] if with_ctx_cols else nb * seq
    assert seq % KC_G == 0 and seq % TQ_G == 0
    n_q = seq // TQ_G
    ctx_rows = (nb * seq) // n_ctx
    return pl.pallas_call(
        functools.partial(_glb_kernel, seq=seq, n_ctx=n_ctx),
        grid=(nb, n_q),
        in_specs=[pl.BlockSpec((TQ_G, D_ATT), lambda b, i: (b * n_q + i, P_GQ // D_ATT)),
                  pl.BlockSpec((seq, N_KV), lambda b, i: (b, P_GK // N_KV)),
                  pl.BlockSpec((n_ctx, N_KV), lambda b, i: (ctx_rows + b, P_GK // N_KV)),
                  pl.BlockSpec((N_KV, seq), lambda b, i: (1, b)),
                  pl.BlockSpec((N_KV, n_ctx), lambda b, i: (1, ctx_rows + b))],
        out_specs=pl.BlockSpec((D_ATT, TQ_G), lambda b, i: (0, b * n_q + i)),
        out_shape=jax.ShapeDtypeStruct((D_ATT, ta), BF16),
        scratch_shapes=[pltpu.VMEM((seq + n_ctx, N_KV), BF16),
                        pltpu.VMEM((GLB_KV_HEADS, 2 * HEAD_DIM, seq + n_ctx), BF16)],
        compiler_params=_params(("parallel", "arbitrary")),
        name="global_attn",
    )(px, px, px, vt, vt)


def _route_rows(logits, bias):
    sig = jax.nn.sigmoid(logits)
    biased = sig + bias
    b_rows = [biased[e:e + 1, :] for e in range(N_EXPERTS)]
    s_rows = [sig[e:e + 1, :] for e in range(N_EXPERTS)]
    n_loc = EXPERTS_PER_GROUP

    best_score, grp = None, None
    for g in range(N_GROUPS):
        a = b_rows[g * n_loc:(g + 1) * n_loc]
        top2 = None
        for i in range(n_loc):
            for j in range(i + 1, n_loc):
                pair = a[i] + a[j]
                top2 = pair if top2 is None else jnp.maximum(top2, pair)
        if g == 0:
            best_score, grp = top2, jnp.zeros(top2.shape, jnp.int32)
        else:
            upd = top2 > best_score
            grp = jnp.where(upd, g, grp)
            best_score = jnp.where(upd, top2, best_score)

    def pick(rows, i):
        out = rows[i]
        for g in range(1, N_GROUPS):
            out = jnp.where(grp == g, rows[g * n_loc + i], out)
        return out

    cand = [pick(b_rows, i) for i in range(n_loc)]
    cand_s = [pick(s_rows, i) for i in range(n_loc)]
    m1, l1, w1 = cand[0], jnp.zeros(grp.shape, jnp.int32), cand_s[0]
    for i in range(1, n_loc):
        upd = cand[i] > m1
        m1 = jnp.where(upd, cand[i], m1)
        l1 = jnp.where(upd, i, l1)
        w1 = jnp.where(upd, cand_s[i], w1)
    m2, l2, w2 = None, None, None
    for i in range(n_loc):
        rest = jnp.where(l1 == i, -jnp.inf, cand[i])
        if i == 0:
            m2, l2, w2 = rest, jnp.zeros(grp.shape, jnp.int32), cand_s[0]
        else:
            upd = rest > m2
            m2 = jnp.where(upd, rest, m2)
            l2 = jnp.where(upd, i, l2)
            w2 = jnp.where(upd, cand_s[i], w2)
    tot = w1 + w2
    e_idx = jnp.concatenate([grp * n_loc + l1, grp * n_loc + l2], axis=0)
    gate = jnp.concatenate([w1 / tot, w2 / tot], axis=0)
    return e_idx, gate


def _outproj_kernel(of_ref, ob_ref, g_ref, ywt_ref, ygt_ref, x_ref, mod_ref, n2_ref, wf_ref, wr_ref, br_ref,
                    xo_ref, h_ref, e_ref, gate_ref, w_ref):
    @pl.when(pl.program_id(0) == 0)
    def _():
        w_ref[...] = wf_ref[...].astype(BF16)

    def ret_out(r):
        o = of_ref[r, :].astype(F32) + ob_ref[r, :].astype(F32)
        normed = []
        for h in range(RET_HEADS):
            oh = o[:, h * RET_DV:(h + 1) * RET_DV]
            mu = jnp.mean(oh, axis=-1, keepdims=True)
            var = jnp.mean(jnp.square(oh - mu), axis=-1, keepdims=True)
            normed.append((oh - mu) * lax.rsqrt(var + EPS))
        return (_silu(g_ref[r, :].astype(F32)) * jnp.concatenate(normed, axis=-1)).astype(BF16)

    def project(r, yr):
        acc = _dot_tn(ywt_ref[:, r], w_ref[D_RET:D_RET + D_ATT, :])
        acc += _dot_tn(ygt_ref[:, r], w_ref[D_RET + D_ATT:D_RET + 2 * D_ATT, :])
        return acc + _dot(yr, w_ref[0:D_RET, :])

    def residual_norm(r, acc):
        x = x_ref[r, :] + mod_ref[0, 2:3, :] * acc
        xo_ref[r, :] = x
        ms = jnp.mean(x * x, axis=-1, keepdims=True)
        h2 = x * lax.rsqrt(ms + EPS) * n2_ref[...]
        h2 = (h2 * (1.0 + mod_ref[0, 4:5, :]) + mod_ref[0, 3:4, :]).astype(BF16)
        h_ref[r, :] = h2
        return h2

    def route(r, h2):
        e_idx, gate = _route_rows(_dot_nt(wr_ref[...], h2), br_ref[...])
        e_ref[:, r] = e_idx
        gate_ref[:, r] = gate

    half = x_ref.shape[0] // 2
    ra, rb = slice(0, half), slice(half, 2 * half)
    yr_a = ret_out(ra)
    acc_a = project(ra, yr_a)
    yr_b = ret_out(rb)
    h2_a = residual_norm(ra, acc_a)
    acc_b = project(rb, yr_b)
    route(ra, h2_a)
    h2_b = residual_norm(rb, acc_b)
    route(rb, h2_b)


def _outproj(o_f, o_b, px, yw, yg, xa, mod_l, n2, w_out, layer, wr_t, b_r, nb, n_x_tiles, tiles_per_seq, n_tiles):
    ta, d = xa.shape
    once = pl.Buffered(1)

    def mod_idx(i):
        return (jnp.where(i < n_x_tiles, i // tiles_per_seq, nb), 0, 0)

    row = lambda i: (i, 0)
    return pl.pallas_call(
        _outproj_kernel,
        grid=(n_tiles,),
        in_specs=[pl.BlockSpec((TM, D_RET), row),
                  pl.BlockSpec((TM, D_RET), row),
                  pl.BlockSpec((TM, D_RET), lambda i: (i, P_RG // D_RET)),
                  pl.BlockSpec((D_ATT, TM), lambda i: (0, i)),
                  pl.BlockSpec((D_ATT, TM), lambda i: (0, i)),
                  pl.BlockSpec((TM, d), row),
                  pl.BlockSpec((1, 6, d), mod_idx),
                  pl.BlockSpec((1, d), lambda i: (0, 0)),
                  pl.BlockSpec((None, d, d), lambda i: (layer, 0, 0), pipeline_mode=once),
                  pl.BlockSpec((N_EXPERTS, d), lambda i: (0, 0)),
                  pl.BlockSpec((N_EXPERTS, 1), lambda i: (0, 0))],
        out_specs=[pl.BlockSpec((TM, d), row),
                   pl.BlockSpec((TM, d), row),
                   pl.BlockSpec((TOP_K, TM), lambda i: (0, i)),
                   pl.BlockSpec((TOP_K, TM), lambda i: (0, i))],
        out_shape=[jax.ShapeDtypeStruct((ta, d), F32),
                   jax.ShapeDtypeStruct((n_tiles * TM, d), BF16),
                   jax.ShapeDtypeStruct((TOP_K, n_tiles * TM), jnp.int32),
                   jax.ShapeDtypeStruct((TOP_K, n_tiles * TM), F32)],
        scratch_shapes=[pltpu.VMEM((d, d), BF16)],
        input_output_aliases={5: 0},
        compiler_params=_params(("arbitrary",)),
        name="out_proj",
    )(o_f, o_b, px, yw, yg, xa, mod_l, n2, w_out, wr_t, b_r)


def _moe_kernel(be_ref, first_ref, nu_ref, x_ref, wgu_f_ref, wd_f_ref, o_ref, wgu_ref, wd_ref):
    i = pl.program_id(0)

    @pl.when(jnp.logical_and(i < nu_ref[0], first_ref[i] == 1))
    def _():
        wgu_ref[...] = wgu_f_ref[...].astype(BF16)
        wd_ref[...] = wd_f_ref[...].astype(BF16)

    @pl.when(i < nu_ref[0])
    def _():
        f = wd_ref.shape[0]
        x = x_ref[...]
        acc = None
        for lo in range(0, f, f // 2):
            hi = lo + f // 2
            mid = (_silu(_dot(x, wgu_ref[:, lo:hi])) * _dot(x, wgu_ref[:, f + lo:f + hi])).astype(BF16)
            part = _dot(mid, wd_ref[lo:hi, :])
            acc = part if acc is None else acc + part
        o_ref[...] = acc.astype(BF16)

    @pl.when(i >= nu_ref[0])
    def _():
        o_ref[...] = jnp.zeros_like(o_ref)


def _moe_ffn(blk_e, first, n_used, xs, w_gu, w_down, layer):
    rows, d = xs.shape
    f2 = w_gu.shape[3]
    f = w_down.shape[2]
    grid_spec = pltpu.PrefetchScalarGridSpec(
        num_scalar_prefetch=3,
        grid=(rows // TM_E,),
        in_specs=[pl.BlockSpec((TM_E, d), lambda i, be, fi, nu: (i, 0)),
                  pl.BlockSpec((None, None, d, f2), lambda i, be, fi, nu: (layer, be[i], 0, 0)),
                  pl.BlockSpec((None, None, f, d), lambda i, be, fi, nu: (layer, be[i], 0, 0))],
        out_specs=pl.BlockSpec((TM_E, d), lambda i, be, fi, nu: (i, 0)),
        scratch_shapes=[pltpu.VMEM((d, f2), BF16), pltpu.VMEM((f, d), BF16)],
    )
    return pl.pallas_call(
        _moe_kernel,
        grid_spec=grid_spec,
        out_shape=jax.ShapeDtypeStruct((rows, d), BF16),
        compiler_params=_params(("arbitrary",)),
        name="moe_ffn",
    )(blk_e, first, n_used, xs, w_gu, w_down)


def _moe_plan(e_idx_t, n_tok):
    n_asg = n_tok * TOP_K
    flat_e = e_idx_t[:, :n_tok].reshape(-1)
    onehot = (flat_e[:, None] == jnp.arange(N_EXPERTS, dtype=jnp.int32)[None, :]).astype(jnp.int32)
    csum = jnp.cumsum(onehot, axis=0)
    counts = csum[-1]
    rank = jnp.sum(csum * onehot, axis=1) - 1
    padded = (counts + TM_E - 1) // TM_E * TM_E
    pad_end = jnp.cumsum(padded)
    pad_start = pad_end - padded
    cnt_start = jnp.cumsum(counts) - counts
    pos = (jnp.sum(pad_start[None, :] * onehot, axis=1) + rank).reshape(TOP_K, n_tok)
    n_blocks = (n_asg + N_EXPERTS * (TM_E - 1) + TM_E - 1) // TM_E
    blk_start = jnp.arange(n_blocks, dtype=jnp.int32) * TM_E
    blk_e = jnp.minimum(jnp.sum(blk_start[:, None] >= pad_end[None, :], axis=1), N_EXPERTS - 1).astype(jnp.int32)
    first = jnp.concatenate([jnp.ones((1,), jnp.int32), (blk_e[1:] != blk_e[:-1]).astype(jnp.int32)])
    n_used = (pad_end[-1] // TM_E).astype(jnp.int32).reshape(1)
    bits = max(1, (n_asg - 1).bit_length())
    order = jnp.sort((flat_e << bits) | jnp.arange(n_asg, dtype=jnp.int32)) & ((1 << bits) - 1)
    e_row = jnp.repeat(blk_e, TM_E)
    p = jnp.arange(n_blocks * TM_E, dtype=jnp.int32)
    r = p - pad_start[e_row]
    src = jnp.where(r < counts[e_row], order[jnp.clip(cnt_start[e_row] + r, 0, n_asg - 1)], p) % n_tok
    return (blk_e, first, n_used), src, pos


def _combine_kernel(x_ref, y0_ref, y1_ref, gate_ref, mod_ref, o_ref):
    g = gate_ref[...]
    y = y0_ref[...].astype(F32) * g[:, 0:1] + y1_ref[...].astype(F32) * g[:, 1:2]
    o_ref[...] = x_ref[...] + mod_ref[0, 5:6, :] * y


def _combine_final_kernel(x_ref, y0_ref, y1_ref, gate_ref, mod_ref, prev_ref, o_ref):
    del prev_ref
    _combine_kernel(x_ref, y0_ref, y1_ref, gate_ref, mod_ref, o_ref)


def _combine_final(xa, y0, y1, gate, mod_l, tiles_per_seq, n_tiles, prev, tile_off, total_rows):
    d = xa.shape[1]
    row = lambda i: (i, 0)
    ins = [pl.BlockSpec((TM, d), row), pl.BlockSpec((TM, d), row), pl.BlockSpec((TM, d), row),
           pl.BlockSpec((TM, TOP_K), row), pl.BlockSpec((1, 6, d), lambda i: (i // tiles_per_seq, 0, 0))]
    args = [xa, y0, y1, gate, mod_l]
    if prev is not None:
        ins.append(pl.BlockSpec(memory_space=pl.ANY))
        args.append(prev)
    return pl.pallas_call(
        _combine_kernel if prev is None else _combine_final_kernel,
        grid=(n_tiles,),
        in_specs=ins,
        out_specs=pl.BlockSpec((TM, d), lambda i: (tile_off + i, 0)),
        out_shape=jax.ShapeDtypeStruct((total_rows, d), F32),
        input_output_aliases={} if prev is None else {5: 0},
        compiler_params=_params(("parallel",)),
        name="moe_combine_out",
    )(*args)


def _rope_tables(seq, dtype):
    rows = seq // GRID_W
    row = jnp.repeat(jnp.arange(rows), GRID_W).astype(jnp.float32)
    col = (jnp.arange(rows * GRID_W) % GRID_W).astype(jnp.float32)
    half = HEAD_DIM // 2
    inv = jnp.power(ROPE_BASE, -jnp.arange(0, half, 2, dtype=jnp.float32) / half)
    ang_r, ang_c = row[:, None] * inv, col[:, None] * inv
    cos_r, cos_c = jnp.cos(ang_r).astype(dtype), jnp.cos(ang_c).astype(dtype)
    sin_r, sin_c = jnp.sin(ang_r).astype(dtype), jnp.sin(ang_c).astype(dtype)
    cos_h = jnp.concatenate([cos_r, cos_r, cos_c, cos_c], -1)
    sin_h = jnp.concatenate([-sin_r, sin_r, -sin_c, sin_c], -1)
    cos_t = jnp.concatenate([jnp.tile(cos_h, (1, D_ATT // HEAD_DIM)), jnp.ones((TM, D_ATT), dtype)], 0)
    sin_t = jnp.concatenate([jnp.tile(sin_h, (1, D_ATT // HEAD_DIM)), jnp.zeros((TM, D_ATT), dtype)], 0)
    return cos_t, sin_t


def kernel(x, c, ctx, c_ctx, ada_w, ada_b, norm1, norm2, w_in, w_out, ret_decay, win_qk_gain, win_sink,
           glb_qk_gain, w_router, b_router, w_gate_up, w_down):
    nb_all, seq, d = x.shape
    n_ctx = ctx.shape[1]
    depth = ada_w.shape[0]
    n_streams = N_STREAMS if nb_all % N_STREAMS == 0 and (nb_all // N_STREAMS * n_ctx) % TM == 0 else 1
    nb = nb_all // n_streams
    n_x = nb * seq
    ta = n_x + nb * n_ctx
    assert seq % TM == 0 and (nb * n_ctx) % TM == 0 and seq % GRID_W == 0
    tiles_per_seq = seq // TM
    n_x_tiles = n_x // TM
    n_tiles = ta // TM

    mod_rows = (nb_all + 1 + 7) // 8 * 8
    cc = jnp.zeros((mod_rows, d), F32).at[:nb_all].set(c).at[nb_all].set(c_ctx)
    mod_all = _ada(cc, ada_w, ada_b).reshape(depth, mod_rows, 6, d)

    cos_t, sin_t = _rope_tables(seq, F32)
    head_of = jnp.arange(D_ATT) // HEAD_DIM
    hm = jnp.where(head_of[:, None] == head_of[None, :], 1.0 / HEAD_DIM, 0.0).astype(BF16)
    wr_t = w_router.T.astype(BF16)
    b_r = b_router.astype(F32)[:, None]
    key_j = jnp.arange(WINDOW)[:, None]
    qry_i = jnp.tile(jnp.arange(WINDOW), WIN_Q_HEADS)[None, :]
    tri = jnp.stack([jnp.where(key_j >= qry_i, 0.0, NEG_INF), jnp.where(key_j <= qry_i, 0.0, NEG_INF)]).astype(F32)

    layer_tabs = []
    for l in range(depth):
        gains = jnp.zeros((8, D_ATT), F32)
        gains = gains.at[0].set(jnp.tile(win_qk_gain[l, 0], WIN_Q_HEADS))
        gains = gains.at[1, :WIN_KV_HEADS * HEAD_DIM].set(jnp.tile(win_qk_gain[l, 1], WIN_KV_HEADS))
        gains = gains.at[2].set(jnp.tile(glb_qk_gain[l, 0], GLB_Q_HEADS))
        gains = gains.at[3, :GLB_KV_HEADS * HEAD_DIM].set(jnp.tile(glb_qk_gain[l, 1], GLB_KV_HEADS))
        sink_row = jnp.repeat(win_sink[l].astype(F32) * LOG2_E, WINDOW)[None, :]
        layer_tabs.append((gains, sink_row, _ret_tables(ret_decay[l])))

    def mix(l, s, xa, pending, release=None):
        last = l == depth - 1
        gains, sink_row, (gc, dmat, xi, zeta) = layer_tabs[l]
        xa, px, vt = _inproj(xa, mods[s][l], norm1[l][None, :], w_in, l, cos_t, sin_t, gains, hm,
                             nb, n_x_tiles, tiles_per_seq, pending)
        if release is not None:
            px = release(px)
        o_f, o_b = _retention(px, gc, dmat, xi, zeta, nb, seq, n_ctx)
        yw = _window(px, vt, sink_row, tri, nb, seq, n_ctx, not last)
        yg = _global(px, vt, nb, seq, n_ctx, not last)
        if not last:
            yw, yg = _ctx_attention(px, vt, sink_row, yw, yg, nb, seq, n_ctx)
        return _outproj(o_f, o_b, px, yw, yg, xa, mods[s][l], norm2[l][None, :], w_out, l,
                        wr_t, b_r, nb, n_x_tiles, tiles_per_seq, n_x_tiles if last else n_tiles)

    def tie(a, b):
        if n_streams == 1:
            return a, b
        return lax.optimization_barrier((a, b))

    mods, xas = [], []
    for s in range(n_streams):
        b0 = s * nb
        mods.append(jnp.concatenate([mod_all[:, b0:b0 + nb], mod_all[:, nb_all:nb_all + 1]], axis=1))
        xas.append(jnp.concatenate([x[b0:b0 + nb].reshape(n_x, d), ctx[b0:b0 + nb].reshape(nb * n_ctx, d)], 0))

    out = None
    pending = [None] * n_streams
    carry = None
    s_last = n_streams - 1

    def gather_back(yp, pos):
        return yp[pos[0]], yp[pos[1]]

    for l in range(depth):
        last = l == depth - 1
        n_tok = n_x if last else ta
        states, srcs, plans, xss = [], [], [], []
        for s in range(n_streams):
            due, release = None, None
            if s > 0:
                def release(px, s=s):
                    px, src_t = tie(px, srcs[s - 1])
                    xss.append(states[s - 1][1][src_t])
                    return px
            elif carry is not None:
                due, yp_c, pos_c, gate_c, mod_c = carry
                carry = None

                def release(px, due=due, yp_c=yp_c, pos_c=pos_c, gate_c=gate_c, mod_c=mod_c):
                    px, yp_t = tie(px, yp_c)
                    pending[due] = gather_back(yp_t, pos_c) + (gate_c, mod_c)
                    return px
            state = list(mix(l, s, xas[s], pending[s], release))
            if s > 0:
                state[2], xss[s - 1] = tie(state[2], xss[s - 1])
            elif due is not None:
                y0d, y1d, gate_d, mod_d = pending[due]
                state[2], (y0d, y1d) = tie(state[2], (y0d, y1d))
                pending[due] = (y0d, y1d, gate_d, mod_d)
            states.append(state)
            tables, src, pos = _moe_plan(state[2], n_tok)
            plans.append((tables, pos))
            srcs.append(src)
        yp_prev = None
        for s in range(n_streams):
            xa, h2, _, gate_t = states[s]
            tables, pos = plans[s]
            xs = xss[s] if s < s_last else h2[srcs[s]]
            if s < s_last:
                xs, srcs[s + 1] = tie(xs, srcs[s + 1])
            if s > 0:
                xs, yp_prev = tie(xs, yp_prev)
                ps, ppos, pgate, pmod, pxa = prev_info
                done = gather_back(yp_prev, ppos) + (pgate, pmod)
                if last:
                    out = _combine_final(pxa, *done, tiles_per_seq, n_x_tiles, out, ps * n_x_tiles, n_streams * n_x)
                else:
                    pending[ps] = done
            yp_prev = _moe_ffn(*tables, xs, w_gate_up, w_down, l)
            prev_info = (s, pos, gate_t.T, mods[s][l], xa)
            xas[s] = xa
        ps, ppos, pgate, pmod, pxa = prev_info
        if last:
            out = _combine_final(pxa, *gather_back(yp_prev, ppos), pgate, pmod, tiles_per_seq, n_x_tiles, out,
                                 ps * n_x_tiles, n_streams * n_x)
        elif n_streams == 1:
            pending[ps] = gather_back(yp_prev, ppos) + (pgate, pmod)
        else:
            carry = (ps, yp_prev, ppos, pgate, pmod)
    return out.reshape(nb_all, seq, d)
```
